```python
import jax
import jax.numpy as jnp
from jax import lax
import numpy as np

D_MODEL = 2048
BATCH = 4
SEQ = 4096
DEPTH = 2

PLE_DIM = 256
NORM_EPS = 1e-6
ROPE_THETA = 10000.0

GDN_HEADS = 8
GDN_DK = 128
GDN_DV = 128
GDN_CONV = 4
GDN_CHUNK = 64
MLA_HEADS = 8
MLA_Q_RANK = 512
MLA_KV_RANK = 512
MLA_NOPE = 128
MLA_ROPE = 64
MLA_V = 128
MLA_BLOCK = 128
RET_HEADS = 8
RET_DK = D_MODEL // RET_HEADS
RET_DV = 2 * RET_DK
RET_CHUNK = 64
D_FF = 5632
FFN_CONV = 3

L0_SPLITS = (GDN_HEADS * GDN_DK, GDN_HEADS * GDN_DK, GDN_HEADS * GDN_DV, GDN_HEADS * GDN_DV,
             GDN_HEADS, GDN_HEADS, MLA_Q_RANK, MLA_KV_RANK, MLA_ROPE)
L0_IN = sum(L0_SPLITS)
L0_MIX = GDN_HEADS * GDN_DV + MLA_HEADS * MLA_V
L1_SPLITS = (RET_HEADS * RET_DK, RET_HEADS * RET_DK, RET_HEADS * RET_DV, RET_HEADS * RET_DV)
L1_IN = sum(L1_SPLITS)
L1_MIX = RET_HEADS * RET_DV

kernel_name = 'hybrid_gdn_mla_retnet_block'


def rmsnorm(x, g):
    xf = x.astype(jnp.float32)
    y = xf * lax.rsqrt(jnp.mean(xf * xf, axis=-1, keepdims=True) + NORM_EPS)
    return (y * g.astype(jnp.float32)).astype(x.dtype)


def split_cols(x, sizes):
    idx = [int(i) for i in np.cumsum(sizes)[:-1]]
    return jnp.split(x, idx, axis=-1)


def to_heads(x, n_heads):
    b, s, _ = x.shape
    return x.reshape(b, s, n_heads, -1).transpose(0, 2, 1, 3)


def from_heads(x):
    b, h, s, d = x.shape
    return x.transpose(0, 2, 1, 3).reshape(b, s, h * d)


def causal_dwconv(x, w):
    width, s = w.shape[0], x.shape[1]
    xp = jnp.pad(x, ((0, 0), (width - 1, 0), (0, 0)))
    y = xp[:, 0:s] * w[0]
    for j in range(1, width):
        y = y + xp[:, j:j + s] * w[j]
    return y


def rope_tables(positions, dim):
    inv_freq = ROPE_THETA ** (-jnp.arange(0, dim, 2, dtype=jnp.float32) / dim)
    ang = positions.astype(jnp.float32)[..., None] * inv_freq
    return jnp.cos(ang), jnp.sin(ang)


def apply_rope(x, cos, sin):
    x1, x2 = jnp.split(x, 2, axis=-1)
    return jnp.concatenate([x1 * cos - x2 * sin, x2 * cos + x1 * sin], axis=-1).astype(x.dtype)


def l2norm(x):
    return x * lax.rsqrt(jnp.sum(x * x, axis=-1, keepdims=True) + NORM_EPS)


def chunk(t, c):
    b, h, s = t.shape[:3]
    return t.reshape(b, h, s // c, c, *t.shape[3:])


def gated_deltanet(q, k, v, z, a, b, conv_w, A_log, dt_bias, norm_w):
    f32 = jnp.float32
    dtype = q.dtype
    bsz, s, _ = q.shape
    c = GDN_CHUNK
    qkv = jax.nn.silu(causal_dwconv(jnp.concatenate([q, k, v], axis=-1), conv_w)).astype(f32)
    q, k, v = split_cols(qkv, L0_SPLITS[:3])
    q = l2norm(to_heads(q, GDN_HEADS)) * GDN_DK ** -0.5
    k = l2norm(to_heads(k, GDN_HEADS))
    v = to_heads(v, GDN_HEADS)
    beta = jax.nn.sigmoid(b.astype(f32)).transpose(0, 2, 1)
    g = (-jnp.exp(A_log.astype(f32)) * jax.nn.softplus(a.astype(f32) + dt_bias.astype(f32))
         ).transpose(0, 2, 1)
    qc, kc, vc = chunk(q, c), chunk(k, c), chunk(v, c)
    bc = chunk(beta, c)[..., None]
    G = jnp.cumsum(chunk(g, c), axis=-1)
    incl = jnp.tril(jnp.ones((c, c), dtype=bool))
    strict = jnp.tril(jnp.ones((c, c), dtype=bool), -1)
    gamma = jnp.exp(jnp.where(incl, G[..., :, None] - G[..., None, :], -jnp.inf))
    kb = kc * bc
    A = jnp.where(strict, jnp.einsum('bhncd,bhnsd->bhncs', kb, kc) * gamma, 0.0)
    M = A + jnp.eye(c, dtype=f32)
    rhs = jnp.concatenate([vc * bc, kb * jnp.exp(G)[..., None]], axis=-1)
    sol = lax.linalg.triangular_solve(M, rhs, left_side=True, lower=True, unit_diagonal=True)
    u, w = sol[..., :GDN_DV], sol[..., GDN_DV:]
    attn = jnp.einsum('bhncd,bhnsd->bhncs', qc, kc) * gamma
    q_dec = qc * jnp.exp(G)[..., None]
    g_last = G[..., -1]
    k_dec = kc * jnp.exp(g_last[..., None] - G)[..., None]

    def step(state, xs):
        u_n, w_n, qd_n, kd_n, attn_n, gl_n = xs
        v_new = u_n - jnp.einsum('bhcd,bhde->bhce', w_n, state)
        o_n = jnp.einsum('bhcd,bhde->bhce', qd_n, state) + jnp.einsum('bhcs,bhse->bhce', attn_n, v_new)
        state = state * jnp.exp(gl_n)[..., None, None] + jnp.einsum('bhcd,bhce->bhde', kd_n, v_new)
        return state, o_n

    xs = tuple(jnp.moveaxis(t, 2, 0) for t in (u, w, q_dec, k_dec, attn, g_last))
    state0 = jnp.zeros((bsz, GDN_HEADS, GDN_DK, GDN_DV), f32)
    _, o = lax.scan(step, state0, xs)
    o = jnp.moveaxis(o, 0, 2).reshape(bsz, GDN_HEADS, s, GDN_DV)
    o = rmsnorm(o, norm_w) * jax.nn.silu(to_heads(z.astype(f32), GDN_HEADS))
    return from_heads(o).astype(dtype)


def mla(c_q, c_kv, k_rope, cos, sin, q_norm, w_uq, kv_norm, w_ukv):
    bsz, s, _ = c_q.shape
    q = to_heads(rmsnorm(c_q, q_norm) @ w_uq, MLA_HEADS)
    q_nope = q[..., :MLA_NOPE]
    q_pe = apply_rope(q[..., MLA_NOPE:], cos[:, None], sin[:, None])
    kv = to_heads(rmsnorm(c_kv, kv_norm) @ w_ukv, MLA_HEADS)
    k_nope, v = kv[..., :MLA_NOPE], kv[..., MLA_NOPE:]
    k_pe = apply_rope(k_rope, cos, sin)
    scale = (MLA_NOPE + MLA_ROPE) ** -0.5
    nb = s // MLA_BLOCK
    qn_b = q_nope.reshape(bsz, MLA_HEADS, nb, MLA_BLOCK, MLA_NOPE).transpose(2, 0, 1, 3, 4)
    qp_b = q_pe.reshape(bsz, MLA_HEADS, nb, MLA_BLOCK, MLA_ROPE).transpose(2, 0, 1, 3, 4)
    key_idx = jnp.arange(s)

    def block(args):
        i, qn, qp = args
        sc = (jnp.einsum('bhqd,bhkd->bhqk', qn, k_nope)
              + jnp.einsum('bhqr,bkr->bhqk', qp, k_pe)).astype(jnp.float32) * scale
        q_idx = i * MLA_BLOCK + jnp.arange(MLA_BLOCK)
        sc = jnp.where(key_idx[None, :] <= q_idx[:, None], sc, -jnp.inf)
        pr = jax.nn.softmax(sc, axis=-1).astype(v.dtype)
        return jnp.einsum('bhqk,bhkd->bhqd', pr, v)

    o = lax.map(block, (jnp.arange(nb), qn_b, qp_b))
    o = o.transpose(1, 2, 0, 3, 4).reshape(bsz, MLA_HEADS, s, MLA_V)
    return from_heads(o)


def retention(q, k, v, cos, sin, norm_w):
    f32 = jnp.float32
    dtype = q.dtype
    bsz, s, _ = q.shape
    c = RET_CHUNK
    q = apply_rope(to_heads(q.astype(f32), RET_HEADS), cos[:, None], sin[:, None])
    k = apply_rope(to_heads(k.astype(f32), RET_HEADS), cos[:, None], sin[:, None]) * RET_DK ** -0.5
    v = to_heads(v.astype(f32), RET_HEADS)
    log_gamma = jnp.log1p(-jnp.power(2.0, -5.0 - jnp.arange(RET_HEADS, dtype=f32)))
    pos = jnp.arange(c, dtype=f32)
    incl = jnp.tril(jnp.ones((c, c), dtype=bool))
    decay = jnp.exp(jnp.where(incl, (pos[:, None] - pos[None, :]) * log_gamma[:, None, None], -jnp.inf))
    qc, kc, vc = chunk(q, c), chunk(k, c), chunk(v, c)
    inner = jnp.einsum('bhncs,bhnse->bhnce', jnp.einsum('bhncd,bhnsd->bhncs', qc, kc) * decay[:, None], vc)
    xi = jnp.exp((pos + 1.0) * log_gamma[:, None])
    zeta = jnp.exp((c - 1.0 - pos) * log_gamma[:, None])
    gamma_c = jnp.exp(c * log_gamma)
    q_dec = qc * xi[:, None, :, None]
    k_dec = kc * zeta[:, None, :, None]

    def step(state, xs):
        qd_n, kd_n, v_n = xs
        cross = jnp.einsum('bhcd,bhde->bhce', qd_n, state)
        state = state * gamma_c[:, None, None] + jnp.einsum('bhcd,bhce->bhde', kd_n, v_n)
        return state, cross

    xs = tuple(jnp.moveaxis(t, 2, 0) for t in (q_dec, k_dec, vc))
    state0 = jnp.zeros((bsz, RET_HEADS, RET_DK, RET_DV), f32)
    _, cross = lax.scan(step, state0, xs)
    o = (inner + jnp.moveaxis(cross, 0, 2)).reshape(bsz, RET_HEADS, s, RET_DV)
    mu = jnp.mean(o, axis=-1, keepdims=True)
    var = jnp.mean(jnp.square(o - mu), axis=-1, keepdims=True)
    o = (o - mu) * lax.rsqrt(var + NORM_EPS)
    return (from_heads(o) * norm_w.astype(f32)).astype(dtype)


def even_mixer(hn, cos, sin, w_in, gdn_conv, gdn_A_log, gdn_dt_bias, gdn_norm,
               q_norm, w_uq, kv_norm, w_ukv, w_out):
    q, k, v, z, a, b, c_q, c_kv, k_rope = split_cols(hn @ w_in, L0_SPLITS)
    y_a = gated_deltanet(q, k, v, z, a, b, gdn_conv, gdn_A_log, gdn_dt_bias, gdn_norm)
    y_b = mla(c_q, c_kv, k_rope, cos, sin, q_norm, w_uq, kv_norm, w_ukv)
    return jnp.concatenate([y_a, y_b], axis=-1) @ w_out


def odd_mixer(hn, cos, sin, w_in, ret_norm, w_out):
    q, k, v, g = split_cols(hn @ w_in, L1_SPLITS)
    y = retention(q, k, v, cos, sin, ret_norm)
    return (jax.nn.silu(g) * y) @ w_out


def conv_ffn(hn, w_up, conv_w, conv_b, w_down):
    u = causal_dwconv(hn @ w_up, conv_w) + conv_b
    gate, up = jnp.split(u, 2, axis=-1)
    return (jax.nn.silu(gate) * up) @ w_down


def ple_term(h, p_i, w_proj, gate_norm, w_gate):
    return (p_i @ w_proj) * jax.nn.sigmoid(rmsnorm(h, gate_norm) @ w_gate)


def setup_inputs(seed: int = 0) -> dict:
    key = jax.random.key(seed)
    keys = jax.random.split(key, 64)
    counter = [0]

    def nk():
        kk = keys[counter[0]]
        counter[0] += 1
        return kk

    def dense(fan_in, fan_out):
        return jax.random.normal(nk(), (fan_in, fan_out), jnp.float32) * fan_in ** -0.5

    def gain(n):
        return 1.0 + 0.02 * jax.random.normal(nk(), (n,), jnp.float32)

    x = jax.random.normal(nk(), (BATCH, SEQ, D_MODEL), jnp.float32)
    p = jax.random.normal(nk(), (DEPTH, BATCH, SEQ, PLE_DIM), jnp.float32)
    offset = jax.random.randint(nk(), (BATCH, 1), 0, 1024, jnp.int32)
    positions = offset + jnp.arange(SEQ, dtype=jnp.int32)[None, :]

    gdn_ch = 2 * GDN_HEADS * GDN_DK + GDN_HEADS * GDN_DV
    dt = jnp.exp(jax.random.uniform(nk(), (GDN_HEADS,), jnp.float32, np.log(1e-3), np.log(1e-1)))

    def ffn_params():
        return (gain(D_MODEL), dense(D_MODEL, 2 * D_FF),
                jax.random.normal(nk(), (FFN_CONV, 2 * D_FF), jnp.float32) * FFN_CONV ** -0.5,
                0.01 * jax.random.normal(nk(), (2 * D_FF,), jnp.float32),
                dense(D_FF, D_MODEL))

    def ple_params():
        return (dense(PLE_DIM, D_MODEL), gain(D_MODEL), dense(D_MODEL, D_MODEL))

    out = {'x': x, 'p': p, 'positions': positions}
    out['l0_attn_norm'] = gain(D_MODEL)
    out['l0_w_in'] = dense(D_MODEL, L0_IN)
    out['l0_gdn_conv'] = jax.random.normal(nk(), (GDN_CONV, gdn_ch), jnp.float32) * GDN_CONV ** -0.5
    out['l0_gdn_A_log'] = jnp.log(jax.random.uniform(nk(), (GDN_HEADS,), jnp.float32, 1.0, 16.0))
    out['l0_gdn_dt_bias'] = dt + jnp.log(-jnp.expm1(-dt))
    out['l0_gdn_norm'] = gain(GDN_DV)
    out['l0_mla_q_norm'] = gain(MLA_Q_RANK)
    out['l0_mla_w_uq'] = dense(MLA_Q_RANK, MLA_HEADS * (MLA_NOPE + MLA_ROPE))
    out['l0_mla_kv_norm'] = gain(MLA_KV_RANK)
    out['l0_mla_w_ukv'] = dense(MLA_KV_RANK, MLA_HEADS * (MLA_NOPE + MLA_V))
    out['l0_w_out'] = dense(L0_MIX, D_MODEL)
    (out['l0_ffn_norm'], out['l0_ffn_w_up'], out['l0_ffn_conv_w'], out['l0_ffn_conv_b'],
     out['l0_ffn_w_down']) = ffn_params()
    out['l0_ple_proj'], out['l0_ple_gate_norm'], out['l0_ple_gate'] = ple_params()
    out['l1_attn_norm'] = gain(D_MODEL)
    out['l1_w_in'] = dense(D_MODEL, L1_IN)
    out['l1_ret_norm'] = gain(L1_MIX)
    out['l1_w_out'] = dense(L1_MIX, D_MODEL)
    (out['l1_ffn_norm'], out['l1_ffn_w_up'], out['l1_ffn_conv_w'], out['l1_ffn_conv_b'],
     out['l1_ffn_w_down']) = ffn_params()
    out['l1_ple_proj'], out['l1_ple_gate_norm'], out['l1_ple_gate'] = ple_params()
    out['final_norm'] = gain(D_MODEL)
    return out


def reference(x, p, positions,
              l0_attn_norm, l0_w_in, l0_gdn_conv, l0_gdn_A_log, l0_gdn_dt_bias, l0_gdn_norm,
              l0_mla_q_norm, l0_mla_w_uq, l0_mla_kv_norm, l0_mla_w_ukv, l0_w_out,
              l0_ffn_norm, l0_ffn_w_up, l0_ffn_conv_w, l0_ffn_conv_b, l0_ffn_w_down,
              l0_ple_proj, l0_ple_gate_norm, l0_ple_gate,
              l1_attn_norm, l1_w_in, l1_ret_norm, l1_w_out,
              l1_ffn_norm, l1_ffn_w_up, l1_ffn_conv_w, l1_ffn_conv_b, l1_ffn_w_down,
              l1_ple_proj, l1_ple_gate_norm, l1_ple_gate,
              final_norm):
    cos_mla, sin_mla = rope_tables(positions, MLA_ROPE)
    cos_ret, sin_ret = rope_tables(positions, RET_DK)
    mixers = (
        lambda hn: even_mixer(hn, cos_mla, sin_mla, l0_w_in, l0_gdn_conv, l0_gdn_A_log, l0_gdn_dt_bias,
                              l0_gdn_norm, l0_mla_q_norm, l0_mla_w_uq, l0_mla_kv_norm, l0_mla_w_ukv, l0_w_out),
        lambda hn: odd_mixer(hn, cos_ret, sin_ret, l1_w_in, l1_ret_norm, l1_w_out),
    )
    attn_norms = (l0_attn_norm, l1_attn_norm)
    ffn_params = ((l0_ffn_norm, l0_ffn_w_up, l0_ffn_conv_w, l0_ffn_conv_b, l0_ffn_w_down),
                  (l1_ffn_norm, l1_ffn_w_up, l1_ffn_conv_w, l1_ffn_conv_b, l1_ffn_w_down))
    ple_params = ((l0_ple_proj, l0_ple_gate_norm, l0_ple_gate),
                  (l1_ple_proj, l1_ple_gate_norm, l1_ple_gate))
    h = x
    for i in range(DEPTH):
        h = h + mixers[i](rmsnorm(h, attn_norms[i]))
        ffn_norm, w_up, conv_w, conv_b, w_down = ffn_params[i]
        h = h + conv_ffn(rmsnorm(h, ffn_norm), w_up, conv_w, conv_b, w_down)
        w_proj, gate_norm, w_gate = ple_params[i]
        h = h + ple_term(h, p[i], w_proj, gate_norm, w_gate)
    return rmsnorm(h, final_norm)
```

```python
import functools

import jax
import jax.numpy as jnp
from jax import lax
from jax.experimental import pallas as pl
from jax.experimental.pallas import tpu as pltpu

F32 = jnp.float32
BF16 = jnp.bfloat16

NORM_EPS = 1e-6
ROPE_THETA = 10000.0

GDN_HEADS = 8
GDN_D = 128
GDN_CONV = 4
GDN_CHUNK = 64
MLA_HEADS = 8
MLA_RANK = 512
MLA_NOPE = 128
MLA_ROPE = 64
MLA_V = 128
RET_HEADS = 8
RET_DK = 256
RET_DV = 512
FFN_CONV = 3

LANES = 128
VMEM_LIMIT = 56 * 1024 * 1024


def _params(n_grid):
    return pltpu.CompilerParams(dimension_semantics=("arbitrary",) * n_grid,
                                vmem_limit_bytes=VMEM_LIMIT)


def _rms(x, g):
    return x * lax.rsqrt(jnp.mean(x * x, axis=-1, keepdims=True) + NORM_EPS) * g


def _dot(a, b):
    return jnp.dot(a, b, preferred_element_type=F32)


def _dot_nt(a, b):
    return lax.dot_general(a, b, (((1,), (1,)), ((), ())), preferred_element_type=F32)


def _dot_tn(a, b):
    return lax.dot_general(a, b, (((0,), (0,)), ((), ())), preferred_element_type=F32)


def _silu(x):
    return x * jax.nn.sigmoid(x)


def _norm_proj_kernel(x_ref, g_ref, w_ref, *rest, with_side):
    if with_side:
        ws_ref, o_ref, os_ref, xn_ref = rest
    else:
        o_ref, xn_ref = rest

    @pl.when(pl.program_id(1) == 0)
    def _():
        xn = _rms(x_ref[...], g_ref[...]).astype(BF16)
        xn_ref[...] = xn
        if with_side:
            os_ref[...] = _dot(xn, ws_ref[...])

    o_ref[...] = _dot(xn_ref[...], w_ref[...]).astype(o_ref.dtype)


def _norm_proj(x, g, w, w_side=None, *, bm, bn):
    t, d = x.shape
    n = w.shape[1]
    in_specs = [pl.BlockSpec((bm, d), lambda i, j: (i, 0)),
                pl.BlockSpec((1, d), lambda i, j: (0, 0)),
                pl.BlockSpec((d, bn), lambda i, j: (0, j))]
    out_shape = [jax.ShapeDtypeStruct((t, n), BF16)]
    out_specs = [pl.BlockSpec((bm, bn), lambda i, j: (i, j))]
    args = [x, g.reshape(1, d), w]
    if w_side is not None:
        ns = w_side.shape[1]
        in_specs.append(pl.BlockSpec((d, ns), lambda i, j: (0, 0)))
        out_shape.append(jax.ShapeDtypeStruct((t, ns), F32))
        out_specs.append(pl.BlockSpec((bm, ns), lambda i, j: (i, 0)))
        args.append(w_side)
    out = pl.pallas_call(
        functools.partial(_norm_proj_kernel, with_side=w_side is not None),
        grid=(t // bm, n // bn),
        in_specs=in_specs, out_specs=out_specs, out_shape=out_shape,
        scratch_shapes=[pltpu.VMEM((bm, d), BF16)],
        compiler_params=_params(2), name="norm_proj",
    )(*args)
    return out if w_side is not None else out[0]


def _proj_res_kernel(*refs, n_pairs):
    res_ref, o_ref = refs[2 * n_pairs], refs[2 * n_pairs + 1]
    acc = res_ref[...]
    for p in range(n_pairs):
        acc = acc + _dot(refs[2 * p][...], refs[2 * p + 1][...])
    o_ref[...] = acc


def _proj_res(pairs, res, *, bm, bn):
    t, n = res.shape
    in_specs, args = [], []
    for y, w in pairs:
        kk = y.shape[1]
        in_specs += [pl.BlockSpec((bm, kk), lambda i, j: (i, 0)),
                     pl.BlockSpec((kk, bn), lambda i, j: (0, j))]
        args += [y, w]
    in_specs.append(pl.BlockSpec((bm, bn), lambda i, j: (i, j)))
    args.append(res)
    return pl.pallas_call(
        functools.partial(_proj_res_kernel, n_pairs=len(pairs)),
        grid=(t // bm, n // bn),
        in_specs=in_specs,
        out_specs=pl.BlockSpec((bm, bn), lambda i, j: (i, j)),
        out_shape=jax.ShapeDtypeStruct((t, n), F32),
        compiler_params=_params(2), name="proj_res",
    )(*args)


FFN_HALO = 16


def _ffn_up_kernel(h_ref, hh_ref, g_ref, wg_ref, wu_ref, cg_ref, cu_ref, bg_ref, bu_ref,
                   o_ref, xn_ref, ug_ref, uu_ref, *, bm, blocks_per_seq):
    i = pl.program_id(0)

    @pl.when(pl.program_id(1) == 0)
    def _():
        g = g_ref[...]
        halo = _rms(hh_ref[...], g)
        halo = jnp.where(i % blocks_per_seq == 0, 0.0, halo)
        xn_ref[0:FFN_HALO, :] = halo.astype(BF16)
        xn_ref[FFN_HALO:, :] = _rms(h_ref[...], g).astype(BF16)

    xn = xn_ref[...]
    ug_ref[...] = _dot(xn, wg_ref[...])
    uu_ref[...] = _dot(xn, wu_ref[...])

    def conv(u_ref, c_ref, b_ref):
        c = c_ref[...]
        y = b_ref[...] + c[2:3] * u_ref[pl.ds(FFN_HALO, bm), :]
        y = y + c[1:2] * u_ref[pl.ds(FFN_HALO - 1, bm), :]
        y = y + c[0:1] * u_ref[pl.ds(FFN_HALO - 2, bm), :]
        return y

    gate = conv(ug_ref, cg_ref, bg_ref)
    up = conv(uu_ref, cu_ref, bu_ref)
    o_ref[...] = (_silu(gate) * up).astype(o_ref.dtype)


def _ffn_up(h, g, w_up, conv_w, conv_b, *, seq, bm, bn):
    t, d = h.shape
    dff = w_up.shape[1] // 2
    nj = dff // bn
    hb = bm // FFN_HALO
    kern = functools.partial(_ffn_up_kernel, bm=bm, blocks_per_seq=seq // bm)
    return pl.pallas_call(
        kern,
        grid=(t // bm, nj),
        in_specs=[
            pl.BlockSpec((bm, d), lambda i, j: (i, 0)),
            pl.BlockSpec((FFN_HALO, d), lambda i, j: (jnp.maximum(i * hb - 1, 0), 0)),
            pl.BlockSpec((1, d), lambda i, j: (0, 0)),
            pl.BlockSpec((d, bn), lambda i, j: (0, j)),
            pl.BlockSpec((d, bn), lambda i, j: (0, j + nj)),
            pl.BlockSpec((FFN_CONV, bn), lambda i, j: (0, j)),
            pl.BlockSpec((FFN_CONV, bn), lambda i, j: (0, j + nj)),
            pl.BlockSpec((1, bn), lambda i, j: (0, j)),
            pl.BlockSpec((1, bn), lambda i, j: (0, j + nj)),
        ],
        out_specs=pl.BlockSpec((bm, bn), lambda i, j: (i, j)),
        out_shape=jax.ShapeDtypeStruct((t, dff), BF16),
        scratch_shapes=[pltpu.VMEM((bm + FFN_HALO, d), BF16),
                        pltpu.VMEM((bm + FFN_HALO, bn), F32),
                        pltpu.VMEM((bm + FFN_HALO, bn), F32)],
        compiler_params=_params(2), name="ffn_up",
    )(h, h, g.reshape(1, d), w_up, w_up, conv_w, conv_w,
      conv_b.reshape(1, -1), conv_b.reshape(1, -1))


def _ple_kernel(h_ref, p_ref, gn_ref, wp_ref, wg_ref, *rest, bn, final):
    if final:
        fn_ref, o_ref = rest
    else:
        (o_ref,) = rest
    d = h_ref.shape[1]
    hn = _rms(h_ref[...], gn_ref[...]).astype(BF16)
    pb = p_ref[...].astype(BF16)
    ssq = None
    for c in range(d // bn):
        cols = slice(c * bn, (c + 1) * bn)
        gate = _dot(hn, wg_ref[:, cols])
        proj = _dot(pb, wp_ref[:, cols])
        out = h_ref[:, cols] + proj * jax.nn.sigmoid(gate)
        o_ref[:, cols] = out
        if final:
            part = jnp.sum(out * out, axis=-1, keepdims=True)
            ssq = part if ssq is None else ssq + part
    if final:
        scale = lax.rsqrt(ssq / d + NORM_EPS)
        o_ref[...] = o_ref[...] * scale * fn_ref[...]


def _ple(h, p, gate_norm, w_proj, w_gate, final_norm=None, *, bm, bn):
    t, d = h.shape
    dp = p.shape[1]
    final = final_norm is not None
    in_specs = [pl.BlockSpec((bm, d), lambda i: (i, 0)),
                pl.BlockSpec((bm, dp), lambda i: (i, 0)),
                pl.BlockSpec((1, d), lambda i: (0, 0)),
                pl.BlockSpec((dp, d), lambda i: (0, 0)),
                pl.BlockSpec((d, d), lambda i: (0, 0))]
    args = [h, p, gate_norm.reshape(1, d), w_proj, w_gate]
    if final:
        in_specs.append(pl.BlockSpec((1, d), lambda i: (0, 0)))
        args.append(final_norm.reshape(1, d))
    return pl.pallas_call(
        functools.partial(_ple_kernel, bn=bn, final=final),
        grid=(t // bm,),
        in_specs=in_specs,
        out_specs=pl.BlockSpec((bm, d), lambda i: (i, 0)),
        out_shape=jax.ShapeDtypeStruct((t, d), F32),
        compiler_params=_params(1), name="ple",
    )(*args)


def _split3(x):
    hi = x.astype(BF16).astype(F32)
    r1 = x - hi
    mid = r1.astype(BF16).astype(F32)
    lo = (r1 - mid).astype(BF16).astype(F32)
    return hi, mid, lo


def _lane_pick(x, lane, idx):
    return jnp.sum(jnp.where(lane == idx, x, 0.0), axis=-1, keepdims=True)


def _gdn_kernel(q_ref, k_ref, v_ref, z_ref, ab_ref, cq_ref, ck_ref, cv_ref, hp_ref, nw_ref,
                tril_ref, o_ref, xq_ref, xk_ref, xv_ref, s_ref, *, lb):
    c = GDN_CHUNK
    n_chunks = lb // c
    head = pl.program_id(1)
    halo = 8

    @pl.when(pl.program_id(2) == 0)
    def _():
        for x_ref in (xq_ref, xk_ref, xv_ref):
            x_ref[0:halo, :] = jnp.zeros((halo, GDN_D), F32)
        s_ref[...] = jnp.zeros_like(s_ref)

    def conv(x_ref, in_ref, cw_ref):
        x_ref[halo:, :] = in_ref[...].astype(F32)
        w = cw_ref[...]
        y = w[0:1] * x_ref[pl.ds(halo - 3, lb), :]
        for j in range(1, GDN_CONV):
            y = y + w[j:j + 1] * x_ref[pl.ds(halo - 3 + j, lb), :]
        x_ref[0:halo, :] = x_ref[lb:lb + halo, :]
        return _silu(y)

    def l2n(x):
        return x * lax.rsqrt(jnp.sum(x * x, axis=-1, keepdims=True) + NORM_EPS)

    q = l2n(conv(xq_ref, q_ref, cq_ref)) * (GDN_D ** -0.5)
    k = l2n(conv(xk_ref, k_ref, ck_ref))
    v = conv(xv_ref, v_ref, cv_ref)

    ab = ab_ref[...]
    lane = lax.broadcasted_iota(jnp.int32, ab.shape, 1)
    hp = hp_ref[...]
    xa = ab + hp[1:2]
    softplus = jnp.maximum(xa, 0.0) + jnp.log1p(jnp.exp(-jnp.abs(xa)))
    g_col = _lane_pick(-jnp.exp(hp[0:1]) * softplus, lane, head)
    beta = _lane_pick(jax.nn.sigmoid(ab), lane, head + GDN_HEADS)

    def pieces_to_lanes(x, first, sign):
        hi, mid, lo = _split3(x)
        out = jnp.where(lane == first, sign * hi, 0.0)
        out = jnp.where(lane == first + 1, sign * mid, out)
        return jnp.where(lane == first + 2, sign * lo, out)

    gm = _dot(tril_ref[...], pieces_to_lanes(g_col, 0, 1.0).astype(BF16))
    big_g = jnp.sum(gm, axis=-1, keepdims=True)
    e_g = jnp.exp(big_g)

    am = jnp.where(lane < 3, pieces_to_lanes(big_g, 0, 1.0), jnp.where(lane < 6, 1.0, 0.0)).astype(BF16)
    bmat = jnp.where(lane < 3, 1.0, pieces_to_lanes(big_g, 3, -1.0)).astype(BF16)

    row = lax.broadcasted_iota(jnp.int32, (c, c), 0)
    col = lax.broadcasted_iota(jnp.int32, (c, c), 1)
    incl = row >= col
    strict = row > col
    eye = (row == col).astype(F32)

    kb = k * beta
    vb = v * beta
    kbg = kb * e_g
    qd = q * e_g

    us, ws, attns, kds, gls = [], [], [], [], []
    for n in range(n_chunks):
        r = slice(n * c, (n + 1) * c)
        diff = _dot_nt(am[r], bmat[r])
        gamma = jnp.exp(jnp.where(incl, diff, -jnp.inf))
        kn = k[r].astype(BF16)
        a = jnp.where(strict, _dot_nt(kb[r].astype(BF16), kn) * gamma, 0.0)
        attns.append((_dot_nt(q[r].astype(BF16), kn) * gamma).astype(BF16))
        pw = -a
        t_inv = eye + pw
        for _ in range(5):
            pwb = pw.astype(BF16)
            pw = _dot(pwb, pwb)
            t_inv = t_inv + _dot(t_inv.astype(BF16), pw.astype(BF16))
        rhs = jnp.concatenate([vb[r], kbg[r]], axis=-1).astype(BF16)
        sol = _dot(t_inv.astype(BF16), rhs)
        us.append(sol[:, :GDN_D])
        ws.append(sol[:, GDN_D:])
        g_last = big_g[(n + 1) * c - 1:(n + 1) * c, :]
        kds.append((k[r] * jnp.exp(g_last - big_g[r])).astype(BF16))
        gls.append(jnp.exp(g_last))

    state = s_ref[...]
    outs = []
    for n in range(n_chunks):
        r = slice(n * c, (n + 1) * c)
        lhs = jnp.concatenate([ws[n], qd[r]], axis=0).astype(BF16)
        prod = _dot(lhs, state.astype(BF16))
        v_new = us[n] - prod[:c]
        v_new_b = v_new.astype(BF16)
        outs.append(prod[c:] + _dot(attns[n], v_new_b))
        state = state * gls[n] + _dot_tn(kds[n], v_new_b)
    s_ref[...] = state

    o = jnp.concatenate(outs, axis=0)
    o_ref[...] = (_rms(o, nw_ref[...]) * _silu(z_ref[...].astype(F32))).astype(o_ref.dtype)


def _gdn(main, side, conv_w, a_log, dt_bias, norm_w, *, batch, seq, lb):
    t = main.shape[0]
    ns = seq // lb
    hh = GDN_HEADS
    hp = jnp.zeros((8, LANES), F32)
    hp = hp.at[0, :hh].set(a_log).at[1, :hh].set(dt_bias)
    r = jnp.arange(lb)
    tril = ((r[:, None] >= r[None, :]) & (r[:, None] // GDN_CHUNK == r[None, :] // GDN_CHUNK)).astype(BF16)
    row = lambda b, h, s: b * ns + s
    return pl.pallas_call(
        functools.partial(_gdn_kernel, lb=lb),
        grid=(batch, hh, ns),
        in_specs=[
            pl.BlockSpec((lb, GDN_D), lambda b, h, s: (row(b, h, s), h)),
            pl.BlockSpec((lb, GDN_D), lambda b, h, s: (row(b, h, s), hh + h)),
            pl.BlockSpec((lb, GDN_D), lambda b, h, s: (row(b, h, s), 2 * hh + h)),
            pl.BlockSpec((lb, GDN_D), lambda b, h, s: (row(b, h, s), 3 * hh + h)),
            pl.BlockSpec((lb, LANES), lambda b, h, s: (row(b, h, s), 1)),
            pl.BlockSpec((GDN_CONV, GDN_D), lambda b, h, s: (0, h)),
            pl.BlockSpec((GDN_CONV, GDN_D), lambda b, h, s: (0, hh + h)),
            pl.BlockSpec((GDN_CONV, GDN_D), lambda b, h, s: (0, 2 * hh + h)),
            pl.BlockSpec((8, LANES), lambda b, h, s: (0, 0)),
            pl.BlockSpec((1, GDN_D), lambda b, h, s: (0, 0)),
            pl.BlockSpec((lb, lb), lambda b, h, s: (0, 0)),
        ],
        out_specs=pl.BlockSpec((lb, GDN_D), lambda b, h, s: (row(b, h, s), h)),
        out_shape=jax.ShapeDtypeStruct((t, hh * GDN_D), BF16),
        scratch_shapes=[pltpu.VMEM((lb + 8, GDN_D), F32)] * 3 + [pltpu.VMEM((GDN_D, GDN_D), F32)],
        compiler_params=_params(3), name="gdn",
    )(main, main, main, main, side, conv_w, conv_w, conv_w, hp, norm_w.reshape(1, GDN_D), tril)


def _mla_up_kernel(cq_ref, ckv_ref, side_ref, tab_ref, qn_ref, kvn_ref, wq_ref, wkv_ref,
                   q_out, k_out, v_out):
    hh = MLA_HEADS
    tab = tab_ref[...]
    scale = (MLA_NOPE + MLA_ROPE) ** -0.5

    cq = _rms(cq_ref[...].astype(F32), qn_ref[...]).astype(BF16)
    for h in range(hh):
        qh = _dot(cq, wq_ref[:, h * 256:(h + 1) * 256])
        q_out[:, h * 256:h * 256 + LANES] = (qh[:, :LANES] * scale).astype(BF16)
        q_out[:, h * 256 + LANES:(h + 1) * 256] = (qh[:, LANES:] * tab * scale).astype(BF16)

    kr = side_ref[...] * tab
    kpe = (kr + pltpu.roll(kr, shift=MLA_ROPE, axis=1)).astype(BF16)

    ckv = _rms(ckv_ref[...].astype(F32), kvn_ref[...]).astype(BF16)
    for h in range(hh):
        k_out[:, h * 256:h * 256 + LANES] = _dot(ckv, wkv_ref[:, h * LANES:(h + 1) * LANES]).astype(BF16)
        k_out[:, h * 256 + LANES:(h + 1) * 256] = kpe
    v_out[...] = _dot(ckv, wkv_ref[:, hh * LANES:]).astype(BF16)


def _mla_up(main, side, tab, q_norm, kv_norm, wq, wkv, *, bm):
    t = main.shape[0]
    r = MLA_RANK
    cq_blk = (GDN_HEADS * GDN_D * 4) // r
    return pl.pallas_call(
        _mla_up_kernel,
        grid=(t // bm,),
        in_specs=[
            pl.BlockSpec((bm, r), lambda i: (i, cq_blk)),
            pl.BlockSpec((bm, r), lambda i: (i, cq_blk + 1)),
            pl.BlockSpec((bm, LANES), lambda i: (i, 0)),
            pl.BlockSpec((bm, LANES), lambda i: (i, 0)),
            pl.BlockSpec((1, r), lambda i: (0, 0)),
            pl.BlockSpec((1, r), lambda i: (0, 0)),
            pl.BlockSpec(wq.shape, lambda i: (0, 0)),
            pl.BlockSpec(wkv.shape, lambda i: (0, 0)),
        ],
        out_specs=[pl.BlockSpec((bm, MLA_HEADS * 256), lambda i: (i, 0)),
                   pl.BlockSpec((bm, MLA_HEADS * 256), lambda i: (i, 0)),
                   pl.BlockSpec((bm, MLA_HEADS * MLA_V), lambda i: (i, 0))],
        out_shape=[jax.ShapeDtypeStruct((t, MLA_HEADS * 256), BF16),
                   jax.ShapeDtypeStruct((t, MLA_HEADS * 256), BF16),
                   jax.ShapeDtypeStruct((t, MLA_HEADS * MLA_V), BF16)],
        compiler_params=_params(1), name="mla_up",
    )(main, main, side, tab, q_norm.reshape(1, r), kv_norm.reshape(1, r), wq, wkv)


def _flash_kernel(q_ref, k_ref, v_ref, o_ref, *, blk):
    qi = pl.program_id(2)
    q = q_ref[...]

    def step(j, carry, masked):
        m, l, acc = carry
        start = pl.multiple_of(j * blk, blk)
        s = _dot_nt(q, k_ref[pl.ds(start, blk), :])
        if masked:
            row = lax.broadcasted_iota(jnp.int32, s.shape, 0)
            col = lax.broadcasted_iota(jnp.int32, s.shape, 1)
            s = jnp.where(col <= row, s, -jnp.inf)
        m_new = jnp.maximum(m, jnp.max(s, axis=-1, keepdims=True))
        alpha = jnp.exp(m - m_new)
        p = jnp.exp(s - m_new)
        l = l * alpha + jnp.sum(p, axis=-1, keepdims=True)
        acc = acc * alpha + _dot(p.astype(BF16), v_ref[pl.ds(start, blk), :])
        return m_new, l, acc

    init = (jnp.full((blk, 1), -jnp.inf, F32), jnp.zeros((blk, 1), F32),
            jnp.zeros((blk, MLA_V), F32))
    carry = lax.fori_loop(0, qi, lambda j, cr: step(j, cr, False), init)
    _, l, acc = step(qi, carry, True)
    o_ref[...] = (acc / l).astype(o_ref.dtype)


def _flash(qf, kf, vf, *, batch, seq, blk):
    t = qf.shape[0]
    nq = seq // blk
    hh = MLA_HEADS
    kf3 = kf.reshape(batch, seq, hh * 256)
    vf3 = vf.reshape(batch, seq, hh * MLA_V)
    return pl.pallas_call(
        functools.partial(_flash_kernel, blk=blk),
        grid=(batch, hh, nq),
        in_specs=[
            pl.BlockSpec((blk, 256), lambda b, h, i: (b * nq + i, h)),
            pl.BlockSpec((None, seq, 256), lambda b, h, i: (b, 0, h)),
            pl.BlockSpec((None, seq, MLA_V), lambda b, h, i: (b, 0, h)),
        ],
        out_specs=pl.BlockSpec((blk, MLA_V), lambda b, h, i: (b * nq + i, h)),
        out_shape=jax.ShapeDtypeStruct((t, hh * MLA_V), BF16),
        compiler_params=_params(3), name="mla_flash",
    )(qf, kf3, vf3)


def _ret_kernel(lg_ref, q_ref, k_ref, v_ref, g_ref, cs_ref, nw_ref, o_ref,
                s_ref, dec_ref, xz_ref, *, c):
    half = RET_DK // 2

    @pl.when(pl.program_id(2) == 0)
    def _():
        lg = lg_ref[:, 0:1]
        row = lax.broadcasted_iota(jnp.int32, (c, c), 0)
        col = lax.broadcasted_iota(jnp.int32, (c, c), 1)
        dec_ref[...] = jnp.exp(jnp.where(row >= col, (row - col).astype(F32) * lg, -jnp.inf))
        pos = lax.broadcasted_iota(jnp.int32, (c, LANES), 0).astype(F32)
        lane = lax.broadcasted_iota(jnp.int32, (c, LANES), 1)
        xi = jnp.exp((pos + 1.0) * lg)
        zeta = jnp.exp((c - 1.0 - pos) * lg)
        xz_ref[...] = jnp.where(lane == 0, xi, zeta)
        s_ref[...] = jnp.zeros_like(s_ref)

    cos = cs_ref[:, :half]
    sin = cs_ref[:, half:]

    def rope(x_ref):
        x1 = x_ref[:, :half].astype(F32)
        x2 = x_ref[:, half:].astype(F32)
        return jnp.concatenate([x1 * cos - x2 * sin, x2 * cos + x1 * sin], axis=-1)

    q = rope(q_ref)
    k = rope(k_ref) * (RET_DK ** -0.5)
    v = v_ref[...]
    xi = xz_ref[:, 0:1]
    zeta = xz_ref[:, 1:2]
    gamma_c = xz_ref[c - 1:c, 0:1]

    qk = _dot_nt(q.astype(BF16), k.astype(BF16)) * dec_ref[...]
    state = s_ref[...]
    o = _dot(qk.astype(BF16), v) + _dot((q * xi).astype(BF16), state.astype(BF16))
    s_ref[...] = state * gamma_c + _dot_tn((k * zeta).astype(BF16), v)

    mu = jnp.mean(o, axis=-1, keepdims=True)
    xc = o - mu
    var = jnp.mean(xc * xc, axis=-1, keepdims=True)
    y = xc * lax.rsqrt(var + NORM_EPS) * nw_ref[...]
    o_ref[...] = (_silu(g_ref[...].astype(F32)) * y).astype(o_ref.dtype)


def _retention(main, tab, norm_w, *, batch, seq, c):
    t = main.shape[0]
    nc = seq // c
    hh = RET_HEADS
    log_gamma = jnp.log1p(-jnp.power(2.0, -5.0 - jnp.arange(hh, dtype=F32)))
    lg = jnp.broadcast_to(log_gamma[:, None, None], (hh, 1, LANES))
    row = lambda b, h, s: b * nc + s
    kq = RET_DK
    v_blk0 = (2 * hh * kq) // RET_DV
    return pl.pallas_call(
        functools.partial(_ret_kernel, c=c),
        grid=(batch, hh, nc),
        in_specs=[
            pl.BlockSpec((None, 1, LANES), lambda b, h, s: (h, 0, 0)),
            pl.BlockSpec((c, kq), lambda b, h, s: (row(b, h, s), h)),
            pl.BlockSpec((c, kq), lambda b, h, s: (row(b, h, s), hh + h)),
            pl.BlockSpec((c, RET_DV), lambda b, h, s: (row(b, h, s), v_blk0 + h)),
            pl.BlockSpec((c, RET_DV), lambda b, h, s: (row(b, h, s), v_blk0 + hh + h)),
            pl.BlockSpec((c, kq), lambda b, h, s: (row(b, h, s), 0)),
            pl.BlockSpec((1, RET_DV), lambda b, h, s: (0, h)),
        ],
        out_specs=pl.BlockSpec((c, RET_DV), lambda b, h, s: (row(b, h, s), h)),
        out_shape=jax.ShapeDtypeStruct((t, hh * RET_DV), BF16),
        scratch_shapes=[pltpu.VMEM((kq, RET_DV), F32), pltpu.VMEM((c, c), F32),
                        pltpu.VMEM((c, LANES), F32)],
        compiler_params=_params(3), name="retention",
    )(lg, main, main, main, main, tab, norm_w.reshape(1, -1))


def _rotate_half_cols(w, half):
    return jnp.concatenate([-w[..., half:], w[..., :half]], axis=-1)


def _rope_tab(positions, dim):
    inv_freq = ROPE_THETA ** (-jnp.arange(0, dim, 2, dtype=F32) / dim)
    ang = positions.astype(F32).reshape(-1, 1) * inv_freq
    return jnp.cos(ang), jnp.sin(ang)


def _conv_ffn(h, norm, w_up, conv_w, conv_b, w_down, *, seq):
    act = _ffn_up(h, norm, w_up.astype(BF16), conv_w, conv_b, seq=seq, bm=1024, bn=512)
    return _proj_res([(act, w_down.astype(BF16))], h, bm=1024, bn=512)


def kernel(x, p, positions, l0_attn_norm, l0_w_in, l0_gdn_conv, l0_gdn_A_log, l0_gdn_dt_bias, l0_gdn_norm, l0_mla_q_norm, l0_mla_w_uq, l0_mla_kv_norm, l0_mla_w_ukv, l0_w_out, l0_ffn_norm, l0_ffn_w_up, l0_ffn_conv_w, l0_ffn_conv_b, l0_ffn_w_down, l0_ple_proj, l0_ple_gate_norm, l0_ple_gate, l1_attn_norm, l1_w_in, l1_ret_norm, l1_w_out, l1_ffn_norm, l1_ffn_w_up, l1_ffn_conv_w, l1_ffn_conv_b, l1_ffn_w_down, l1_ple_proj, l1_ple_gate_norm, l1_ple_gate, final_norm):
    batch, seq, d = x.shape
    t = batch * seq
    x2 = x.reshape(t, d)
    p2 = p.reshape(p.shape[0], t, p.shape[-1])

    gq = GDN_HEADS * GDN_D * 4
    n_ab = 2 * GDN_HEADS
    c0 = gq + n_ab
    w_main = jnp.concatenate([l0_w_in[:, :gq], l0_w_in[:, c0:c0 + 2 * MLA_RANK]], axis=1).astype(BF16)
    w_kr = l0_w_in[:, c0 + 2 * MLA_RANK:]
    w_side = jnp.concatenate(
        [w_kr, _rotate_half_cols(w_kr, MLA_ROPE // 2), l0_w_in[:, gq:c0],
         jnp.zeros((d, LANES - n_ab), F32)], axis=1).astype(BF16)

    wq = l0_mla_w_uq.reshape(MLA_RANK, MLA_HEADS, MLA_NOPE + MLA_ROPE)
    wq_pe = wq[:, :, MLA_NOPE:]
    wq = jnp.concatenate([wq[:, :, :MLA_NOPE], wq_pe, _rotate_half_cols(wq_pe, MLA_ROPE // 2)], axis=-1)
    wq = wq.reshape(MLA_RANK, MLA_HEADS * 256).astype(BF16)
    wkv = l0_mla_w_ukv.reshape(MLA_RANK, MLA_HEADS, MLA_NOPE + MLA_V)
    wkv = jnp.concatenate([wkv[:, :, :MLA_NOPE].reshape(MLA_RANK, -1),
                           wkv[:, :, MLA_NOPE:].reshape(MLA_RANK, -1)], axis=1).astype(BF16)

    cos_m, sin_m = _rope_tab(positions, MLA_ROPE)
    tab_m = jnp.concatenate([cos_m, cos_m, sin_m, sin_m], axis=1)
    cos_r, sin_r = _rope_tab(positions, RET_DK)
    tab_r = jnp.concatenate([cos_r, sin_r], axis=1)

    main0, side0 = _norm_proj(x2, l0_attn_norm, w_main, w_side, bm=1024, bn=1024)
    y_a = _gdn(main0, side0, l0_gdn_conv, l0_gdn_A_log, l0_gdn_dt_bias, l0_gdn_norm,
               batch=batch, seq=seq, lb=512)
    qf, kf, vf = _mla_up(main0, side0, tab_m, l0_mla_q_norm, l0_mla_kv_norm, wq, wkv, bm=512)
    y_b = _flash(qf, kf, vf, batch=batch, seq=seq, blk=512)
    w_out0 = l0_w_out.astype(BF16)
    ha = GDN_HEADS * GDN_D
    h = _proj_res([(y_a, w_out0[:ha]), (y_b, w_out0[ha:])], x2, bm=1024, bn=1024)
    h = _conv_ffn(h, l0_ffn_norm, l0_ffn_w_up, l0_ffn_conv_w, l0_ffn_conv_b, l0_ffn_w_down, seq=seq)
    h = _ple(h, p2[0], l0_ple_gate_norm, l0_ple_proj.astype(BF16), l0_ple_gate.astype(BF16),
             bm=512, bn=512)

    main1 = _norm_proj(h, l1_attn_norm, l1_w_in.astype(BF16), bm=1024, bn=1024)
    y = _retention(main1, tab_r, l1_ret_norm, batch=batch, seq=seq, c=256)
    h = _proj_res([(y, l1_w_out.astype(BF16))], h, bm=1024, bn=1024)
    h = _conv_ffn(h, l1_ffn_norm, l1_ffn_w_up, l1_ffn_conv_w, l1_ffn_conv_b, l1_ffn_w_down, seq=seq)
    h = _ple(h, p2[1], l1_ple_gate_norm, l1_ple_proj.astype(BF16), l1_ple_gate.astype(BF16),
             final_norm, bm=512, bn=512)
    return h.reshape(batch, seq, d)
```

```python
import functools

import jax
import jax.numpy as jnp
from jax import lax
from jax.experimental import pallas as pl
from jax.experimental.pallas import tpu as pltpu

F32 = jnp.float32
BF16 = jnp.bfloat16

NORM_EPS = 1e-6
ROPE_THETA = 10000.0

GDN_HEADS = 8
GDN_D = 128
GDN_CONV = 4
GDN_CHUNK = 64
MLA_HEADS = 8
MLA_RANK = 512
MLA_NOPE = 128
MLA_ROPE = 64
MLA_V = 128
RET_HEADS = 8
RET_DK = 256
RET_DV = 512
FFN_CONV = 3

LANES = 128
VMEM_LIMIT = 56 * 1024 * 1024
CONV_HALO = 16


def _params(n_grid):
    return pltpu.CompilerParams(dimension_semantics=("arbitrary",) * n_grid,
                                vmem_limit_bytes=VMEM_LIMIT)


def _rms(x, g):
    return x * lax.rsqrt(jnp.mean(x * x, axis=-1, keepdims=True) + NORM_EPS) * g


def _dot(a, b):
    return jnp.dot(a, b, preferred_element_type=F32)


def _dot_nt(a, b):
    return lax.dot_general(a, b, (((1,), (1,)), ((), ())), preferred_element_type=F32)


def _dot_tn(a, b):
    return lax.dot_general(a, b, (((0,), (0,)), ((), ())), preferred_element_type=F32)


def _silu(x):
    return x * jax.nn.sigmoid(x)


def _in_proj0_kernel(x_ref, xh_ref, g_ref, w_ref, ws_ref, cw_ref, o_ref, os_ref, xn_ref, u_ref,
                     *, bm, blocks_per_seq, n_conv_blocks):
    i = pl.program_id(0)
    j = pl.program_id(1)

    @pl.when(j == 0)
    def _():
        g = g_ref[...]
        halo = _rms(xh_ref[...], g)
        halo = jnp.where(i % blocks_per_seq == 0, 0.0, halo)
        xn_ref[0:CONV_HALO, :] = halo.astype(BF16)
        xn = _rms(x_ref[...], g).astype(BF16)
        xn_ref[CONV_HALO:, :] = xn
        os_ref[...] = _dot(xn, ws_ref[...])

    def conv_silu():
        u_ref[...] = _dot(xn_ref[...], w_ref[...])
        cw = cw_ref[...]
        y = cw[GDN_CONV - 1:GDN_CONV] * u_ref[pl.ds(CONV_HALO, bm), :]
        for s in range(1, GDN_CONV):
            y = y + cw[GDN_CONV - 1 - s:GDN_CONV - s] * u_ref[pl.ds(CONV_HALO - s, bm), :]
        return _silu(y)

    @pl.when(j < n_conv_blocks - 1)
    def _():
        y = conv_silu()
        scale = jnp.where(j == 0, GDN_D ** -0.5, 1.0)
        for h in range(y.shape[1] // GDN_D):
            cols = slice(h * GDN_D, (h + 1) * GDN_D)
            yh = y[:, cols]
            inv = lax.rsqrt(jnp.sum(yh * yh, axis=-1, keepdims=True) + NORM_EPS) * scale
            o_ref[:, cols] = (yh * inv).astype(o_ref.dtype)

    @pl.when(j == n_conv_blocks - 1)
    def _():
        o_ref[...] = conv_silu().astype(o_ref.dtype)

    @pl.when(j >= n_conv_blocks)
    def _():
        o_ref[...] = _dot(xn_ref[CONV_HALO:, :], w_ref[...]).astype(o_ref.dtype)


def _in_proj0(x, g, w, w_side, conv_w, *, seq, bm, bn):
    t, d = x.shape
    n = w.shape[1]
    ns = w_side.shape[1]
    n_conv_blocks = conv_w.shape[1] // bn
    hb = bm // CONV_HALO
    kern = functools.partial(_in_proj0_kernel, bm=bm, blocks_per_seq=seq // bm, n_conv_blocks=n_conv_blocks)
    return pl.pallas_call(
        kern,
        grid=(t // bm, n // bn),
        in_specs=[
            pl.BlockSpec((bm, d), lambda i, j: (i, 0)),
            pl.BlockSpec((CONV_HALO, d), lambda i, j: (jnp.maximum(i * hb - 1, 0), 0)),
            pl.BlockSpec((1, d), lambda i, j: (0, 0)),
            pl.BlockSpec((d, bn), lambda i, j: (0, j)),
            pl.BlockSpec((d, ns), lambda i, j: (0, 0)),
            pl.BlockSpec((GDN_CONV, bn), lambda i, j: (0, jnp.minimum(j, n_conv_blocks - 1))),
        ],
        out_specs=[pl.BlockSpec((bm, bn), lambda i, j: (i, j)),
                   pl.BlockSpec((bm, ns), lambda i, j: (i, 0))],
        out_shape=[jax.ShapeDtypeStruct((t, n), BF16), jax.ShapeDtypeStruct((t, ns), F32)],
        scratch_shapes=[pltpu.VMEM((bm + CONV_HALO, d), BF16), pltpu.VMEM((bm + CONV_HALO, bn), F32)],
        compiler_params=_params(2), name="in_proj0",
    )(x, x, g.reshape(1, d), w, w_side, conv_w)


def _in_proj1_kernel(x_ref, g_ref, w_ref, cs_ref, o_ref, xn_ref, *, n_q, n_k, n_v):
    j = pl.program_id(1)
    half = RET_DK // 2

    @pl.when(j == 0)
    def _():
        xn_ref[...] = _rms(x_ref[...], g_ref[...]).astype(BF16)

    def proj():
        return _dot(xn_ref[...], w_ref[...])

    def rope_store(scale):
        y = proj()
        cos = cs_ref[:, :half]
        sin = cs_ref[:, half:]
        for h in range(y.shape[1] // RET_DK):
            x1 = y[:, h * RET_DK:h * RET_DK + half]
            x2 = y[:, h * RET_DK + half:(h + 1) * RET_DK]
            o_ref[:, h * RET_DK:h * RET_DK + half] = ((x1 * cos - x2 * sin) * scale).astype(o_ref.dtype)
            o_ref[:, h * RET_DK + half:(h + 1) * RET_DK] = ((x2 * cos + x1 * sin) * scale).astype(o_ref.dtype)

    @pl.when(j < n_q)
    def _():
        rope_store(1.0)

    @pl.when((j >= n_q) & (j < n_q + n_k))
    def _():
        rope_store(RET_DK ** -0.5)

    @pl.when((j >= n_q + n_k) & (j < n_q + n_k + n_v))
    def _():
        o_ref[...] = proj().astype(o_ref.dtype)

    @pl.when(j >= n_q + n_k + n_v)
    def _():
        o_ref[...] = _silu(proj()).astype(o_ref.dtype)


def _in_proj1(x, g, w, tab, *, bm, bn):
    t, d = x.shape
    n = w.shape[1]
    n_q = RET_HEADS * RET_DK // bn
    n_v = RET_HEADS * RET_DV // bn
    return pl.pallas_call(
        functools.partial(_in_proj1_kernel, n_q=n_q, n_k=n_q, n_v=n_v),
        grid=(t // bm, n // bn),
        in_specs=[pl.BlockSpec((bm, d), lambda i, j: (i, 0)),
                  pl.BlockSpec((1, d), lambda i, j: (0, 0)),
                  pl.BlockSpec((d, bn), lambda i, j: (0, j)),
                  pl.BlockSpec((bm, RET_DK), lambda i, j: (i, 0))],
        out_specs=pl.BlockSpec((bm, bn), lambda i, j: (i, j)),
        out_shape=jax.ShapeDtypeStruct((t, n), BF16),
        scratch_shapes=[pltpu.VMEM((bm, d), BF16)],
        compiler_params=_params(2), name="in_proj1",
    )(x, g.reshape(1, d), w, tab)


def _proj_res_kernel(*refs, n_pairs):
    res_ref, o_ref = refs[2 * n_pairs], refs[2 * n_pairs + 1]
    acc = res_ref[...]
    for p in range(n_pairs):
        acc = acc + _dot(refs[2 * p][...], refs[2 * p + 1][...])
    o_ref[...] = acc


def _proj_res(pairs, res, *, bm, bn):
    t, n = res.shape
    in_specs, args = [], []
    for y, w in pairs:
        kk = y.shape[1]
        in_specs += [pl.BlockSpec((bm, kk), lambda i, j: (i, 0)),
                     pl.BlockSpec((kk, bn), lambda i, j: (0, j))]
        args += [y, w]
    in_specs.append(pl.BlockSpec((bm, bn), lambda i, j: (i, j)))
    args.append(res)
    return pl.pallas_call(
        functools.partial(_proj_res_kernel, n_pairs=len(pairs)),
        grid=(t // bm, n // bn),
        in_specs=in_specs,
        out_specs=pl.BlockSpec((bm, bn), lambda i, j: (i, j)),
        out_shape=jax.ShapeDtypeStruct((t, n), F32),
        compiler_params=_params(2), name="proj_res",
    )(*args)


def _ffn_up_kernel(h_ref, hh_ref, g_ref, wg_ref, wu_ref, cg_ref, cu_ref, bg_ref, bu_ref,
                   o_ref, xn_ref, ug_ref, uu_ref, *, bm, blocks_per_seq):
    i = pl.program_id(0)

    @pl.when(pl.program_id(1) == 0)
    def _():
        g = g_ref[...]
        halo = _rms(hh_ref[...], g)
        halo = jnp.where(i % blocks_per_seq == 0, 0.0, halo)
        xn_ref[0:CONV_HALO, :] = halo.astype(BF16)
        xn_ref[CONV_HALO:, :] = _rms(h_ref[...], g).astype(BF16)

    xn = xn_ref[...]
    ug_ref[...] = _dot(xn, wg_ref[...])
    uu_ref[...] = _dot(xn, wu_ref[...])

    def conv(u_ref, c_ref, b_ref):
        cw = c_ref[...]
        y = b_ref[...] + cw[FFN_CONV - 1:FFN_CONV] * u_ref[pl.ds(CONV_HALO, bm), :]
        for s in range(1, FFN_CONV):
            y = y + cw[FFN_CONV - 1 - s:FFN_CONV - s] * u_ref[pl.ds(CONV_HALO - s, bm), :]
        return y

    gate = conv(ug_ref, cg_ref, bg_ref)
    up = conv(uu_ref, cu_ref, bu_ref)
    o_ref[...] = (_silu(gate) * up).astype(o_ref.dtype)


def _ffn_up(h, g, w_up, conv_w, conv_b, *, seq, bm, bn):
    t, d = h.shape
    dff = w_up.shape[1] // 2
    nj = dff // bn
    hb = bm // CONV_HALO
    kern = functools.partial(_ffn_up_kernel, bm=bm, blocks_per_seq=seq // bm)
    return pl.pallas_call(
        kern,
        grid=(t // bm, nj),
        in_specs=[
            pl.BlockSpec((bm, d), lambda i, j: (i, 0)),
            pl.BlockSpec((CONV_HALO, d), lambda i, j: (jnp.maximum(i * hb - 1, 0), 0)),
            pl.BlockSpec((1, d), lambda i, j: (0, 0)),
            pl.BlockSpec((d, bn), lambda i, j: (0, j)),
            pl.BlockSpec((d, bn), lambda i, j: (0, j + nj)),
            pl.BlockSpec((FFN_CONV, bn), lambda i, j: (0, j)),
            pl.BlockSpec((FFN_CONV, bn), lambda i, j: (0, j + nj)),
            pl.BlockSpec((1, bn), lambda i, j: (0, j)),
            pl.BlockSpec((1, bn), lambda i, j: (0, j + nj)),
        ],
        out_specs=pl.BlockSpec((bm, bn), lambda i, j: (i, j)),
        out_shape=jax.ShapeDtypeStruct((t, dff), BF16),
        scratch_shapes=[pltpu.VMEM((bm + CONV_HALO, d), BF16),
                        pltpu.VMEM((bm + CONV_HALO, bn), F32),
                        pltpu.VMEM((bm + CONV_HALO, bn), F32)],
        compiler_params=_params(2), name="ffn_up",
    )(h, h, g.reshape(1, d), w_up, w_up, conv_w, conv_w,
      conv_b.reshape(1, -1), conv_b.reshape(1, -1))


def _ple_kernel(h_ref, p_ref, gn_ref, wp_ref, wg_ref, *rest, bn, final):
    if final:
        fn_ref, o_ref = rest
    else:
        (o_ref,) = rest
    d = h_ref.shape[1]
    hn = _rms(h_ref[...], gn_ref[...]).astype(BF16)
    pb = p_ref[...].astype(BF16)
    ssq = None
    for c in range(d // bn):
        cols = slice(c * bn, (c + 1) * bn)
        gate = _dot(hn, wg_ref[:, cols])
        proj = _dot(pb, wp_ref[:, cols])
        out = h_ref[:, cols] + proj * jax.nn.sigmoid(gate)
        o_ref[:, cols] = out
        if final:
            part = jnp.sum(out * out, axis=-1, keepdims=True)
            ssq = part if ssq is None else ssq + part
    if final:
        scale = lax.rsqrt(ssq / d + NORM_EPS)
        o_ref[...] = o_ref[...] * scale * fn_ref[...]


def _ple(h, p, gate_norm, w_proj, w_gate, final_norm=None, *, bm, bn):
    t, d = h.shape
    dp = p.shape[1]
    final = final_norm is not None
    in_specs = [pl.BlockSpec((bm, d), lambda i: (i, 0)),
                pl.BlockSpec((bm, dp), lambda i: (i, 0)),
                pl.BlockSpec((1, d), lambda i: (0, 0)),
                pl.BlockSpec((dp, d), lambda i: (0, 0)),
                pl.BlockSpec((d, d), lambda i: (0, 0))]
    args = [h, p, gate_norm.reshape(1, d), w_proj, w_gate]
    if final:
        in_specs.append(pl.BlockSpec((1, d), lambda i: (0, 0)))
        args.append(final_norm.reshape(1, d))
    return pl.pallas_call(
        functools.partial(_ple_kernel, bn=bn, final=final),
        grid=(t // bm,),
        in_specs=in_specs,
        out_specs=pl.BlockSpec((bm, d), lambda i: (i, 0)),
        out_shape=jax.ShapeDtypeStruct((t, d), F32),
        compiler_params=_params(1), name="ple",
    )(*args)


def _split3(x):
    hi = x.astype(BF16).astype(F32)
    r1 = x - hi
    mid = r1.astype(BF16).astype(F32)
    lo = (r1 - mid).astype(BF16).astype(F32)
    return hi, mid, lo


def _lane_pick(x, lane, idx):
    return jnp.sum(jnp.where(lane == idx, x, 0.0), axis=-1, keepdims=True)


def _gdn_kernel(q_ref, k_ref, v_ref, z_ref, ab_ref, hp_ref, nw_ref, tril_ref, o_ref, s_ref, *, lb, nh):
    c = GDN_CHUNK
    n_chunks = lb // c
    dk = GDN_D

    @pl.when(pl.program_id(2) == 0)
    def _():
        s_ref[...] = jnp.zeros_like(s_ref)

    ab = ab_ref[...]
    lane = lax.broadcasted_iota(jnp.int32, ab.shape, 1)
    hp = hp_ref[...]
    xa = ab + hp[1:2]
    softplus = jnp.maximum(xa, 0.0) + jnp.log1p(jnp.exp(-jnp.abs(xa)))
    g_lanes = -jnp.exp(hp[0:1]) * softplus
    beta_lanes = jax.nn.sigmoid(ab)

    def pieces_to_lanes(x, first, sign):
        hi, mid, lo = _split3(x)
        out = jnp.where(lane == first, sign * hi, 0.0)
        out = jnp.where(lane == first + 1, sign * mid, out)
        return jnp.where(lane == first + 2, sign * lo, out)

    row2 = lax.broadcasted_iota(jnp.int32, (c, 2 * c), 0)
    lane2 = lax.broadcasted_iota(jnp.int32, (c, 2 * c), 1)
    col2 = lane2 & (c - 1)
    incl2 = row2 >= col2
    strict2 = row2 > col2
    left = lane2 < c
    eye_left = jnp.where(left & (row2 == lane2), 1.0, 0.0)
    zeros_k = jnp.zeros((c, 2 * c), BF16)
    zeros_x = jnp.zeros((c, 2 * dk), BF16)

    items = []
    for hi in range(nh):
        head = pl.program_id(1) * nh + hi
        cols = slice(hi * dk, (hi + 1) * dk)
        qb = q_ref[:, cols]
        kb16 = k_ref[:, cols]
        q = qb.astype(F32)
        k = kb16.astype(F32)
        v = v_ref[:, cols].astype(F32)
        g_col = _lane_pick(g_lanes, lane, head)
        beta = _lane_pick(beta_lanes, lane, head + GDN_HEADS)
        gm = _dot(tril_ref[...], pieces_to_lanes(g_col, 0, 1.0).astype(BF16))
        big_g = jnp.sum(gm, axis=-1, keepdims=True)
        e_g = jnp.exp(big_g)
        am = jnp.where(lane < 3, pieces_to_lanes(big_g, 0, 1.0), jnp.where(lane < 6, 1.0, 0.0)).astype(BF16)
        bmat = jnp.where(lane < 3, 1.0, pieces_to_lanes(big_g, 3, -1.0)).astype(BF16)
        kbeta = k * beta
        x0 = jnp.concatenate([v * beta, kbeta * e_g], axis=-1).astype(BF16)
        qd = q * e_g
        kbeta16 = kbeta.astype(BF16)
        for n in range(n_chunks):
            r = slice(n * c, (n + 1) * c)
            g_last = big_g[(n + 1) * c - 1:(n + 1) * c, :]
            items.append(dict(
                am=am[r], bm2=jnp.concatenate([bmat[r], bmat[r]], axis=0),
                kb=kbeta16[r], k=kb16[r], k2=jnp.concatenate([kb16[r], kb16[r]], axis=0), q=qb[r],
                x0=jnp.concatenate([x0[r], zeros_x], axis=0), qd=qd[r],
                kd=(k[r] * jnp.exp(g_last - big_g[r])).astype(BF16), gl=jnp.exp(g_last)))

    for it in items:
        it["gamma2"] = jnp.exp(jnp.where(incl2, _dot_nt(it["am"], it["bm2"]), -jnp.inf))
    for it in items:
        it["tp"] = jnp.where(left, eye_left, jnp.where(strict2, -_dot_nt(it["kb"], it["k2"]) * it["gamma2"], 0.0))
    for it in items:
        it["attn"] = (_dot_nt(it["q"], it["k"]) * it["gamma2"][:, :c]).astype(BF16)

    for j in range(6):
        for it in items:
            it["tpb"] = it["tp"].astype(BF16)
        for it in items:
            upd = _dot(it["tpb"], jnp.concatenate([zeros_k, it["tpb"]], axis=0))
            it["tp"] = jnp.where(left, it["tp"] + upd, upd)
    for it in items:
        it["tpb"] = it["tp"].astype(BF16)
    for it in items:
        it["xb"] = _dot(it["tpb"], it["x0"]).astype(BF16)

    for it in items:
        it["au"] = _dot(it["attn"], it["xb"])
    for it in items:
        it["ku"] = _dot_tn(it["kd"], it["xb"])
    for it in items:
        it["lhs"] = jnp.concatenate([it["qd"] - it["au"][:, dk:], it["ku"][:, dk:]], axis=0).astype(BF16)

    states = [s_ref[hi] for hi in range(nh)]
    outs = [[None] * n_chunks for _ in range(nh)]
    for n in range(n_chunks):
        for hi in range(nh):
            it = items[hi * n_chunks + n]
            prod = _dot(it["lhs"], states[hi].astype(BF16))
            outs[hi][n] = prod[:c] + it["au"][:, :dk]
            states[hi] = states[hi] * it["gl"] - prod[c:] + it["ku"][:, :dk]
    for hi in range(nh):
        s_ref[hi] = states[hi]

    nw = nw_ref[...]
    for hi in range(nh):
        cols = slice(hi * dk, (hi + 1) * dk)
        o = jnp.concatenate(outs[hi], axis=0)
        o_ref[:, cols] = (_rms(o, nw) * _silu(z_ref[:, cols].astype(F32))).astype(o_ref.dtype)


def _gdn(main, side, a_log, dt_bias, norm_w, *, batch, seq, lb, nh):
    t = main.shape[0]
    ns = seq // lb
    hh = GDN_HEADS
    ng = hh // nh
    w = nh * GDN_D
    hp = jnp.zeros((8, LANES), F32)
    hp = hp.at[0, :hh].set(a_log).at[1, :hh].set(dt_bias)
    r = jnp.arange(lb)
    tril = ((r[:, None] >= r[None, :]) & (r[:, None] // GDN_CHUNK == r[None, :] // GDN_CHUNK)).astype(BF16)
    row = lambda b, h, s: b * ns + s
    return pl.pallas_call(
        functools.partial(_gdn_kernel, lb=lb, nh=nh),
        grid=(batch, ng, ns),
        in_specs=[
            pl.BlockSpec((lb, w), lambda b, h, s: (row(b, h, s), h)),
            pl.BlockSpec((lb, w), lambda b, h, s: (row(b, h, s), ng + h)),
            pl.BlockSpec((lb, w), lambda b, h, s: (row(b, h, s), 2 * ng + h)),
            pl.BlockSpec((lb, w), lambda b, h, s: (row(b, h, s), 3 * ng + h)),
            pl.BlockSpec((lb, LANES), lambda b, h, s: (row(b, h, s), 1)),
            pl.BlockSpec((8, LANES), lambda b, h, s: (0, 0)),
            pl.BlockSpec((1, GDN_D), lambda b, h, s: (0, 0)),
            pl.BlockSpec((lb, lb), lambda b, h, s: (0, 0)),
        ],
        out_specs=pl.BlockSpec((lb, w), lambda b, h, s: (row(b, h, s), h)),
        out_shape=jax.ShapeDtypeStruct((t, hh * GDN_D), BF16),
        scratch_shapes=[pltpu.VMEM((nh, GDN_D, GDN_D), F32)],
        compiler_params=_params(3), name="gdn",
    )(main, main, main, main, side, hp, norm_w.reshape(1, GDN_D), tril)


def _mla_up_kernel(cq_ref, ckv_ref, side_ref, tab_ref, qn_ref, kvn_ref, wq_ref, wkv_ref,
                   q_out, k_out, v_out):
    hh = MLA_HEADS
    tab = tab_ref[...]
    scale = (MLA_NOPE + MLA_ROPE) ** -0.5

    cq = _rms(cq_ref[...].astype(F32), qn_ref[...]).astype(BF16)
    for h in range(hh):
        qh = _dot(cq, wq_ref[:, h * 256:(h + 1) * 256])
        q_out[:, h * 256:h * 256 + LANES] = (qh[:, :LANES] * scale).astype(BF16)
        q_out[:, h * 256 + LANES:(h + 1) * 256] = (qh[:, LANES:] * tab * scale).astype(BF16)

    kr = side_ref[...] * tab
    kpe = (kr + pltpu.roll(kr, shift=MLA_ROPE, axis=1)).astype(BF16)

    ckv = _rms(ckv_ref[...].astype(F32), kvn_ref[...]).astype(BF16)
    for h in range(hh):
        k_out[:, h * 256:h * 256 + LANES] = _dot(ckv, wkv_ref[:, h * LANES:(h + 1) * LANES]).astype(BF16)
        k_out[:, h * 256 + LANES:(h + 1) * 256] = kpe
    v_out[...] = _dot(ckv, wkv_ref[:, hh * LANES:]).astype(BF16)


def _mla_up(main, side, tab, q_norm, kv_norm, wq, wkv, *, bm):
    t = main.shape[0]
    r = MLA_RANK
    cq_blk = (GDN_HEADS * GDN_D * 4) // r
    return pl.pallas_call(
        _mla_up_kernel,
        grid=(t // bm,),
        in_specs=[
            pl.BlockSpec((bm, r), lambda i: (i, cq_blk)),
            pl.BlockSpec((bm, r), lambda i: (i, cq_blk + 1)),
            pl.BlockSpec((bm, LANES), lambda i: (i, 0)),
            pl.BlockSpec((bm, LANES), lambda i: (i, 0)),
            pl.BlockSpec((1, r), lambda i: (0, 0)),
            pl.BlockSpec((1, r), lambda i: (0, 0)),
            pl.BlockSpec(wq.shape, lambda i: (0, 0)),
            pl.BlockSpec(wkv.shape, lambda i: (0, 0)),
        ],
        out_specs=[pl.BlockSpec((bm, MLA_HEADS * 256), lambda i: (i, 0)),
                   pl.BlockSpec((bm, MLA_HEADS * 256), lambda i: (i, 0)),
                   pl.BlockSpec((bm, MLA_HEADS * MLA_V), lambda i: (i, 0))],
        out_shape=[jax.ShapeDtypeStruct((t, MLA_HEADS * 256), BF16),
                   jax.ShapeDtypeStruct((t, MLA_HEADS * 256), BF16),
                   jax.ShapeDtypeStruct((t, MLA_HEADS * MLA_V), BF16)],
        compiler_params=_params(1), name="mla_up",
    )(main, main, side, tab, q_norm.reshape(1, r), kv_norm.reshape(1, r), wq, wkv)


def _flash_kernel(q_ref, k_ref, v_ref, o_ref, *, blk, nh):
    qi = pl.program_id(2)
    dq = 2 * LANES
    qs = [q_ref[:, h * dq:(h + 1) * dq] for h in range(nh)]

    def step(j, carry, masked):
        start = pl.multiple_of(j * blk, blk)
        ss = [_dot_nt(qs[h], k_ref[pl.ds(start, blk), h * dq:(h + 1) * dq]) for h in range(nh)]
        if masked:
            row = lax.broadcasted_iota(jnp.int32, ss[0].shape, 0)
            col = lax.broadcasted_iota(jnp.int32, ss[0].shape, 1)
            ss = [jnp.where(col <= row, s, -jnp.inf) for s in ss]
        m_new = [jnp.maximum(carry[h][0], jnp.max(ss[h], axis=-1, keepdims=True)) for h in range(nh)]
        ps = [jnp.exp(ss[h] - m_new[h]) for h in range(nh)]
        pv = [_dot(ps[h].astype(BF16), v_ref[pl.ds(start, blk), h * MLA_V:(h + 1) * MLA_V]) for h in range(nh)]
        out = []
        for h in range(nh):
            m, l, acc = carry[h]
            alpha = jnp.exp(m - m_new[h])
            out.append((m_new[h], l * alpha + jnp.sum(ps[h], axis=-1, keepdims=True), acc * alpha + pv[h]))
        return tuple(out)

    init = tuple((jnp.full((blk, 1), -jnp.inf, F32), jnp.zeros((blk, 1), F32),
                  jnp.zeros((blk, MLA_V), F32)) for _ in range(nh))
    carry = lax.fori_loop(0, qi, lambda j, cr: step(j, cr, False), init)
    carry = step(qi, carry, True)
    for h in range(nh):
        _, l, acc = carry[h]
        o_ref[:, h * MLA_V:(h + 1) * MLA_V] = (acc / l).astype(o_ref.dtype)


def _flash(qf, kf, vf, *, batch, seq, blk, nh):
    t = qf.shape[0]
    nq = seq // blk
    hh = MLA_HEADS
    kf3 = kf.reshape(batch, seq, hh * 256)
    vf3 = vf.reshape(batch, seq, hh * MLA_V)
    return pl.pallas_call(
        functools.partial(_flash_kernel, blk=blk, nh=nh),
        grid=(batch, hh // nh, nq),
        in_specs=[
            pl.BlockSpec((blk, nh * 256), lambda b, h, i: (b * nq + i, h)),
            pl.BlockSpec((None, seq, nh * 256), lambda b, h, i: (b, 0, h)),
            pl.BlockSpec((None, seq, nh * MLA_V), lambda b, h, i: (b, 0, h)),
        ],
        out_specs=pl.BlockSpec((blk, nh * MLA_V), lambda b, h, i: (b * nq + i, h)),
        out_shape=jax.ShapeDtypeStruct((t, hh * MLA_V), BF16),
        compiler_params=_params(3), name="mla_flash",
    )(qf, kf3, vf3)


def _ret_kernel(lg_ref, q_ref, k_ref, v_ref, g_ref, nw_ref, o_ref,
                s_ref, dec_ref, xz_ref, *, c, nh):
    @pl.when(pl.program_id(2) == 0)
    def _():
        row = lax.broadcasted_iota(jnp.int32, (c, c), 0)
        col = lax.broadcasted_iota(jnp.int32, (c, c), 1)
        pos = lax.broadcasted_iota(jnp.int32, (c, LANES), 0).astype(F32)
        lane = lax.broadcasted_iota(jnp.int32, (c, LANES), 1)
        for h in range(nh):
            lg = lg_ref[h, :, 0:1]
            dec_ref[h] = jnp.exp(jnp.where(row >= col, (row - col).astype(F32) * lg, -jnp.inf))
            xi = jnp.exp((pos + 1.0) * lg)
            zeta = jnp.exp((c - 1.0 - pos) * lg)
            xz_ref[h] = jnp.where(lane == 0, xi, zeta)
        s_ref[...] = jnp.zeros_like(s_ref)

    hs = range(nh)
    qb = [q_ref[:, h * RET_DK:(h + 1) * RET_DK] for h in hs]
    kb = [k_ref[:, h * RET_DK:(h + 1) * RET_DK] for h in hs]
    v = [v_ref[:, h * RET_DV:(h + 1) * RET_DV] for h in hs]
    qk = [(_dot_nt(qb[h], kb[h]) * dec_ref[h]).astype(BF16) for h in hs]
    states = [s_ref[h] for h in hs]
    cross = [_dot((qb[h].astype(F32) * xz_ref[h, :, 0:1]).astype(BF16), states[h].astype(BF16)) for h in hs]
    inner = [_dot(qk[h], v[h]) for h in hs]
    upd = [_dot_tn((kb[h].astype(F32) * xz_ref[h, :, 1:2]).astype(BF16), v[h]) for h in hs]
    for h in hs:
        s_ref[h] = states[h] * xz_ref[h, c - 1:c, 0:1] + upd[h]
    for h in hs:
        o = inner[h] + cross[h]
        mu = jnp.mean(o, axis=-1, keepdims=True)
        xc = o - mu
        var = jnp.mean(xc * xc, axis=-1, keepdims=True)
        cols = slice(h * RET_DV, (h + 1) * RET_DV)
        y = xc * lax.rsqrt(var + NORM_EPS) * nw_ref[:, cols]
        o_ref[:, cols] = (g_ref[:, cols].astype(F32) * y).astype(o_ref.dtype)


def _retention(main, norm_w, *, batch, seq, c, nh):
    t = main.shape[0]
    nc = seq // c
    hh = RET_HEADS
    ng = hh // nh
    log_gamma = jnp.log1p(-jnp.power(2.0, -5.0 - jnp.arange(hh, dtype=F32)))
    lg = jnp.broadcast_to(log_gamma[:, None, None], (hh, 1, LANES))
    row = lambda b, h, s: b * nc + s
    kq = RET_DK
    v_blk0 = (2 * hh * kq) // (nh * RET_DV)
    return pl.pallas_call(
        functools.partial(_ret_kernel, c=c, nh=nh),
        grid=(batch, ng, nc),
        in_specs=[
            pl.BlockSpec((nh, 1, LANES), lambda b, h, s: (h, 0, 0)),
            pl.BlockSpec((c, nh * kq), lambda b, h, s: (row(b, h, s), h)),
            pl.BlockSpec((c, nh * kq), lambda b, h, s: (row(b, h, s), ng + h)),
            pl.BlockSpec((c, nh * RET_DV), lambda b, h, s: (row(b, h, s), v_blk0 + h)),
            pl.BlockSpec((c, nh * RET_DV), lambda b, h, s: (row(b, h, s), v_blk0 + ng + h)),
            pl.BlockSpec((1, nh * RET_DV), lambda b, h, s: (0, h)),
        ],
        out_specs=pl.BlockSpec((c, nh * RET_DV), lambda b, h, s: (row(b, h, s), h)),
        out_shape=jax.ShapeDtypeStruct((t, hh * RET_DV), BF16),
        scratch_shapes=[pltpu.VMEM((nh, kq, RET_DV), F32), pltpu.VMEM((nh, c, c), F32),
                        pltpu.VMEM((nh, c, LANES), F32)],
        compiler_params=_params(3), name="retention",
    )(lg, main, main, main, main, norm_w.reshape(1, -1))


def _rotate_half_cols(w, half):
    return jnp.concatenate([-w[..., half:], w[..., :half]], axis=-1)


def _rope_tab(positions, dim):
    inv_freq = ROPE_THETA ** (-jnp.arange(0, dim, 2, dtype=F32) / dim)
    ang = positions.astype(F32).reshape(-1, 1) * inv_freq
    return jnp.cos(ang), jnp.sin(ang)


def _conv_ffn(h, norm, w_up, conv_w, conv_b, w_down, *, seq):
    act = _ffn_up(h, norm, w_up.astype(BF16), conv_w, conv_b, seq=seq, bm=1024, bn=512)
    return _proj_res([(act, w_down.astype(BF16))], h, bm=1024, bn=512)


def kernel(x, p, positions, l0_attn_norm, l0_w_in, l0_gdn_conv, l0_gdn_A_log, l0_gdn_dt_bias, l0_gdn_norm, l0_mla_q_norm, l0_mla_w_uq, l0_mla_kv_norm, l0_mla_w_ukv, l0_w_out, l0_ffn_norm, l0_ffn_w_up, l0_ffn_conv_w, l0_ffn_conv_b, l0_ffn_w_down, l0_ple_proj, l0_ple_gate_norm, l0_ple_gate, l1_attn_norm, l1_w_in, l1_ret_norm, l1_w_out, l1_ffn_norm, l1_ffn_w_up, l1_ffn_conv_w, l1_ffn_conv_b, l1_ffn_w_down, l1_ple_proj, l1_ple_gate_norm, l1_ple_gate, final_norm):
    batch, seq, d = x.shape
    t = batch * seq
    x2 = x.reshape(t, d)
    p2 = p.reshape(p.shape[0], t, p.shape[-1])

    gq = GDN_HEADS * GDN_D * 4
    n_ab = 2 * GDN_HEADS
    c0 = gq + n_ab
    w_main = jnp.concatenate([l0_w_in[:, :gq], l0_w_in[:, c0:c0 + 2 * MLA_RANK]], axis=1).astype(BF16)
    w_kr = l0_w_in[:, c0 + 2 * MLA_RANK:]
    w_side = jnp.concatenate(
        [w_kr, _rotate_half_cols(w_kr, MLA_ROPE // 2), l0_w_in[:, gq:c0],
         jnp.zeros((d, LANES - n_ab), F32)], axis=1).astype(BF16)

    wq = l0_mla_w_uq.reshape(MLA_RANK, MLA_HEADS, MLA_NOPE + MLA_ROPE)
    wq_pe = wq[:, :, MLA_NOPE:]
    wq = jnp.concatenate([wq[:, :, :MLA_NOPE], wq_pe, _rotate_half_cols(wq_pe, MLA_ROPE // 2)], axis=-1)
    wq = wq.reshape(MLA_RANK, MLA_HEADS * 256).astype(BF16)
    wkv = l0_mla_w_ukv.reshape(MLA_RANK, MLA_HEADS, MLA_NOPE + MLA_V)
    wkv = jnp.concatenate([wkv[:, :, :MLA_NOPE].reshape(MLA_RANK, -1),
                           wkv[:, :, MLA_NOPE:].reshape(MLA_RANK, -1)], axis=1).astype(BF16)

    cos_m, sin_m = _rope_tab(positions, MLA_ROPE)
    tab_m = jnp.concatenate([cos_m, cos_m, sin_m, sin_m], axis=1)
    cos_r, sin_r = _rope_tab(positions, RET_DK)
    tab_r = jnp.concatenate([cos_r, sin_r], axis=1)

    main0, side0 = _in_proj0(x2, l0_attn_norm, w_main, w_side, l0_gdn_conv, seq=seq, bm=1024, bn=1024)
    y_a = _gdn(main0, side0, l0_gdn_A_log, l0_gdn_dt_bias, l0_gdn_norm,
               batch=batch, seq=seq, lb=256, nh=8)
    qf, kf, vf = _mla_up(main0, side0, tab_m, l0_mla_q_norm, l0_mla_kv_norm, wq, wkv, bm=512)
    y_b = _flash(qf, kf, vf, batch=batch, seq=seq, blk=512, nh=2)
    w_out0 = l0_w_out.astype(BF16)
    ha = GDN_HEADS * GDN_D
    h = _proj_res([(y_a, w_out0[:ha]), (y_b, w_out0[ha:])], x2, bm=1024, bn=1024)
    h = _conv_ffn(h, l0_ffn_norm, l0_ffn_w_up, l0_ffn_conv_w, l0_ffn_conv_b, l0_ffn_w_down, seq=seq)
    h = _ple(h, p2[0], l0_ple_gate_norm, l0_ple_proj.astype(BF16), l0_ple_gate.astype(BF16),
             bm=512, bn=512)

    main1 = _in_proj1(h, l1_attn_norm, l1_w_in.astype(BF16), tab_r, bm=1024, bn=1024)
    y = _retention(main1, l1_ret_norm, batch=batch, seq=seq, c=256, nh=4)
    h = _proj_res([(y, l1_w_out.astype(BF16))], h, bm=1024, bn=1024)
    h = _conv_ffn(h, l1_ffn_norm, l1_ffn_w_up, l1_ffn_conv_w, l1_ffn_conv_b, l1_ffn_w_down, seq=seq)
    h = _ple(h, p2[1], l1_ple_gate_norm, l1_ple_proj.astype(BF16), l1_ple_gate.astype(BF16),
             final_norm, bm=512, bn=512)
    return h.reshape(batch, seq, d)
```

```python
import functools

import jax
import jax.numpy as jnp
from jax import lax
from jax.experimental import pallas as pl
from jax.experimental.pallas import tpu as pltpu

F32 = jnp.float32
BF16 = jnp.bfloat16

NORM_EPS = 1e-6
ROPE_THETA = 10000.0

GDN_HEADS = 8
GDN_D = 128
GDN_CONV = 4
GDN_CHUNK = 64
MLA_HEADS = 8
MLA_RANK = 512
MLA_NOPE = 128
MLA_ROPE = 64
MLA_V = 128
RET_HEADS = 8
RET_DK = 256
RET_DV = 512
FFN_CONV = 3

LANES = 128
VMEM_LIMIT = 56 * 1024 * 1024
LOG2_E = 1.4426950408889634
CONV_HALO = 16


def _params(n_grid):
    return pltpu.CompilerParams(dimension_semantics=("arbitrary",) * n_grid,
                                vmem_limit_bytes=VMEM_LIMIT)


def _rms(x, g):
    return x * lax.rsqrt(jnp.mean(x * x, axis=-1, keepdims=True) + NORM_EPS) * g


def _dot(a, b):
    return jnp.dot(a, b, preferred_element_type=F32)


def _dot_nt(a, b):
    return lax.dot_general(a, b, (((1,), (1,)), ((), ())), preferred_element_type=F32)


def _dot_tn(a, b):
    return lax.dot_general(a, b, (((0,), (0,)), ((), ())), preferred_element_type=F32)


def _silu(x):
    return x * jax.nn.sigmoid(x)


def _in_proj0_kernel(x_ref, xh_ref, g_ref, w_ref, ws_ref, cw_ref, o_ref, os_ref, xn_ref, u_ref,
                     *, bm, blocks_per_seq, n_conv_blocks):
    i = pl.program_id(0)
    j = pl.program_id(1)

    @pl.when(j == 0)
    def _():
        g = g_ref[...]
        halo = _rms(xh_ref[...], g)
        halo = jnp.where(i % blocks_per_seq == 0, 0.0, halo)
        xn_ref[0:CONV_HALO, :] = halo.astype(BF16)
        xn = _rms(x_ref[...], g).astype(BF16)
        xn_ref[CONV_HALO:, :] = xn
        os_ref[...] = _dot(xn, ws_ref[...])

    def conv_silu():
        u_ref[...] = _dot(xn_ref[...], w_ref[...])
        cw = cw_ref[...]
        y = cw[GDN_CONV - 1:GDN_CONV] * u_ref[pl.ds(CONV_HALO, bm), :]
        for s in range(1, GDN_CONV):
            y = y + cw[GDN_CONV - 1 - s:GDN_CONV - s] * u_ref[pl.ds(CONV_HALO - s, bm), :]
        return _silu(y)

    @pl.when(j < n_conv_blocks - 1)
    def _():
        y = conv_silu()
        scale = jnp.where(j == 0, GDN_D ** -0.5, 1.0)
        for h in range(y.shape[1] // GDN_D):
            cols = slice(h * GDN_D, (h + 1) * GDN_D)
            yh = y[:, cols]
            inv = lax.rsqrt(jnp.sum(yh * yh, axis=-1, keepdims=True) + NORM_EPS) * scale
            o_ref[:, cols] = (yh * inv).astype(o_ref.dtype)

    @pl.when(j == n_conv_blocks - 1)
    def _():
        o_ref[...] = conv_silu().astype(o_ref.dtype)

    @pl.when(j >= n_conv_blocks)
    def _():
        o_ref[...] = _dot(xn_ref[CONV_HALO:, :], w_ref[...]).astype(o_ref.dtype)


def _in_proj0(x, g, w, w_side, conv_w, *, seq, bm, bn):
    t, d = x.shape
    n = w.shape[1]
    ns = w_side.shape[1]
    n_conv_blocks = conv_w.shape[1] // bn
    hb = bm // CONV_HALO
    kern = functools.partial(_in_proj0_kernel, bm=bm, blocks_per_seq=seq // bm, n_conv_blocks=n_conv_blocks)
    return pl.pallas_call(
        kern,
        grid=(t // bm, n // bn),
        in_specs=[
            pl.BlockSpec((bm, d), lambda i, j: (i, 0)),
            pl.BlockSpec((CONV_HALO, d), lambda i, j: (jnp.maximum(i * hb - 1, 0), 0)),
            pl.BlockSpec((1, d), lambda i, j: (0, 0)),
            pl.BlockSpec((d, bn), lambda i, j: (0, j)),
            pl.BlockSpec((d, ns), lambda i, j: (0, 0)),
            pl.BlockSpec((GDN_CONV, bn), lambda i, j: (0, jnp.minimum(j, n_conv_blocks - 1))),
        ],
        out_specs=[pl.BlockSpec((bm, bn), lambda i, j: (i, j)),
                   pl.BlockSpec((bm, ns), lambda i, j: (i, 0))],
        out_shape=[jax.ShapeDtypeStruct((t, n), BF16), jax.ShapeDtypeStruct((t, ns), F32)],
        scratch_shapes=[pltpu.VMEM((bm + CONV_HALO, d), BF16), pltpu.VMEM((bm + CONV_HALO, bn), F32)],
        compiler_params=_params(2), name="in_proj0",
    )(x, x, g.reshape(1, d), w, w_side, conv_w)


def _in_proj1_kernel(x_ref, g_ref, w_ref, cs_ref, o_ref, xn_ref, *, n_q, n_k, n_v):
    j = pl.program_id(1)
    half = RET_DK // 2

    @pl.when(j == 0)
    def _():
        xn_ref[...] = _rms(x_ref[...], g_ref[...]).astype(BF16)

    def proj():
        return _dot(xn_ref[...], w_ref[...])

    def rope_store(scale):
        y = proj()
        cos = cs_ref[:, :half]
        sin = cs_ref[:, half:]
        for h in range(y.shape[1] // RET_DK):
            x1 = y[:, h * RET_DK:h * RET_DK + half]
            x2 = y[:, h * RET_DK + half:(h + 1) * RET_DK]
            o_ref[:, h * RET_DK:h * RET_DK + half] = ((x1 * cos - x2 * sin) * scale).astype(o_ref.dtype)
            o_ref[:, h * RET_DK + half:(h + 1) * RET_DK] = ((x2 * cos + x1 * sin) * scale).astype(o_ref.dtype)

    @pl.when(j < n_q)
    def _():
        rope_store(1.0)

    @pl.when((j >= n_q) & (j < n_q + n_k))
    def _():
        rope_store(RET_DK ** -0.5)

    @pl.when((j >= n_q + n_k) & (j < n_q + n_k + n_v))
    def _():
        o_ref[...] = proj().astype(o_ref.dtype)

    @pl.when(j >= n_q + n_k + n_v)
    def _():
        o_ref[...] = _silu(proj()).astype(o_ref.dtype)


def _in_proj1(x, g, w, tab, *, bm, bn):
    t, d = x.shape
    n = w.shape[1]
    n_q = RET_HEADS * RET_DK // bn
    n_v = RET_HEADS * RET_DV // bn
    return pl.pallas_call(
        functools.partial(_in_proj1_kernel, n_q=n_q, n_k=n_q, n_v=n_v),
        grid=(t // bm, n // bn),
        in_specs=[pl.BlockSpec((bm, d), lambda i, j: (i, 0)),
                  pl.BlockSpec((1, d), lambda i, j: (0, 0)),
                  pl.BlockSpec((d, bn), lambda i, j: (0, j)),
                  pl.BlockSpec((bm, RET_DK), lambda i, j: (i, 0))],
        out_specs=pl.BlockSpec((bm, bn), lambda i, j: (i, j)),
        out_shape=jax.ShapeDtypeStruct((t, n), BF16),
        scratch_shapes=[pltpu.VMEM((bm, d), BF16)],
        compiler_params=_params(2), name="in_proj1",
    )(x, g.reshape(1, d), w, tab)


def _proj_res_kernel(*refs, n_pairs):
    res_ref, o_ref = refs[2 * n_pairs], refs[2 * n_pairs + 1]
    acc = res_ref[...]
    for p in range(n_pairs):
        acc = acc + _dot(refs[2 * p][...], refs[2 * p + 1][...])
    o_ref[...] = acc


def _proj_res(pairs, res, *, bm, bn):
    t, n = res.shape
    in_specs, args = [], []
    for y, w in pairs:
        kk = y.shape[1]
        in_specs += [pl.BlockSpec((bm, kk), lambda i, j: (i, 0)),
                     pl.BlockSpec((kk, bn), lambda i, j: (0, j))]
        args += [y, w]
    in_specs.append(pl.BlockSpec((bm, bn), lambda i, j: (i, j)))
    args.append(res)
    return pl.pallas_call(
        functools.partial(_proj_res_kernel, n_pairs=len(pairs)),
        grid=(t // bm, n // bn),
        in_specs=in_specs,
        out_specs=pl.BlockSpec((bm, bn), lambda i, j: (i, j)),
        out_shape=jax.ShapeDtypeStruct((t, n), F32),
        compiler_params=_params(2), name="proj_res",
    )(*args)


def _ffn_up_kernel(h_ref, hh_ref, g_ref, wg_ref, wu_ref, cg_ref, cu_ref, bg_ref, bu_ref,
                   o_ref, xn_ref, ug_ref, uu_ref, *, bm, blocks_per_seq):
    i = pl.program_id(0)

    @pl.when(pl.program_id(1) == 0)
    def _():
        g = g_ref[...]
        halo = _rms(hh_ref[...], g)
        halo = jnp.where(i % blocks_per_seq == 0, 0.0, halo)
        xn_ref[0:CONV_HALO, :] = halo.astype(BF16)
        xn_ref[CONV_HALO:, :] = _rms(h_ref[...], g).astype(BF16)

    xn = xn_ref[...]
    ug_ref[...] = _dot(xn, wg_ref[...])
    uu_ref[...] = _dot(xn, wu_ref[...])

    def conv(u_ref, c_ref, b_ref):
        cw = c_ref[...]
        y = b_ref[...] + cw[FFN_CONV - 1:FFN_CONV] * u_ref[pl.ds(CONV_HALO, bm), :]
        for s in range(1, FFN_CONV):
            y = y + cw[FFN_CONV - 1 - s:FFN_CONV - s] * u_ref[pl.ds(CONV_HALO - s, bm), :]
        return y

    gate = conv(ug_ref, cg_ref, bg_ref)
    up = conv(uu_ref, cu_ref, bu_ref)
    o_ref[...] = (_silu(gate) * up).astype(o_ref.dtype)


def _ffn_up(h, g, w_up, conv_w, conv_b, *, seq, bm, bn):
    t, d = h.shape
    dff = w_up.shape[1] // 2
    nj = dff // bn
    hb = bm // CONV_HALO
    kern = functools.partial(_ffn_up_kernel, bm=bm, blocks_per_seq=seq // bm)
    return pl.pallas_call(
        kern,
        grid=(t // bm, nj),
        in_specs=[
            pl.BlockSpec((bm, d), lambda i, j: (i, 0)),
            pl.BlockSpec((CONV_HALO, d), lambda i, j: (jnp.maximum(i * hb - 1, 0), 0)),
            pl.BlockSpec((1, d), lambda i, j: (0, 0)),
            pl.BlockSpec((d, bn), lambda i, j: (0, j)),
            pl.BlockSpec((d, bn), lambda i, j: (0, j + nj)),
            pl.BlockSpec((FFN_CONV, bn), lambda i, j: (0, j)),
            pl.BlockSpec((FFN_CONV, bn), lambda i, j: (0, j + nj)),
            pl.BlockSpec((1, bn), lambda i, j: (0, j)),
            pl.BlockSpec((1, bn), lambda i, j: (0, j + nj)),
        ],
        out_specs=pl.BlockSpec((bm, bn), lambda i, j: (i, j)),
        out_shape=jax.ShapeDtypeStruct((t, dff), BF16),
        scratch_shapes=[pltpu.VMEM((bm + CONV_HALO, d), BF16),
                        pltpu.VMEM((bm + CONV_HALO, bn), F32),
                        pltpu.VMEM((bm + CONV_HALO, bn), F32)],
        compiler_params=_params(2), name="ffn_up",
    )(h, h, g.reshape(1, d), w_up, w_up, conv_w, conv_w,
      conv_b.reshape(1, -1), conv_b.reshape(1, -1))


def _ple_kernel(h_ref, p_ref, gn_ref, wp_ref, wg_ref, *rest, bn, final):
    if final:
        fn_ref, o_ref = rest
    else:
        (o_ref,) = rest
    d = h_ref.shape[1]
    hn = _rms(h_ref[...], gn_ref[...]).astype(BF16)
    pb = p_ref[...].astype(BF16)
    ssq = None
    for c in range(d // bn):
        cols = slice(c * bn, (c + 1) * bn)
        gate = _dot(hn, wg_ref[:, cols])
        proj = _dot(pb, wp_ref[:, cols])
        out = h_ref[:, cols] + proj * jax.nn.sigmoid(gate)
        o_ref[:, cols] = out
        if final:
            part = jnp.sum(out * out, axis=-1, keepdims=True)
            ssq = part if ssq is None else ssq + part
    if final:
        scale = lax.rsqrt(ssq / d + NORM_EPS)
        o_ref[...] = o_ref[...] * scale * fn_ref[...]


def _ple(h, p, gate_norm, w_proj, w_gate, final_norm=None, *, bm, bn):
    t, d = h.shape
    dp = p.shape[1]
    final = final_norm is not None
    in_specs = [pl.BlockSpec((bm, d), lambda i: (i, 0)),
                pl.BlockSpec((bm, dp), lambda i: (i, 0)),
                pl.BlockSpec((1, d), lambda i: (0, 0)),
                pl.BlockSpec((dp, d), lambda i: (0, 0)),
                pl.BlockSpec((d, d), lambda i: (0, 0))]
    args = [h, p, gate_norm.reshape(1, d), w_proj, w_gate]
    if final:
        in_specs.append(pl.BlockSpec((1, d), lambda i: (0, 0)))
        args.append(final_norm.reshape(1, d))
    return pl.pallas_call(
        functools.partial(_ple_kernel, bn=bn, final=final),
        grid=(t // bm,),
        in_specs=in_specs,
        out_specs=pl.BlockSpec((bm, d), lambda i: (i, 0)),
        out_shape=jax.ShapeDtypeStruct((t, d), F32),
        compiler_params=_params(1), name="ple",
    )(*args)


def _split3(x):
    hi = x.astype(BF16).astype(F32)
    r1 = x - hi
    mid = r1.astype(BF16).astype(F32)
    lo = (r1 - mid).astype(BF16).astype(F32)
    return hi, mid, lo


def _lane_pick(x, lane, idx):
    return jnp.sum(jnp.where(lane == idx, x, 0.0), axis=-1, keepdims=True)


def _gdn_kernel(q_ref, k_ref, v_ref, z_ref, ab_ref, hp_ref, nw_ref, tril_ref, o_ref, s_ref, *, lb, nh):
    c = GDN_CHUNK
    n_chunks = lb // c
    dk = GDN_D

    @pl.when(pl.program_id(2) == 0)
    def _():
        s_ref[...] = jnp.zeros_like(s_ref)

    ab = ab_ref[...]
    lane = lax.broadcasted_iota(jnp.int32, ab.shape, 1)
    hp = hp_ref[...]
    xa = ab + hp[1:2]
    softplus = jnp.maximum(xa, 0.0) + jnp.log1p(jnp.exp(-jnp.abs(xa)))
    g_lanes = -jnp.exp(hp[0:1]) * softplus
    beta_lanes = jax.nn.sigmoid(ab)

    def pieces_to_lanes(x, first, sign):
        hi, mid, lo = _split3(x)
        out = jnp.where(lane == first, sign * hi, 0.0)
        out = jnp.where(lane == first + 1, sign * mid, out)
        return jnp.where(lane == first + 2, sign * lo, out)

    row2 = lax.broadcasted_iota(jnp.int32, (c, 2 * c), 0)
    lane2 = lax.broadcasted_iota(jnp.int32, (c, 2 * c), 1)
    col2 = lane2 & (c - 1)
    incl2 = row2 >= col2
    strict2 = row2 > col2
    left = lane2 < c
    eye_left = jnp.where(left & (row2 == lane2), 1.0, 0.0)
    zeros_k = jnp.zeros((c, 2 * c), BF16)
    zeros_x = jnp.zeros((c, 2 * dk), BF16)

    items = []
    for hi in range(nh):
        head = pl.program_id(1) * nh + hi
        cols = slice(hi * dk, (hi + 1) * dk)
        qb = q_ref[:, cols]
        kb16 = k_ref[:, cols]
        q = qb.astype(F32)
        k = kb16.astype(F32)
        v = v_ref[:, cols].astype(F32)
        g_col = _lane_pick(g_lanes, lane, head)
        beta = _lane_pick(beta_lanes, lane, head + GDN_HEADS)
        gm = _dot(tril_ref[...], pieces_to_lanes(g_col, 0, 1.0).astype(BF16))
        big_g = jnp.sum(gm, axis=-1, keepdims=True)
        e_g = jnp.exp(big_g)
        am = jnp.where(lane < 3, pieces_to_lanes(big_g, 0, 1.0), jnp.where(lane < 6, 1.0, 0.0)).astype(BF16)
        bmat = jnp.where(lane < 3, 1.0, pieces_to_lanes(big_g, 3, -1.0)).astype(BF16)
        kbeta = k * beta
        x0 = jnp.concatenate([v * beta, kbeta * e_g], axis=-1).astype(BF16)
        qd = q * e_g
        kbeta16 = kbeta.astype(BF16)
        for n in range(n_chunks):
            r = slice(n * c, (n + 1) * c)
            g_last = big_g[(n + 1) * c - 1:(n + 1) * c, :]
            items.append(dict(
                am=am[r], bm2=jnp.concatenate([bmat[r], bmat[r]], axis=0),
                kb=kbeta16[r], k=kb16[r], k2=jnp.concatenate([kb16[r], kb16[r]], axis=0), q=qb[r],
                x0=jnp.concatenate([x0[r], zeros_x], axis=0), qd=qd[r],
                kd=(k[r] * jnp.exp(g_last - big_g[r])).astype(BF16), gl=jnp.exp(g_last)))

    for it in items:
        it["gamma2"] = jnp.exp(jnp.where(incl2, _dot_nt(it["am"], it["bm2"]), -jnp.inf))
    for it in items:
        it["tp"] = jnp.where(left, eye_left, jnp.where(strict2, -_dot_nt(it["kb"], it["k2"]) * it["gamma2"], 0.0))
    for it in items:
        it["attn"] = (_dot_nt(it["q"], it["k"]) * it["gamma2"][:, :c]).astype(BF16)

    for j in range(6):
        for it in items:
            it["tpb"] = it["tp"].astype(BF16)
        for it in items:
            upd = _dot(it["tpb"], jnp.concatenate([zeros_k, it["tpb"]], axis=0))
            it["tp"] = jnp.where(left, it["tp"] + upd, upd)
    for it in items:
        it["tpb"] = it["tp"].astype(BF16)
    for it in items:
        it["xb"] = _dot(it["tpb"], it["x0"]).astype(BF16)

    for it in items:
        it["au"] = _dot(it["attn"], it["xb"])
    for it in items:
        it["ku"] = _dot_tn(it["kd"], it["xb"])
    for it in items:
        it["lhs"] = jnp.concatenate([it["qd"] - it["au"][:, dk:], it["ku"][:, dk:]], axis=0).astype(BF16)

    states = [s_ref[hi] for hi in range(nh)]
    outs = [[None] * n_chunks for _ in range(nh)]
    for n in range(n_chunks):
        for hi in range(nh):
            it = items[hi * n_chunks + n]
            prod = _dot(it["lhs"], states[hi].astype(BF16))
            outs[hi][n] = prod[:c] + it["au"][:, :dk]
            states[hi] = states[hi] * it["gl"] - prod[c:] + it["ku"][:, :dk]
    for hi in range(nh):
        s_ref[hi] = states[hi]

    nw = nw_ref[...]
    for hi in range(nh):
        cols = slice(hi * dk, (hi + 1) * dk)
        o = jnp.concatenate(outs[hi], axis=0)
        o_ref[:, cols] = (_rms(o, nw) * _silu(z_ref[:, cols].astype(F32))).astype(o_ref.dtype)


def _gdn(main, side, a_log, dt_bias, norm_w, *, batch, seq, lb, nh):
    t = main.shape[0]
    ns = seq // lb
    hh = GDN_HEADS
    ng = hh // nh
    w = nh * GDN_D
    hp = jnp.zeros((8, LANES), F32)
    hp = hp.at[0, :hh].set(a_log).at[1, :hh].set(dt_bias)
    r = jnp.arange(lb)
    tril = ((r[:, None] >= r[None, :]) & (r[:, None] // GDN_CHUNK == r[None, :] // GDN_CHUNK)).astype(BF16)
    row = lambda b, h, s: b * ns + s
    return pl.pallas_call(
        functools.partial(_gdn_kernel, lb=lb, nh=nh),
        grid=(batch, ng, ns),
        in_specs=[
            pl.BlockSpec((lb, w), lambda b, h, s: (row(b, h, s), h)),
            pl.BlockSpec((lb, w), lambda b, h, s: (row(b, h, s), ng + h)),
            pl.BlockSpec((lb, w), lambda b, h, s: (row(b, h, s), 2 * ng + h)),
            pl.BlockSpec((lb, w), lambda b, h, s: (row(b, h, s), 3 * ng + h)),
            pl.BlockSpec((lb, LANES), lambda b, h, s: (row(b, h, s), 1)),
            pl.BlockSpec((8, LANES), lambda b, h, s: (0, 0)),
            pl.BlockSpec((1, GDN_D), lambda b, h, s: (0, 0)),
            pl.BlockSpec((lb, lb), lambda b, h, s: (0, 0)),
        ],
        out_specs=pl.BlockSpec((lb, w), lambda b, h, s: (row(b, h, s), h)),
        out_shape=jax.ShapeDtypeStruct((t, hh * GDN_D), BF16),
        scratch_shapes=[pltpu.VMEM((nh, GDN_D, GDN_D), F32)],
        compiler_params=_params(3), name="gdn",
    )(main, main, main, main, side, hp, norm_w.reshape(1, GDN_D), tril)


def _mla_up_kernel(cq_ref, ckv_ref, side_ref, tab_ref, qn_ref, kvn_ref, wq_ref, wk_ref, wvt_ref,
                   q_out, k_out, v_out):
    hh = MLA_HEADS
    tab = tab_ref[...]
    scale = (MLA_NOPE + MLA_ROPE) ** -0.5 * LOG2_E

    cq = _rms(cq_ref[...].astype(F32), qn_ref[...]).astype(BF16)
    for h in range(hh):
        qh = _dot(cq, wq_ref[:, h * 256:(h + 1) * 256])
        q_out[:, h * 256:h * 256 + LANES] = (qh[:, :LANES] * scale).astype(BF16)
        q_out[:, h * 256 + LANES:(h + 1) * 256] = (qh[:, LANES:] * tab * scale).astype(BF16)

    kr = side_ref[...] * tab
    kpe = (kr + pltpu.roll(kr, shift=MLA_ROPE, axis=1)).astype(BF16)

    ckv = _rms(ckv_ref[...].astype(F32), kvn_ref[...]).astype(BF16)
    for h in range(hh):
        k_out[:, h * 256:h * 256 + LANES] = _dot(ckv, wk_ref[:, h * LANES:(h + 1) * LANES]).astype(BF16)
        k_out[:, h * 256 + LANES:(h + 1) * 256] = kpe
    v_out[...] = _dot_nt(wvt_ref[...], ckv).astype(BF16)


def _mla_up(main, side, tab, q_norm, kv_norm, wq, wk, wvt, *, bm):
    t = main.shape[0]
    r = MLA_RANK
    cq_blk = (GDN_HEADS * GDN_D * 4) // r
    return pl.pallas_call(
        _mla_up_kernel,
        grid=(t // bm,),
        in_specs=[
            pl.BlockSpec((bm, r), lambda i: (i, cq_blk)),
            pl.BlockSpec((bm, r), lambda i: (i, cq_blk + 1)),
            pl.BlockSpec((bm, LANES), lambda i: (i, 0)),
            pl.BlockSpec((bm, LANES), lambda i: (i, 0)),
            pl.BlockSpec((1, r), lambda i: (0, 0)),
            pl.BlockSpec((1, r), lambda i: (0, 0)),
            pl.BlockSpec(wq.shape, lambda i: (0, 0)),
            pl.BlockSpec(wk.shape, lambda i: (0, 0)),
            pl.BlockSpec(wvt.shape, lambda i: (0, 0)),
        ],
        out_specs=[pl.BlockSpec((bm, MLA_HEADS * 256), lambda i: (i, 0)),
                   pl.BlockSpec((bm, MLA_HEADS * 256), lambda i: (i, 0)),
                   pl.BlockSpec((MLA_HEADS * MLA_V, bm), lambda i: (0, i))],
        out_shape=[jax.ShapeDtypeStruct((t, MLA_HEADS * 256), BF16),
                   jax.ShapeDtypeStruct((t, MLA_HEADS * 256), BF16),
                   jax.ShapeDtypeStruct((MLA_HEADS * MLA_V, t), BF16)],
        compiler_params=_params(1), name="mla_up",
    )(main, main, side, tab, q_norm.reshape(1, r), kv_norm.reshape(1, r), wq, wk, wvt)


def _flash_kernel(q_ref, k_ref, vt_ref, o_ref, *, blk, nh):
    qi = pl.program_id(2)
    dq = 2 * LANES
    qs = [q_ref[:, h * dq:(h + 1) * dq] for h in range(nh)]

    def step(j, carry, masked):
        start = pl.multiple_of(j * blk, blk)
        ss = [_dot_nt(k_ref[pl.ds(start, blk), h * dq:(h + 1) * dq], qs[h]) for h in range(nh)]
        if masked:
            key = lax.broadcasted_iota(jnp.int32, ss[0].shape, 0)
            qry = lax.broadcasted_iota(jnp.int32, ss[0].shape, 1)
            ss = [jnp.where(key <= qry, s, -jnp.inf) for s in ss]
        m_new = [jnp.maximum(carry[h][0], jnp.max(ss[h], axis=0, keepdims=True)) for h in range(nh)]
        ps = [jnp.exp2(ss[h] - m_new[h]) for h in range(nh)]
        pv = [_dot(vt_ref[h * MLA_V:(h + 1) * MLA_V, pl.ds(start, blk)], ps[h].astype(BF16)) for h in range(nh)]
        out = []
        for h in range(nh):
            m, l, acc = carry[h]
            alpha = jnp.exp2(m - m_new[h])
            out.append((m_new[h], l * alpha + jnp.sum(ps[h], axis=0, keepdims=True), acc * alpha + pv[h]))
        return tuple(out)

    init = tuple((jnp.full((1, blk), -jnp.inf, F32), jnp.zeros((1, blk), F32),
                  jnp.zeros((MLA_V, blk), F32)) for _ in range(nh))
    carry = lax.fori_loop(0, qi, lambda j, cr: step(j, cr, False), init)
    carry = step(qi, carry, True)
    for h in range(nh):
        _, l, acc = carry[h]
        o_ref[:, h * MLA_V:(h + 1) * MLA_V] = (acc / l).T.astype(o_ref.dtype)


def _flash(qf, kf, vt, *, batch, seq, blk, nh):
    t = qf.shape[0]
    nq = seq // blk
    hh = MLA_HEADS
    return pl.pallas_call(
        functools.partial(_flash_kernel, blk=blk, nh=nh),
        grid=(batch, hh // nh, nq),
        in_specs=[
            pl.BlockSpec((blk, nh * 256), lambda b, h, i: (b * nq + i, h)),
            pl.BlockSpec((seq, nh * 256), lambda b, h, i: (b, h)),
            pl.BlockSpec((nh * MLA_V, seq), lambda b, h, i: (h, b)),
        ],
        out_specs=pl.BlockSpec((blk, nh * MLA_V), lambda b, h, i: (b * nq + i, h)),
        out_shape=jax.ShapeDtypeStruct((t, hh * MLA_V), BF16),
        compiler_params=_params(3), name="mla_flash",
    )(qf, kf, vt)


def _ret_kernel(lg_ref, q_ref, k_ref, v_ref, g_ref, nw_ref, o_ref,
                s_ref, dec_ref, xz_ref, *, c, nh):
    @pl.when(pl.program_id(2) == 0)
    def _():
        row = lax.broadcasted_iota(jnp.int32, (c, c), 0)
        col = lax.broadcasted_iota(jnp.int32, (c, c), 1)
        pos = lax.broadcasted_iota(jnp.int32, (c, LANES), 0).astype(F32)
        lane = lax.broadcasted_iota(jnp.int32, (c, LANES), 1)
        for h in range(nh):
            lg = lg_ref[h, :, 0:1]
            dec_ref[h] = jnp.exp(jnp.where(row >= col, (row - col).astype(F32) * lg, -jnp.inf))
            xi = jnp.exp((pos + 1.0) * lg)
            zeta = jnp.exp((c - 1.0 - pos) * lg)
            xz_ref[h] = jnp.where(lane == 0, xi, zeta)
        s_ref[...] = jnp.zeros_like(s_ref)

    hs = range(nh)
    qb = [q_ref[:, h * RET_DK:(h + 1) * RET_DK] for h in hs]
    kb = [k_ref[:, h * RET_DK:(h + 1) * RET_DK] for h in hs]
    v = [v_ref[:, h * RET_DV:(h + 1) * RET_DV] for h in hs]
    qk = [(_dot_nt(qb[h], kb[h]) * dec_ref[h]).astype(BF16) for h in hs]
    states = [s_ref[h] for h in hs]
    cross = [_dot((qb[h].astype(F32) * xz_ref[h, :, 0:1]).astype(BF16), states[h].astype(BF16)) for h in hs]
    inner = [_dot(qk[h], v[h]) for h in hs]
    upd = [_dot_tn((kb[h].astype(F32) * xz_ref[h, :, 1:2]).astype(BF16), v[h]) for h in hs]
    for h in hs:
        s_ref[h] = states[h] * xz_ref[h, c - 1:c, 0:1] + upd[h]
    for h in hs:
        o = inner[h] + cross[h]
        mu = jnp.mean(o, axis=-1, keepdims=True)
        xc = o - mu
        var = jnp.mean(xc * xc, axis=-1, keepdims=True)
        cols = slice(h * RET_DV, (h + 1) * RET_DV)
        y = xc * lax.rsqrt(var + NORM_EPS) * nw_ref[:, cols]
        o_ref[:, cols] = (g_ref[:, cols].astype(F32) * y).astype(o_ref.dtype)


def _retention(main, norm_w, *, batch, seq, c, nh):
    t = main.shape[0]
    nc = seq // c
    hh = RET_HEADS
    ng = hh // nh
    log_gamma = jnp.log1p(-jnp.power(2.0, -5.0 - jnp.arange(hh, dtype=F32)))
    lg = jnp.broadcast_to(log_gamma[:, None, None], (hh, 1, LANES))
    row = lambda b, h, s: b * nc + s
    kq = RET_DK
    v_blk0 = (2 * hh * kq) // (nh * RET_DV)
    return pl.pallas_call(
        functools.partial(_ret_kernel, c=c, nh=nh),
        grid=(batch, ng, nc),
        in_specs=[
            pl.BlockSpec((nh, 1, LANES), lambda b, h, s: (h, 0, 0)),
            pl.BlockSpec((c, nh * kq), lambda b, h, s: (row(b, h, s), h)),
            pl.BlockSpec((c, nh * kq), lambda b, h, s: (row(b, h, s), ng + h)),
            pl.BlockSpec((c, nh * RET_DV), lambda b, h, s: (row(b, h, s), v_blk0 + h)),
            pl.BlockSpec((c, nh * RET_DV), lambda b, h, s: (row(b, h, s), v_blk0 + ng + h)),
            pl.BlockSpec((1, nh * RET_DV), lambda b, h, s: (0, h)),
        ],
        out_specs=pl.BlockSpec((c, nh * RET_DV), lambda b, h, s: (row(b, h, s), h)),
        out_shape=jax.ShapeDtypeStruct((t, hh * RET_DV), BF16),
        scratch_shapes=[pltpu.VMEM((nh, kq, RET_DV), F32), pltpu.VMEM((nh, c, c), F32),
                        pltpu.VMEM((nh, c, LANES), F32)],
        compiler_params=_params(3), name="retention",
    )(lg, main, main, main, main, norm_w.reshape(1, -1))


def _rotate_half_cols(w, half):
    return jnp.concatenate([-w[..., half:], w[..., :half]], axis=-1)


def _rope_tab(positions, dim):
    inv_freq = ROPE_THETA ** (-jnp.arange(0, dim, 2, dtype=F32) / dim)
    ang = positions.astype(F32).reshape(-1, 1) * inv_freq
    return jnp.cos(ang), jnp.sin(ang)


def _conv_ffn(h, norm, w_up, conv_w, conv_b, w_down, *, seq):
    act = _ffn_up(h, norm, w_up.astype(BF16), conv_w, conv_b, seq=seq, bm=1024, bn=512)
    return _proj_res([(act, w_down.astype(BF16))], h, bm=1024, bn=512)


def kernel(x, p, positions, l0_attn_norm, l0_w_in, l0_gdn_conv, l0_gdn_A_log, l0_gdn_dt_bias, l0_gdn_norm, l0_mla_q_norm, l0_mla_w_uq, l0_mla_kv_norm, l0_mla_w_ukv, l0_w_out, l0_ffn_norm, l0_ffn_w_up, l0_ffn_conv_w, l0_ffn_conv_b, l0_ffn_w_down, l0_ple_proj, l0_ple_gate_norm, l0_ple_gate, l1_attn_norm, l1_w_in, l1_ret_norm, l1_w_out, l1_ffn_norm, l1_ffn_w_up, l1_ffn_conv_w, l1_ffn_conv_b, l1_ffn_w_down, l1_ple_proj, l1_ple_gate_norm, l1_ple_gate, final_norm):
    batch, seq, d = x.shape
    t = batch * seq
    x2 = x.reshape(t, d)
    p2 = p.reshape(p.shape[0], t, p.shape[-1])

    gq = GDN_HEADS * GDN_D * 4
    n_ab = 2 * GDN_HEADS
    c0 = gq + n_ab
    w_main = jnp.concatenate([l0_w_in[:, :gq], l0_w_in[:, c0:c0 + 2 * MLA_RANK]], axis=1).astype(BF16)
    w_kr = l0_w_in[:, c0 + 2 * MLA_RANK:]
    w_side = jnp.concatenate(
        [w_kr, _rotate_half_cols(w_kr, MLA_ROPE // 2), l0_w_in[:, gq:c0],
         jnp.zeros((d, LANES - n_ab), F32)], axis=1).astype(BF16)

    wq = l0_mla_w_uq.reshape(MLA_RANK, MLA_HEADS, MLA_NOPE + MLA_ROPE)
    wq_pe = wq[:, :, MLA_NOPE:]
    wq = jnp.concatenate([wq[:, :, :MLA_NOPE], wq_pe, _rotate_half_cols(wq_pe, MLA_ROPE // 2)], axis=-1)
    wq = wq.reshape(MLA_RANK, MLA_HEADS * 256).astype(BF16)
    wkv = l0_mla_w_ukv.reshape(MLA_RANK, MLA_HEADS, MLA_NOPE + MLA_V)
    wk = wkv[:, :, :MLA_NOPE].reshape(MLA_RANK, -1).astype(BF16)
    wvt = wkv[:, :, MLA_NOPE:].reshape(MLA_RANK, -1).T.astype(BF16)

    cos_m, sin_m = _rope_tab(positions, MLA_ROPE)
    tab_m = jnp.concatenate([cos_m, cos_m, sin_m, sin_m], axis=1)
    cos_r, sin_r = _rope_tab(positions, RET_DK)
    tab_r = jnp.concatenate([cos_r, sin_r], axis=1)

    main0, side0 = _in_proj0(x2, l0_attn_norm, w_main, w_side, l0_gdn_conv, seq=seq, bm=1024, bn=1024)
    y_a = _gdn(main0, side0, l0_gdn_A_log, l0_gdn_dt_bias, l0_gdn_norm,
               batch=batch, seq=seq, lb=256, nh=8)
    qf, kf, vt = _mla_up(main0, side0, tab_m, l0_mla_q_norm, l0_mla_kv_norm, wq, wk, wvt, bm=512)
    y_b = _flash(qf, kf, vt, batch=batch, seq=seq, blk=512, nh=4)
    w_out0 = l0_w_out.astype(BF16)
    ha = GDN_HEADS * GDN_D
    h = _proj_res([(y_a, w_out0[:ha]), (y_b, w_out0[ha:])], x2, bm=1024, bn=1024)
    h = _conv_ffn(h, l0_ffn_norm, l0_ffn_w_up, l0_ffn_conv_w, l0_ffn_conv_b, l0_ffn_w_down, seq=seq)
    h = _ple(h, p2[0], l0_ple_gate_norm, l0_ple_proj.astype(BF16), l0_ple_gate.astype(BF16),
             bm=512, bn=512)

    main1 = _in_proj1(h, l1_attn_norm, l1_w_in.astype(BF16), tab_r, bm=1024, bn=1024)
    y = _retention(main1, l1_ret_norm, batch=batch, seq=seq, c=256, nh=8)
    h = _proj_res([(y, l1_w_out.astype(BF16))], h, bm=1024, bn=1024)
    h = _conv_ffn(h, l1_ffn_norm, l1_ffn_w_up, l1_ffn_conv_w, l1_ffn_conv_b, l1_ffn_w_down, seq=seq)
    h = _ple(h, p2[1], l1_ple_gate_norm, l1_ple_proj.astype(BF16), l1_ple_gate.astype(BF16),
             final_norm, bm=512, bn=512)
    return h.reshape(batch, seq, d)
```

```python
import functools

import jax
import jax.numpy as jnp
from jax import lax
from jax.experimental import pallas as pl
from jax.experimental.pallas import tpu as pltpu

F32 = jnp.float32
BF16 = jnp.bfloat16

NORM_EPS = 1e-6
ROPE_THETA = 10000.0

GDN_HEADS = 8
GDN_D = 128
GDN_CONV = 4
GDN_CHUNK = 64
MLA_HEADS = 8
MLA_RANK = 512
MLA_NOPE = 128
MLA_ROPE = 64
MLA_V = 128
RET_HEADS = 8
RET_DK = 256
RET_DV = 512
FFN_CONV = 3

LANES = 128
VMEM_LIMIT = 56 * 1024 * 1024
LOG2_E = 1.4426950408889634
CONV_HALO = 16


def _params(n_grid):
    return pltpu.CompilerParams(dimension_semantics=("arbitrary",) * n_grid,
                                vmem_limit_bytes=VMEM_LIMIT)


def _rms(x, g):
    return x * lax.rsqrt(jnp.mean(x * x, axis=-1, keepdims=True) + NORM_EPS) * g


def _dot(a, b):
    return jnp.dot(a, b, preferred_element_type=F32)


def _dot_nt(a, b):
    return lax.dot_general(a, b, (((1,), (1,)), ((), ())), preferred_element_type=F32)


def _dot_tn(a, b):
    return lax.dot_general(a, b, (((0,), (0,)), ((), ())), preferred_element_type=F32)


def _silu(x):
    return x * jax.nn.sigmoid(x)


def _call(kernel_fn, *, grid, in_specs, out_specs, out_shape, args, name, scratch_shapes=(), casts=()):
    n_in, n_out = len(in_specs), len(out_specs)
    n_steps = 1
    for g in grid:
        n_steps *= g

    def flat_step(*idx):
        step = idx[0]
        for g, i in zip(grid[1:], idx[1:]):
            step = step * g + i
        return step

    in_specs, out_specs, out_shape, args = list(in_specs), list(out_specs), list(out_shape), list(args)
    for w in casts:
        rows, cols = w.shape[0] // n_steps, w.shape[1]
        assert rows * n_steps == w.shape[0] and rows % 16 == 0, (w.shape, n_steps)
        spec = pl.BlockSpec((None, rows, cols), lambda *idx: (flat_step(*idx), 0, 0))
        in_specs.append(spec)
        out_specs.append(spec)
        out_shape.append(jax.ShapeDtypeStruct((n_steps, rows, cols), BF16))
        args.append(w.reshape(n_steps, rows, cols))

    def body(*refs):
        ins, extra_in = refs[:n_in], refs[n_in:n_in + len(casts)]
        outs = refs[n_in + len(casts):n_in + len(casts) + n_out]
        extra_out = refs[n_in + len(casts) + n_out:n_in + 2 * len(casts) + n_out]
        scratch = refs[n_in + 2 * len(casts) + n_out:]
        for src, dst in zip(extra_in, extra_out):
            dst[...] = src[...].astype(BF16)
        kernel_fn(*ins, *outs, *scratch)

    out = pl.pallas_call(
        body, grid=grid, in_specs=in_specs, out_specs=out_specs, out_shape=out_shape,
        scratch_shapes=list(scratch_shapes), compiler_params=_params(len(grid)), name=name,
    )(*args)
    return list(out[:n_out]) + [o.reshape(w.shape) for o, w in zip(out[n_out:], casts)]


def _in_proj0_kernel(x_ref, xh_ref, g_ref, w_ref, ws_ref, cw_ref, o_ref, os_ref, xn_ref, u_ref,
                     *, bm, blocks_per_seq, n_conv_blocks):
    i = pl.program_id(0)
    j = pl.program_id(1)

    @pl.when(j == 0)
    def _():
        g = g_ref[...]
        halo = _rms(xh_ref[...], g)
        halo = jnp.where(i % blocks_per_seq == 0, 0.0, halo)
        xn_ref[0:CONV_HALO, :] = halo.astype(BF16)
        xn = _rms(x_ref[...], g).astype(BF16)
        xn_ref[CONV_HALO:, :] = xn
        os_ref[...] = _dot(xn, ws_ref[...])

    def conv_silu():
        u_ref[...] = _dot(xn_ref[...], w_ref[...])
        cw = cw_ref[...]
        y = cw[GDN_CONV - 1:GDN_CONV] * u_ref[pl.ds(CONV_HALO, bm), :]
        for s in range(1, GDN_CONV):
            y = y + cw[GDN_CONV - 1 - s:GDN_CONV - s] * u_ref[pl.ds(CONV_HALO - s, bm), :]
        return _silu(y)

    @pl.when(j < n_conv_blocks - 1)
    def _():
        y = conv_silu()
        scale = jnp.where(j == 0, GDN_D ** -0.5, 1.0)
        for h in range(y.shape[1] // GDN_D):
            cols = slice(h * GDN_D, (h + 1) * GDN_D)
            yh = y[:, cols]
            inv = lax.rsqrt(jnp.sum(yh * yh, axis=-1, keepdims=True) + NORM_EPS) * scale
            o_ref[:, cols] = (yh * inv).astype(o_ref.dtype)

    @pl.when(j == n_conv_blocks - 1)
    def _():
        o_ref[...] = conv_silu().astype(o_ref.dtype)

    @pl.when(j >= n_conv_blocks)
    def _():
        o_ref[...] = _dot(xn_ref[CONV_HALO:, :], w_ref[...]).astype(o_ref.dtype)


def _in_proj0(x, g, w, w_side, conv_w, *, seq, bm, bn):
    t, d = x.shape
    n = w.shape[1]
    ns = w_side.shape[1]
    n_conv_blocks = conv_w.shape[1] // bn
    hb = bm // CONV_HALO
    kern = functools.partial(_in_proj0_kernel, bm=bm, blocks_per_seq=seq // bm, n_conv_blocks=n_conv_blocks)
    return pl.pallas_call(
        kern,
        grid=(t // bm, n // bn),
        in_specs=[
            pl.BlockSpec((bm, d), lambda i, j: (i, 0)),
            pl.BlockSpec((CONV_HALO, d), lambda i, j: (jnp.maximum(i * hb - 1, 0), 0)),
            pl.BlockSpec((1, d), lambda i, j: (0, 0)),
            pl.BlockSpec((d, bn), lambda i, j: (0, j)),
            pl.BlockSpec((d, ns), lambda i, j: (0, 0)),
            pl.BlockSpec((GDN_CONV, bn), lambda i, j: (0, jnp.minimum(j, n_conv_blocks - 1))),
        ],
        out_specs=[pl.BlockSpec((bm, bn), lambda i, j: (i, j)),
                   pl.BlockSpec((bm, ns), lambda i, j: (i, 0))],
        out_shape=[jax.ShapeDtypeStruct((t, n), BF16), jax.ShapeDtypeStruct((t, ns), F32)],
        scratch_shapes=[pltpu.VMEM((bm + CONV_HALO, d), BF16), pltpu.VMEM((bm + CONV_HALO, bn), F32)],
        compiler_params=_params(2), name="in_proj0",
    )(x, x, g.reshape(1, d), w, w_side, conv_w)


def _in_proj1_kernel(x_ref, g_ref, w_ref, cs_ref, o_ref, xn_ref, *, n_q, n_k, n_v):
    j = pl.program_id(1)
    half = RET_DK // 2

    @pl.when(j == 0)
    def _():
        xn_ref[...] = _rms(x_ref[...], g_ref[...]).astype(BF16)

    def proj():
        return _dot(xn_ref[...], w_ref[...])

    def rope_store(scale):
        y = proj()
        cos = cs_ref[:, :half]
        sin = cs_ref[:, half:]
        for h in range(y.shape[1] // RET_DK):
            x1 = y[:, h * RET_DK:h * RET_DK + half]
            x2 = y[:, h * RET_DK + half:(h + 1) * RET_DK]
            o_ref[:, h * RET_DK:h * RET_DK + half] = ((x1 * cos - x2 * sin) * scale).astype(o_ref.dtype)
            o_ref[:, h * RET_DK + half:(h + 1) * RET_DK] = ((x2 * cos + x1 * sin) * scale).astype(o_ref.dtype)

    @pl.when(j < n_q)
    def _():
        rope_store(1.0)

    @pl.when((j >= n_q) & (j < n_q + n_k))
    def _():
        rope_store(RET_DK ** -0.5)

    @pl.when((j >= n_q + n_k) & (j < n_q + n_k + n_v))
    def _():
        o_ref[...] = proj().astype(o_ref.dtype)

    @pl.when(j >= n_q + n_k + n_v)
    def _():
        o_ref[...] = _silu(proj()).astype(o_ref.dtype)


def _in_proj1(x, g, w, tab, *, bm, bn):
    t, d = x.shape
    n = w.shape[1]
    n_q = RET_HEADS * RET_DK // bn
    n_v = RET_HEADS * RET_DV // bn
    return pl.pallas_call(
        functools.partial(_in_proj1_kernel, n_q=n_q, n_k=n_q, n_v=n_v),
        grid=(t // bm, n // bn),
        in_specs=[pl.BlockSpec((bm, d), lambda i, j: (i, 0)),
                  pl.BlockSpec((1, d), lambda i, j: (0, 0)),
                  pl.BlockSpec((d, bn), lambda i, j: (0, j)),
                  pl.BlockSpec((bm, RET_DK), lambda i, j: (i, 0))],
        out_specs=pl.BlockSpec((bm, bn), lambda i, j: (i, j)),
        out_shape=jax.ShapeDtypeStruct((t, n), BF16),
        scratch_shapes=[pltpu.VMEM((bm, d), BF16)],
        compiler_params=_params(2), name="in_proj1",
    )(x, g.reshape(1, d), w, tab)


def _proj_res_kernel(*refs, n_pairs):
    res_ref, o_ref = refs[2 * n_pairs], refs[2 * n_pairs + 1]
    acc = res_ref[...]
    for p in range(n_pairs):
        acc = acc + _dot(refs[2 * p][...], refs[2 * p + 1][...])
    o_ref[...] = acc


def _proj_res(pairs, res, *, bm, bn, casts=()):
    t, n = res.shape
    in_specs, args = [], []
    for y, w in pairs:
        kk = y.shape[1]
        in_specs += [pl.BlockSpec((bm, kk), lambda i, j: (i, 0)),
                     pl.BlockSpec((kk, bn), lambda i, j: (0, j))]
        args += [y, w]
    in_specs.append(pl.BlockSpec((bm, bn), lambda i, j: (i, j)))
    args.append(res)
    return _call(
        functools.partial(_proj_res_kernel, n_pairs=len(pairs)),
        grid=(t // bm, n // bn),
        in_specs=in_specs,
        out_specs=[pl.BlockSpec((bm, bn), lambda i, j: (i, j))],
        out_shape=[jax.ShapeDtypeStruct((t, n), F32)],
        args=args, name="proj_res", casts=casts)


def _ffn_up_kernel(h_ref, hh_ref, g_ref, wg_ref, wu_ref, cg_ref, cu_ref, bg_ref, bu_ref,
                   o_ref, xn_ref, ug_ref, uu_ref, *, bm, blocks_per_seq):
    i = pl.program_id(0)

    @pl.when(pl.program_id(1) == 0)
    def _():
        g = g_ref[...]
        halo = _rms(hh_ref[...], g)
        halo = jnp.where(i % blocks_per_seq == 0, 0.0, halo)
        xn_ref[0:CONV_HALO, :] = halo.astype(BF16)
        xn_ref[CONV_HALO:, :] = _rms(h_ref[...], g).astype(BF16)

    xn = xn_ref[...]
    ug_ref[...] = _dot(xn, wg_ref[...])
    uu_ref[...] = _dot(xn, wu_ref[...])

    def conv(u_ref, c_ref, b_ref):
        cw = c_ref[...]
        y = b_ref[...] + cw[FFN_CONV - 1:FFN_CONV] * u_ref[pl.ds(CONV_HALO, bm), :]
        for s in range(1, FFN_CONV):
            y = y + cw[FFN_CONV - 1 - s:FFN_CONV - s] * u_ref[pl.ds(CONV_HALO - s, bm), :]
        return y

    gate = conv(ug_ref, cg_ref, bg_ref)
    up = conv(uu_ref, cu_ref, bu_ref)
    o_ref[...] = (_silu(gate) * up).astype(o_ref.dtype)


def _ffn_up(h, g, w_up, conv_w, conv_b, *, seq, bm, bn, casts=()):
    t, d = h.shape
    dff = w_up.shape[1] // 2
    nj = dff // bn
    hb = bm // CONV_HALO
    kern = functools.partial(_ffn_up_kernel, bm=bm, blocks_per_seq=seq // bm)
    return _call(
        kern,
        grid=(t // bm, nj),
        in_specs=[
            pl.BlockSpec((bm, d), lambda i, j: (i, 0)),
            pl.BlockSpec((CONV_HALO, d), lambda i, j: (jnp.maximum(i * hb - 1, 0), 0)),
            pl.BlockSpec((1, d), lambda i, j: (0, 0)),
            pl.BlockSpec((d, bn), lambda i, j: (0, j)),
            pl.BlockSpec((d, bn), lambda i, j: (0, j + nj)),
            pl.BlockSpec((FFN_CONV, bn), lambda i, j: (0, j)),
            pl.BlockSpec((FFN_CONV, bn), lambda i, j: (0, j + nj)),
            pl.BlockSpec((1, bn), lambda i, j: (0, j)),
            pl.BlockSpec((1, bn), lambda i, j: (0, j + nj)),
        ],
        out_specs=[pl.BlockSpec((bm, bn), lambda i, j: (i, j))],
        out_shape=[jax.ShapeDtypeStruct((t, dff), BF16)],
        scratch_shapes=[pltpu.VMEM((bm + CONV_HALO, d), BF16),
                        pltpu.VMEM((bm + CONV_HALO, bn), F32),
                        pltpu.VMEM((bm + CONV_HALO, bn), F32)],
        args=(h, h, g.reshape(1, d), w_up, w_up, conv_w, conv_w,
              conv_b.reshape(1, -1), conv_b.reshape(1, -1)),
        name="ffn_up", casts=casts)


def _ple_kernel(h_ref, p_ref, gn_ref, wp_ref, wg_ref, *rest, bn, final):
    if final:
        fn_ref, o_ref = rest
    else:
        (o_ref,) = rest
    d = h_ref.shape[1]
    hn = _rms(h_ref[...], gn_ref[...]).astype(BF16)
    pb = p_ref[...].astype(BF16)
    ssq = None
    for c in range(d // bn):
        cols = slice(c * bn, (c + 1) * bn)
        gate = _dot(hn, wg_ref[:, cols])
        proj = _dot(pb, wp_ref[:, cols])
        out = h_ref[:, cols] + proj * jax.nn.sigmoid(gate)
        o_ref[:, cols] = out
        if final:
            part = jnp.sum(out * out, axis=-1, keepdims=True)
            ssq = part if ssq is None else ssq + part
    if final:
        scale = lax.rsqrt(ssq / d + NORM_EPS)
        o_ref[...] = o_ref[...] * scale * fn_ref[...]


def _ple(h, p, gate_norm, w_proj, w_gate, final_norm=None, *, bm, bn, casts=()):
    t, d = h.shape
    dp = p.shape[1]
    final = final_norm is not None
    in_specs = [pl.BlockSpec((bm, d), lambda i: (i, 0)),
                pl.BlockSpec((bm, dp), lambda i: (i, 0)),
                pl.BlockSpec((1, d), lambda i: (0, 0)),
                pl.BlockSpec((dp, d), lambda i: (0, 0)),
                pl.BlockSpec((d, d), lambda i: (0, 0))]
    args = [h, p, gate_norm.reshape(1, d), w_proj, w_gate]
    if final:
        in_specs.append(pl.BlockSpec((1, d), lambda i: (0, 0)))
        args.append(final_norm.reshape(1, d))
    return _call(
        functools.partial(_ple_kernel, bn=bn, final=final),
        grid=(t // bm,),
        in_specs=in_specs,
        out_specs=[pl.BlockSpec((bm, d), lambda i: (i, 0))],
        out_shape=[jax.ShapeDtypeStruct((t, d), F32)],
        args=args, name="ple", casts=casts)


def _split3(x):
    hi = x.astype(BF16).astype(F32)
    r1 = x - hi
    mid = r1.astype(BF16).astype(F32)
    lo = (r1 - mid).astype(BF16).astype(F32)
    return hi, mid, lo


def _lane_pick(x, lane, idx):
    return jnp.sum(jnp.where(lane == idx, x, 0.0), axis=-1, keepdims=True)


def _gdn_kernel(q_ref, k_ref, v_ref, z_ref, ab_ref, hp_ref, nw_ref, tril_ref, o_ref, s_ref, *, lb, nh):
    c = GDN_CHUNK
    n_chunks = lb // c
    dk = GDN_D

    @pl.when(pl.program_id(2) == 0)
    def _():
        s_ref[...] = jnp.zeros_like(s_ref)

    ab = ab_ref[...]
    lane = lax.broadcasted_iota(jnp.int32, ab.shape, 1)
    hp = hp_ref[...]
    xa = ab + hp[1:2]
    softplus = jnp.maximum(xa, 0.0) + jnp.log1p(jnp.exp(-jnp.abs(xa)))
    g_lanes = -jnp.exp(hp[0:1]) * softplus
    beta_lanes = jax.nn.sigmoid(ab)

    def pieces_to_lanes(x, first, sign):
        hi, mid, lo = _split3(x)
        out = jnp.where(lane == first, sign * hi, 0.0)
        out = jnp.where(lane == first + 1, sign * mid, out)
        return jnp.where(lane == first + 2, sign * lo, out)

    row2 = lax.broadcasted_iota(jnp.int32, (c, 2 * c), 0)
    lane2 = lax.broadcasted_iota(jnp.int32, (c, 2 * c), 1)
    col2 = lane2 & (c - 1)
    incl2 = row2 >= col2
    strict2 = row2 > col2
    left = lane2 < c
    eye_left = jnp.where(left & (row2 == lane2), 1.0, 0.0)
    zeros_k = jnp.zeros((c, 2 * c), BF16)
    zeros_x = jnp.zeros((c, 2 * dk), BF16)

    items = []
    for hi in range(nh):
        head = pl.program_id(1) * nh + hi
        cols = slice(hi * dk, (hi + 1) * dk)
        qb = q_ref[:, cols]
        kb16 = k_ref[:, cols]
        q = qb.astype(F32)
        k = kb16.astype(F32)
        v = v_ref[:, cols].astype(F32)
        g_col = _lane_pick(g_lanes, lane, head)
        beta = _lane_pick(beta_lanes, lane, head + GDN_HEADS)
        gm = _dot(tril_ref[...], pieces_to_lanes(g_col, 0, 1.0).astype(BF16))
        big_g = jnp.sum(gm, axis=-1, keepdims=True)
        e_g = jnp.exp(big_g)
        am = jnp.where(lane < 3, pieces_to_lanes(big_g, 0, 1.0), jnp.where(lane < 6, 1.0, 0.0)).astype(BF16)
        bmat = jnp.where(lane < 3, 1.0, pieces_to_lanes(big_g, 3, -1.0)).astype(BF16)
        kbeta = k * beta
        x0 = jnp.concatenate([v * beta, kbeta * e_g], axis=-1).astype(BF16)
        qd = q * e_g
        kbeta16 = kbeta.astype(BF16)
        for n in range(n_chunks):
            r = slice(n * c, (n + 1) * c)
            g_last = big_g[(n + 1) * c - 1:(n + 1) * c, :]
            items.append(dict(
                am=am[r], bm2=jnp.concatenate([bmat[r], bmat[r]], axis=0),
                kb=kbeta16[r], k=kb16[r], k2=jnp.concatenate([kb16[r], kb16[r]], axis=0), q=qb[r],
                x0=jnp.concatenate([x0[r], zeros_x], axis=0), qd=qd[r],
                kd=(k[r] * jnp.exp(g_last - big_g[r])).astype(BF16), gl=jnp.exp(g_last)))

    for it in items:
        it["gamma2"] = jnp.exp(jnp.where(incl2, _dot_nt(it["am"], it["bm2"]), -jnp.inf))
    for it in items:
        it["tp"] = jnp.where(left, eye_left, jnp.where(strict2, -_dot_nt(it["kb"], it["k2"]) * it["gamma2"], 0.0))
    for it in items:
        it["attn"] = (_dot_nt(it["q"], it["k"]) * it["gamma2"][:, :c]).astype(BF16)

    for j in range(6):
        for it in items:
            it["tpb"] = it["tp"].astype(BF16)
        for it in items:
            upd = _dot(it["tpb"], jnp.concatenate([zeros_k, it["tpb"]], axis=0))
            it["tp"] = jnp.where(left, it["tp"] + upd, upd)
    for it in items:
        it["tpb"] = it["tp"].astype(BF16)
    for it in items:
        it["xb"] = _dot(it["tpb"], it["x0"]).astype(BF16)

    for it in items:
        it["au"] = _dot(it["attn"], it["xb"])
    for it in items:
        it["ku"] = _dot_tn(it["kd"], it["xb"])
    for it in items:
        it["lhs"] = jnp.concatenate([it["qd"] - it["au"][:, dk:], it["ku"][:, dk:]], axis=0).astype(BF16)

    states = [s_ref[hi] for hi in range(nh)]
    outs = [[None] * n_chunks for _ in range(nh)]
    for n in range(n_chunks):
        for hi in range(nh):
            it = items[hi * n_chunks + n]
            prod = _dot(it["lhs"], states[hi].astype(BF16))
            outs[hi][n] = prod[:c] + it["au"][:, :dk]
            states[hi] = states[hi] * it["gl"] - prod[c:] + it["ku"][:, :dk]
    for hi in range(nh):
        s_ref[hi] = states[hi]

    nw = nw_ref[...]
    for hi in range(nh):
        cols = slice(hi * dk, (hi + 1) * dk)
        o = jnp.concatenate(outs[hi], axis=0)
        o_ref[:, cols] = (_rms(o, nw) * _silu(z_ref[:, cols].astype(F32))).astype(o_ref.dtype)


def _gdn(main, side, a_log, dt_bias, norm_w, *, batch, seq, lb, nh):
    t = main.shape[0]
    ns = seq // lb
    hh = GDN_HEADS
    ng = hh // nh
    w = nh * GDN_D
    hp = jnp.zeros((8, LANES), F32)
    hp = hp.at[0, :hh].set(a_log).at[1, :hh].set(dt_bias)
    r = jnp.arange(lb)
    tril = ((r[:, None] >= r[None, :]) & (r[:, None] // GDN_CHUNK == r[None, :] // GDN_CHUNK)).astype(BF16)
    row = lambda b, h, s: b * ns + s
    return pl.pallas_call(
        functools.partial(_gdn_kernel, lb=lb, nh=nh),
        grid=(batch, ng, ns),
        in_specs=[
            pl.BlockSpec((lb, w), lambda b, h, s: (row(b, h, s), h)),
            pl.BlockSpec((lb, w), lambda b, h, s: (row(b, h, s), ng + h)),
            pl.BlockSpec((lb, w), lambda b, h, s: (row(b, h, s), 2 * ng + h)),
            pl.BlockSpec((lb, w), lambda b, h, s: (row(b, h, s), 3 * ng + h)),
            pl.BlockSpec((lb, LANES), lambda b, h, s: (row(b, h, s), 1)),
            pl.BlockSpec((8, LANES), lambda b, h, s: (0, 0)),
            pl.BlockSpec((1, GDN_D), lambda b, h, s: (0, 0)),
            pl.BlockSpec((lb, lb), lambda b, h, s: (0, 0)),
        ],
        out_specs=pl.BlockSpec((lb, w), lambda b, h, s: (row(b, h, s), h)),
        out_shape=jax.ShapeDtypeStruct((t, hh * GDN_D), BF16),
        scratch_shapes=[pltpu.VMEM((nh, GDN_D, GDN_D), F32)],
        compiler_params=_params(3), name="gdn",
    )(main, main, main, main, side, hp, norm_w.reshape(1, GDN_D), tril)


def _mla_up_kernel(cq_ref, ckv_ref, side_ref, tab_ref, qn_ref, kvn_ref, wq_ref, wk_ref, wvt_ref,
                   q_out, k_out, v_out):
    hh = MLA_HEADS
    tab = tab_ref[...]
    scale = (MLA_NOPE + MLA_ROPE) ** -0.5 * LOG2_E

    cq = _rms(cq_ref[...].astype(F32), qn_ref[...]).astype(BF16)
    ckv = _rms(ckv_ref[...].astype(F32), kvn_ref[...]).astype(BF16)
    q_all = _dot(cq, wq_ref[...])
    k_all = _dot(ckv, wk_ref[...])
    v_out[...] = _dot_nt(wvt_ref[...], ckv).astype(BF16)

    kr = side_ref[...] * tab
    kpe = (kr + pltpu.roll(kr, shift=MLA_ROPE, axis=1)).astype(BF16)
    rot = tab * scale
    for h in range(hh):
        q_out[:, h * 256:h * 256 + LANES] = (q_all[:, h * 256:h * 256 + LANES] * scale).astype(BF16)
        q_out[:, h * 256 + LANES:(h + 1) * 256] = (q_all[:, h * 256 + LANES:(h + 1) * 256] * rot).astype(BF16)
        k_out[:, h * 256:h * 256 + LANES] = k_all[:, h * LANES:(h + 1) * LANES].astype(BF16)
        k_out[:, h * 256 + LANES:(h + 1) * 256] = kpe


def _mla_up(main, side, tab, q_norm, kv_norm, wq, wk, wvt, *, bm):
    t = main.shape[0]
    r = MLA_RANK
    cq_blk = (GDN_HEADS * GDN_D * 4) // r
    return pl.pallas_call(
        _mla_up_kernel,
        grid=(t // bm,),
        in_specs=[
            pl.BlockSpec((bm, r), lambda i: (i, cq_blk)),
            pl.BlockSpec((bm, r), lambda i: (i, cq_blk + 1)),
            pl.BlockSpec((bm, LANES), lambda i: (i, 0)),
            pl.BlockSpec((bm, LANES), lambda i: (i, 0)),
            pl.BlockSpec((1, r), lambda i: (0, 0)),
            pl.BlockSpec((1, r), lambda i: (0, 0)),
            pl.BlockSpec(wq.shape, lambda i: (0, 0)),
            pl.BlockSpec(wk.shape, lambda i: (0, 0)),
            pl.BlockSpec(wvt.shape, lambda i: (0, 0)),
        ],
        out_specs=[pl.BlockSpec((bm, MLA_HEADS * 256), lambda i: (i, 0)),
                   pl.BlockSpec((bm, MLA_HEADS * 256), lambda i: (i, 0)),
                   pl.BlockSpec((MLA_HEADS * MLA_V, bm), lambda i: (0, i))],
        out_shape=[jax.ShapeDtypeStruct((t, MLA_HEADS * 256), BF16),
                   jax.ShapeDtypeStruct((t, MLA_HEADS * 256), BF16),
                   jax.ShapeDtypeStruct((MLA_HEADS * MLA_V, t), BF16)],
        compiler_params=_params(1), name="mla_up",
    )(main, main, side, tab, q_norm.reshape(1, r), kv_norm.reshape(1, r), wq, wk, wvt)


def _flash_kernel(q_ref, k_ref, vt_ref, o_ref, *, blk, nh):
    qi = pl.program_id(2)
    dq = 2 * LANES
    qs = [q_ref[:, h * dq:(h + 1) * dq] for h in range(nh)]

    def step(j, carry, masked):
        start = pl.multiple_of(j * blk, blk)
        ss = [_dot_nt(k_ref[pl.ds(start, blk), h * dq:(h + 1) * dq], qs[h]) for h in range(nh)]
        if masked:
            key = lax.broadcasted_iota(jnp.int32, ss[0].shape, 0)
            qry = lax.broadcasted_iota(jnp.int32, ss[0].shape, 1)
            ss = [jnp.where(key <= qry, s, -jnp.inf) for s in ss]
        m_new = [jnp.maximum(carry[h][0], jnp.max(ss[h], axis=0, keepdims=True)) for h in range(nh)]
        ps = [jnp.exp2(ss[h] - m_new[h]) for h in range(nh)]
        pv = [_dot(vt_ref[h * MLA_V:(h + 1) * MLA_V, pl.ds(start, blk)], ps[h].astype(BF16)) for h in range(nh)]
        out = []
        for h in range(nh):
            m, l, acc = carry[h]
            alpha = jnp.exp2(m - m_new[h])
            out.append((m_new[h], l * alpha + jnp.sum(ps[h], axis=0, keepdims=True), acc * alpha + pv[h]))
        return tuple(out)

    init = tuple((jnp.full((1, blk), -jnp.inf, F32), jnp.zeros((1, blk), F32),
                  jnp.zeros((MLA_V, blk), F32)) for _ in range(nh))
    carry = lax.fori_loop(0, qi, lambda j, cr: step(j, cr, False), init)
    carry = step(qi, carry, True)
    for h in range(nh):
        _, l, acc = carry[h]
        o_ref[:, h * MLA_V:(h + 1) * MLA_V] = (acc / l).T.astype(o_ref.dtype)


def _flash(qf, kf, vt, *, batch, seq, blk, nh):
    t = qf.shape[0]
    nq = seq // blk
    hh = MLA_HEADS
    return pl.pallas_call(
        functools.partial(_flash_kernel, blk=blk, nh=nh),
        grid=(batch, hh // nh, nq),
        in_specs=[
            pl.BlockSpec((blk, nh * 256), lambda b, h, i: (b * nq + i, h)),
            pl.BlockSpec((seq, nh * 256), lambda b, h, i: (b, h)),
            pl.BlockSpec((nh * MLA_V, seq), lambda b, h, i: (h, b)),
        ],
        out_specs=pl.BlockSpec((blk, nh * MLA_V), lambda b, h, i: (b * nq + i, h)),
        out_shape=jax.ShapeDtypeStruct((t, hh * MLA_V), BF16),
        compiler_params=_params(3), name="mla_flash",
    )(qf, kf, vt)


def _ret_kernel(lg_ref, q_ref, k_ref, v_ref, g_ref, nw_ref, o_ref,
                s_ref, dec_ref, xz_ref, *, c, nh):
    @pl.when(pl.program_id(2) == 0)
    def _():
        row = lax.broadcasted_iota(jnp.int32, (c, c), 0)
        col = lax.broadcasted_iota(jnp.int32, (c, c), 1)
        pos = lax.broadcasted_iota(jnp.int32, (c, LANES), 0).astype(F32)
        lane = lax.broadcasted_iota(jnp.int32, (c, LANES), 1)
        for h in range(nh):
            lg = lg_ref[h, :, 0:1]
            dec_ref[h] = jnp.exp(jnp.where(row >= col, (row - col).astype(F32) * lg, -jnp.inf))
            xi = jnp.exp((pos + 1.0) * lg)
            zeta = jnp.exp((c - 1.0 - pos) * lg)
            xz_ref[h] = jnp.where(lane == 0, xi, zeta)
        s_ref[...] = jnp.zeros_like(s_ref)

    hs = range(nh)
    qb = [q_ref[:, h * RET_DK:(h + 1) * RET_DK] for h in hs]
    kb = [k_ref[:, h * RET_DK:(h + 1) * RET_DK] for h in hs]
    v = [v_ref[:, h * RET_DV:(h + 1) * RET_DV] for h in hs]
    qk = [(_dot_nt(qb[h], kb[h]) * dec_ref[h]).astype(BF16) for h in hs]
    states = [s_ref[h] for h in hs]
    cross = [_dot((qb[h].astype(F32) * xz_ref[h, :, 0:1]).astype(BF16), states[h].astype(BF16)) for h in hs]
    inner = [_dot(qk[h], v[h]) for h in hs]
    upd = [_dot_tn((kb[h].astype(F32) * xz_ref[h, :, 1:2]).astype(BF16), v[h]) for h in hs]
    for h in hs:
        s_ref[h] = states[h] * xz_ref[h, c - 1:c, 0:1] + upd[h]
    for h in hs:
        o = inner[h] + cross[h]
        mu = jnp.mean(o, axis=-1, keepdims=True)
        xc = o - mu
        var = jnp.mean(xc * xc, axis=-1, keepdims=True)
        cols = slice(h * RET_DV, (h + 1) * RET_DV)
        y = xc * lax.rsqrt(var + NORM_EPS) * nw_ref[:, cols]
        o_ref[:, cols] = (g_ref[:, cols].astype(F32) * y).astype(o_ref.dtype)


def _retention(main, norm_w, *, batch, seq, c, nh):
    t = main.shape[0]
    nc = seq // c
    hh = RET_HEADS
    ng = hh // nh
    log_gamma = jnp.log1p(-jnp.power(2.0, -5.0 - jnp.arange(hh, dtype=F32)))
    lg = jnp.broadcast_to(log_gamma[:, None, None], (hh, 1, LANES))
    row = lambda b, h, s: b * nc + s
    kq = RET_DK
    v_blk0 = (2 * hh * kq) // (nh * RET_DV)
    return pl.pallas_call(
        functools.partial(_ret_kernel, c=c, nh=nh),
        grid=(batch, ng, nc),
        in_specs=[
            pl.BlockSpec((nh, 1, LANES), lambda b, h, s: (h, 0, 0)),
            pl.BlockSpec((c, nh * kq), lambda b, h, s: (row(b, h, s), h)),
            pl.BlockSpec((c, nh * kq), lambda b, h, s: (row(b, h, s), ng + h)),
            pl.BlockSpec((c, nh * RET_DV), lambda b, h, s: (row(b, h, s), v_blk0 + h)),
            pl.BlockSpec((c, nh * RET_DV), lambda b, h, s: (row(b, h, s), v_blk0 + ng + h)),
            pl.BlockSpec((1, nh * RET_DV), lambda b, h, s: (0, h)),
        ],
        out_specs=pl.BlockSpec((c, nh * RET_DV), lambda b, h, s: (row(b, h, s), h)),
        out_shape=jax.ShapeDtypeStruct((t, hh * RET_DV), BF16),
        scratch_shapes=[pltpu.VMEM((nh, kq, RET_DV), F32), pltpu.VMEM((nh, c, c), F32),
                        pltpu.VMEM((nh, c, LANES), F32)],
        compiler_params=_params(3), name="retention",
    )(lg, main, main, main, main, norm_w.reshape(1, -1))


def _rotate_half_cols(w, half):
    return jnp.concatenate([-w[..., half:], w[..., :half]], axis=-1)


def _rope_tab(positions, dim):
    inv_freq = ROPE_THETA ** (-jnp.arange(0, dim, 2, dtype=F32) / dim)
    ang = positions.astype(F32).reshape(-1, 1) * inv_freq
    return jnp.cos(ang), jnp.sin(ang)


def kernel(x, p, positions, l0_attn_norm, l0_w_in, l0_gdn_conv, l0_gdn_A_log, l0_gdn_dt_bias, l0_gdn_norm, l0_mla_q_norm, l0_mla_w_uq, l0_mla_kv_norm, l0_mla_w_ukv, l0_w_out, l0_ffn_norm, l0_ffn_w_up, l0_ffn_conv_w, l0_ffn_conv_b, l0_ffn_w_down, l0_ple_proj, l0_ple_gate_norm, l0_ple_gate, l1_attn_norm, l1_w_in, l1_ret_norm, l1_w_out, l1_ffn_norm, l1_ffn_w_up, l1_ffn_conv_w, l1_ffn_conv_b, l1_ffn_w_down, l1_ple_proj, l1_ple_gate_norm, l1_ple_gate, final_norm):
    batch, seq, d = x.shape
    t = batch * seq
    x2 = x.reshape(t, d)
    p2 = p.reshape(p.shape[0], t, p.shape[-1])

    gq = GDN_HEADS * GDN_D * 4
    n_ab = 2 * GDN_HEADS
    c0 = gq + n_ab
    w_main = jnp.concatenate([l0_w_in[:, :gq], l0_w_in[:, c0:c0 + 2 * MLA_RANK]], axis=1).astype(BF16)
    w_kr = l0_w_in[:, c0 + 2 * MLA_RANK:]
    w_side = jnp.concatenate(
        [w_kr, _rotate_half_cols(w_kr, MLA_ROPE // 2), l0_w_in[:, gq:c0],
         jnp.zeros((d, LANES - n_ab), F32)], axis=1).astype(BF16)

    wq = l0_mla_w_uq.reshape(MLA_RANK, MLA_HEADS, MLA_NOPE + MLA_ROPE)
    wq_pe = wq[:, :, MLA_NOPE:]
    wq = jnp.concatenate([wq[:, :, :MLA_NOPE], wq_pe, _rotate_half_cols(wq_pe, MLA_ROPE // 2)], axis=-1)
    wq = wq.reshape(MLA_RANK, MLA_HEADS * 256).astype(BF16)
    wkv = l0_mla_w_ukv.reshape(MLA_RANK, MLA_HEADS, MLA_NOPE + MLA_V)
    wk = wkv[:, :, :MLA_NOPE].reshape(MLA_RANK, -1).astype(BF16)
    wvt = wkv[:, :, MLA_NOPE:].reshape(MLA_RANK, -1).T.astype(BF16)

    cos_m, sin_m = _rope_tab(positions, MLA_ROPE)
    tab_m = jnp.concatenate([cos_m, cos_m, sin_m, sin_m], axis=1)
    cos_r, sin_r = _rope_tab(positions, RET_DK)
    tab_r = jnp.concatenate([cos_r, sin_r], axis=1)

    main0, side0 = _in_proj0(x2, l0_attn_norm, w_main, w_side, l0_gdn_conv, seq=seq, bm=1024, bn=1024)
    y_a = _gdn(main0, side0, l0_gdn_A_log, l0_gdn_dt_bias, l0_gdn_norm,
               batch=batch, seq=seq, lb=256, nh=8)
    qf, kf, vt = _mla_up(main0, side0, tab_m, l0_mla_q_norm, l0_mla_kv_norm, wq, wk, wvt, bm=512)
    y_b = _flash(qf, kf, vt, batch=batch, seq=seq, blk=512, nh=4)
    w_out0 = l0_w_out.astype(BF16)
    ha = GDN_HEADS * GDN_D
    h, w_up = _proj_res([(y_a, w_out0[:ha]), (y_b, w_out0[ha:])], x2, bm=1024, bn=1024, casts=[l0_ffn_w_up])
    act, w_down = _ffn_up(h, l0_ffn_norm, w_up, l0_ffn_conv_w, l0_ffn_conv_b, seq=seq, bm=1024, bn=512,
                          casts=[l0_ffn_w_down])
    h, w_gate = _proj_res([(act, w_down)], h, bm=1024, bn=512, casts=[l0_ple_gate])
    h, w_in1, w_out1 = _ple(h, p2[0], l0_ple_gate_norm, l0_ple_proj.astype(BF16), w_gate, bm=512, bn=512,
                            casts=[l1_w_in, l1_w_out])

    main1 = _in_proj1(h, l1_attn_norm, w_in1, tab_r, bm=1024, bn=1024)
    y = _retention(main1, l1_ret_norm, batch=batch, seq=seq, c=256, nh=8)
    h, w_up = _proj_res([(y, w_out1)], h, bm=1024, bn=512, casts=[l1_ffn_w_up])
    act, w_down = _ffn_up(h, l1_ffn_norm, w_up, l1_ffn_conv_w, l1_ffn_conv_b, seq=seq, bm=1024, bn=512,
                          casts=[l1_ffn_w_down])
    h, w_gate = _proj_res([(act, w_down)], h, bm=1024, bn=512, casts=[l1_ple_gate])
    (h,) = _ple(h, p2[1], l1_ple_gate_norm, l1_ple_proj.astype(BF16), w_gate, final_norm, bm=512, bn=512)
    return h.reshape(batch, seq, d)
```

```python
import functools

import jax
import jax.numpy as jnp
from jax import lax
from jax.experimental import pallas as pl
from jax.experimental.pallas import tpu as pltpu

F32 = jnp.float32
BF16 = jnp.bfloat16

NORM_EPS = 1e-6
ROPE_THETA = 10000.0

GDN_HEADS = 8
GDN_D = 128
GDN_CONV = 4
GDN_CHUNK = 64
MLA_HEADS = 8
MLA_RANK = 512
MLA_NOPE = 128
MLA_ROPE = 64
MLA_V = 128
RET_HEADS = 8
RET_DK = 256
RET_DV = 512
FFN_CONV = 3

LANES = 128
VMEM_LIMIT = 56 * 1024 * 1024
LOG2_E = 1.4426950408889634
CONV_HALO = 16


def _params(n_grid):
    return pltpu.CompilerParams(dimension_semantics=("arbitrary",) * n_grid,
                                vmem_limit_bytes=VMEM_LIMIT)


def _rms(x, g):
    return x * lax.rsqrt(jnp.mean(x * x, axis=-1, keepdims=True) + NORM_EPS) * g


def _dot(a, b):
    return jnp.dot(a, b, preferred_element_type=F32)


def _dot_nt(a, b):
    return lax.dot_general(a, b, (((1,), (1,)), ((), ())), preferred_element_type=F32)


def _dot_tn(a, b):
    return lax.dot_general(a, b, (((0,), (0,)), ((), ())), preferred_element_type=F32)


def _silu(x):
    return x * jax.nn.sigmoid(x)


def _call(kernel_fn, *, grid, in_specs, out_specs, out_shape, args, name, scratch_shapes=(), casts=()):
    n_in, n_out = len(in_specs), len(out_specs)
    n_steps = 1
    for g in grid:
        n_steps *= g

    def flat_step(*idx):
        step = idx[0]
        for g, i in zip(grid[1:], idx[1:]):
            step = step * g + i
        return step

    in_specs, out_specs, out_shape, args = list(in_specs), list(out_specs), list(out_shape), list(args)
    for w in casts:
        rows, cols = w.shape[0] // n_steps, w.shape[1]
        assert rows * n_steps == w.shape[0] and rows % 16 == 0, (w.shape, n_steps)
        spec = pl.BlockSpec((None, rows, cols), lambda *idx: (flat_step(*idx), 0, 0))
        in_specs.append(spec)
        out_specs.append(spec)
        out_shape.append(jax.ShapeDtypeStruct((n_steps, rows, cols), BF16))
        args.append(w.reshape(n_steps, rows, cols))

    def body(*refs):
        ins, extra_in = refs[:n_in], refs[n_in:n_in + len(casts)]
        outs = refs[n_in + len(casts):n_in + len(casts) + n_out]
        extra_out = refs[n_in + len(casts) + n_out:n_in + 2 * len(casts) + n_out]
        scratch = refs[n_in + 2 * len(casts) + n_out:]
        for src, dst in zip(extra_in, extra_out):
            dst[...] = src[...].astype(BF16)
        kernel_fn(*ins, *outs, *scratch)

    out = pl.pallas_call(
        body, grid=grid, in_specs=in_specs, out_specs=out_specs, out_shape=out_shape,
        scratch_shapes=list(scratch_shapes), compiler_params=_params(len(grid)), name=name,
    )(*args)
    return list(out[:n_out]) + [o.reshape(w.shape) for o, w in zip(out[n_out:], casts)]


def _in_proj0_kernel(x_ref, xh_ref, g_ref, w_ref, ws_ref, cw_ref, o_ref, os_ref, xn_ref, u_ref,
                     *, bm, blocks_per_seq, n_conv_blocks):
    i = pl.program_id(0)
    j = pl.program_id(1)

    @pl.when(j == 0)
    def _():
        g = g_ref[...]
        halo = _rms(xh_ref[...], g)
        halo = jnp.where(i % blocks_per_seq == 0, 0.0, halo)
        xn_ref[0:CONV_HALO, :] = halo.astype(BF16)
        xn = _rms(x_ref[...], g).astype(BF16)
        xn_ref[CONV_HALO:, :] = xn
        os_ref[...] = _dot(xn, ws_ref[...])

    def conv_silu():
        u_ref[...] = _dot(xn_ref[...], w_ref[...])
        cw = cw_ref[...]
        y = cw[GDN_CONV - 1:GDN_CONV] * u_ref[pl.ds(CONV_HALO, bm), :]
        for s in range(1, GDN_CONV):
            y = y + cw[GDN_CONV - 1 - s:GDN_CONV - s] * u_ref[pl.ds(CONV_HALO - s, bm), :]
        return _silu(y)

    @pl.when(j < n_conv_blocks - 1)
    def _():
        y = conv_silu()
        scale = jnp.where(j == 0, GDN_D ** -0.5, 1.0)
        for h in range(y.shape[1] // GDN_D):
            cols = slice(h * GDN_D, (h + 1) * GDN_D)
            yh = y[:, cols]
            inv = lax.rsqrt(jnp.sum(yh * yh, axis=-1, keepdims=True) + NORM_EPS) * scale
            o_ref[:, cols] = (yh * inv).astype(o_ref.dtype)

    @pl.when(j == n_conv_blocks - 1)
    def _():
        o_ref[...] = conv_silu().astype(o_ref.dtype)

    @pl.when(j >= n_conv_blocks)
    def _():
        o_ref[...] = _dot(xn_ref[CONV_HALO:, :], w_ref[...]).astype(o_ref.dtype)


def _in_proj0(x, g, w, w_side, conv_w, *, seq, bm, bn):
    t, d = x.shape
    n = w.shape[1]
    ns = w_side.shape[1]
    n_conv_blocks = conv_w.shape[1] // bn
    hb = bm // CONV_HALO
    kern = functools.partial(_in_proj0_kernel, bm=bm, blocks_per_seq=seq // bm, n_conv_blocks=n_conv_blocks)
    return pl.pallas_call(
        kern,
        grid=(t // bm, n // bn),
        in_specs=[
            pl.BlockSpec((bm, d), lambda i, j: (i, 0)),
            pl.BlockSpec((CONV_HALO, d), lambda i, j: (jnp.maximum(i * hb - 1, 0), 0)),
            pl.BlockSpec((1, d), lambda i, j: (0, 0)),
            pl.BlockSpec((d, bn), lambda i, j: (0, j)),
            pl.BlockSpec((d, ns), lambda i, j: (0, 0)),
            pl.BlockSpec((GDN_CONV, bn), lambda i, j: (0, jnp.minimum(j, n_conv_blocks - 1))),
        ],
        out_specs=[pl.BlockSpec((bm, bn), lambda i, j: (i, j)),
                   pl.BlockSpec((bm, ns), lambda i, j: (i, 0))],
        out_shape=[jax.ShapeDtypeStruct((t, n), BF16), jax.ShapeDtypeStruct((t, ns), F32)],
        scratch_shapes=[pltpu.VMEM((bm + CONV_HALO, d), BF16), pltpu.VMEM((bm + CONV_HALO, bn), F32)],
        compiler_params=_params(2), name="in_proj0",
    )(x, x, g.reshape(1, d), w, w_side, conv_w)


def _in_proj1_kernel(x_ref, g_ref, w_ref, cs_ref, o_ref, xn_ref, *, n_q, n_k, n_v):
    j = pl.program_id(1)
    half = RET_DK // 2

    @pl.when(j == 0)
    def _():
        xn_ref[...] = _rms(x_ref[...], g_ref[...]).astype(BF16)

    def proj():
        return _dot(xn_ref[...], w_ref[...])

    def rope_store(scale):
        y = proj()
        cos = cs_ref[:, :half]
        sin = cs_ref[:, half:]
        for h in range(y.shape[1] // RET_DK):
            x1 = y[:, h * RET_DK:h * RET_DK + half]
            x2 = y[:, h * RET_DK + half:(h + 1) * RET_DK]
            o_ref[:, h * RET_DK:h * RET_DK + half] = ((x1 * cos - x2 * sin) * scale).astype(o_ref.dtype)
            o_ref[:, h * RET_DK + half:(h + 1) * RET_DK] = ((x2 * cos + x1 * sin) * scale).astype(o_ref.dtype)

    @pl.when(j < n_q)
    def _():
        rope_store(1.0)

    @pl.when((j >= n_q) & (j < n_q + n_k))
    def _():
        rope_store(RET_DK ** -0.5)

    @pl.when((j >= n_q + n_k) & (j < n_q + n_k + n_v))
    def _():
        o_ref[...] = proj().astype(o_ref.dtype)

    @pl.when(j >= n_q + n_k + n_v)
    def _():
        o_ref[...] = _silu(proj()).astype(o_ref.dtype)


def _in_proj1(x, g, w, tab, *, bm, bn):
    t, d = x.shape
    n = w.shape[1]
    n_q = RET_HEADS * RET_DK // bn
    n_v = RET_HEADS * RET_DV // bn
    return pl.pallas_call(
        functools.partial(_in_proj1_kernel, n_q=n_q, n_k=n_q, n_v=n_v),
        grid=(t // bm, n // bn),
        in_specs=[pl.BlockSpec((bm, d), lambda i, j: (i, 0)),
                  pl.BlockSpec((1, d), lambda i, j: (0, 0)),
                  pl.BlockSpec((d, bn), lambda i, j: (0, j)),
                  pl.BlockSpec((bm, RET_DK), lambda i, j: (i, 0))],
        out_specs=pl.BlockSpec((bm, bn), lambda i, j: (i, j)),
        out_shape=jax.ShapeDtypeStruct((t, n), BF16),
        scratch_shapes=[pltpu.VMEM((bm, d), BF16)],
        compiler_params=_params(2), name="in_proj1",
    )(x, g.reshape(1, d), w, tab)


def _proj_res_kernel(*refs, n_pairs):
    res_ref, o_ref = refs[2 * n_pairs], refs[2 * n_pairs + 1]
    acc = res_ref[...]
    for p in range(n_pairs):
        acc = acc + _dot(refs[2 * p][...], refs[2 * p + 1][...])
    o_ref[...] = acc


def _proj_res(pairs, res, *, bm, bn, casts=()):
    t, n = res.shape
    in_specs, args = [], []
    for y, w in pairs:
        kk = y.shape[1]
        in_specs += [pl.BlockSpec((bm, kk), lambda i, j: (i, 0)),
                     pl.BlockSpec((kk, bn), lambda i, j: (0, j))]
        args += [y, w]
    in_specs.append(pl.BlockSpec((bm, bn), lambda i, j: (i, j)))
    args.append(res)
    return _call(
        functools.partial(_proj_res_kernel, n_pairs=len(pairs)),
        grid=(t // bm, n // bn),
        in_specs=in_specs,
        out_specs=[pl.BlockSpec((bm, bn), lambda i, j: (i, j))],
        out_shape=[jax.ShapeDtypeStruct((t, n), F32)],
        args=args, name="proj_res", casts=casts)


def _ffn_up_kernel(h_ref, hh_ref, g_ref, wg_ref, wu_ref, cg_ref, cu_ref, bg_ref, bu_ref,
                   o_ref, xn_ref, ug_ref, uu_ref, *, bm, blocks_per_seq):
    i = pl.program_id(0)

    @pl.when(pl.program_id(1) == 0)
    def _():
        g = g_ref[...]
        halo = _rms(hh_ref[...], g)
        halo = jnp.where(i % blocks_per_seq == 0, 0.0, halo)
        xn_ref[0:CONV_HALO, :] = halo.astype(BF16)
        xn_ref[CONV_HALO:, :] = _rms(h_ref[...], g).astype(BF16)

    xn = xn_ref[...]
    ug_ref[...] = _dot(xn, wg_ref[...])
    uu_ref[...] = _dot(xn, wu_ref[...])

    def conv(u_ref, c_ref, b_ref):
        cw = c_ref[...]
        y = b_ref[...] + cw[FFN_CONV - 1:FFN_CONV] * u_ref[pl.ds(CONV_HALO, bm), :]
        for s in range(1, FFN_CONV):
            y = y + cw[FFN_CONV - 1 - s:FFN_CONV - s] * u_ref[pl.ds(CONV_HALO - s, bm), :]
        return y

    gate = conv(ug_ref, cg_ref, bg_ref)
    up = conv(uu_ref, cu_ref, bu_ref)
    o_ref[...] = (_silu(gate) * up).astype(o_ref.dtype)


def _ffn_up(h, g, w_up, conv_w, conv_b, *, seq, bm, bn, casts=()):
    t, d = h.shape
    dff = w_up.shape[1] // 2
    nj = dff // bn
    hb = bm // CONV_HALO
    kern = functools.partial(_ffn_up_kernel, bm=bm, blocks_per_seq=seq // bm)
    return _call(
        kern,
        grid=(t // bm, nj),
        in_specs=[
            pl.BlockSpec((bm, d), lambda i, j: (i, 0)),
            pl.BlockSpec((CONV_HALO, d), lambda i, j: (jnp.maximum(i * hb - 1, 0), 0)),
            pl.BlockSpec((1, d), lambda i, j: (0, 0)),
            pl.BlockSpec((d, bn), lambda i, j: (0, j)),
            pl.BlockSpec((d, bn), lambda i, j: (0, j + nj)),
            pl.BlockSpec((FFN_CONV, bn), lambda i, j: (0, j)),
            pl.BlockSpec((FFN_CONV, bn), lambda i, j: (0, j + nj)),
            pl.BlockSpec((1, bn), lambda i, j: (0, j)),
            pl.BlockSpec((1, bn), lambda i, j: (0, j + nj)),
        ],
        out_specs=[pl.BlockSpec((bm, bn), lambda i, j: (i, j))],
        out_shape=[jax.ShapeDtypeStruct((t, dff), BF16)],
        scratch_shapes=[pltpu.VMEM((bm + CONV_HALO, d), BF16),
                        pltpu.VMEM((bm + CONV_HALO, bn), F32),
                        pltpu.VMEM((bm + CONV_HALO, bn), F32)],
        args=(h, h, g.reshape(1, d), w_up, w_up, conv_w, conv_w,
              conv_b.reshape(1, -1), conv_b.reshape(1, -1)),
        name="ffn_up", casts=casts)


def _ple_kernel(h_ref, p_ref, gn_ref, wp_ref, wg_ref, *rest, bn, final):
    if final:
        fn_ref, o_ref = rest
    else:
        (o_ref,) = rest
    d = h_ref.shape[1]
    hn = _rms(h_ref[...], gn_ref[...]).astype(BF16)
    pb = p_ref[...].astype(BF16)
    ssq = None
    for c in range(d // bn):
        cols = slice(c * bn, (c + 1) * bn)
        gate = _dot(hn, wg_ref[:, cols])
        proj = _dot(pb, wp_ref[:, cols])
        out = h_ref[:, cols] + proj * jax.nn.sigmoid(gate)
        o_ref[:, cols] = out
        if final:
            part = jnp.sum(out * out, axis=-1, keepdims=True)
            ssq = part if ssq is None else ssq + part
    if final:
        scale = lax.rsqrt(ssq / d + NORM_EPS)
        o_ref[...] = o_ref[...] * scale * fn_ref[...]


def _ple(h, p, layer, gate_norm, w_proj, w_gate, final_norm=None, *, bm, bn, casts=()):
    t, d = h.shape
    dp = p.shape[-1]
    final = final_norm is not None
    in_specs = [pl.BlockSpec((bm, d), lambda i: (i, 0)),
                pl.BlockSpec((None, bm, dp), lambda i: (layer, i, 0)),
                pl.BlockSpec((1, d), lambda i: (0, 0)),
                pl.BlockSpec((dp, d), lambda i: (0, 0)),
                pl.BlockSpec((d, d), lambda i: (0, 0))]
    args = [h, p, gate_norm.reshape(1, d), w_proj, w_gate]
    if final:
        in_specs.append(pl.BlockSpec((1, d), lambda i: (0, 0)))
        args.append(final_norm.reshape(1, d))
    return _call(
        functools.partial(_ple_kernel, bn=bn, final=final),
        grid=(t // bm,),
        in_specs=in_specs,
        out_specs=[pl.BlockSpec((bm, d), lambda i: (i, 0))],
        out_shape=[jax.ShapeDtypeStruct((t, d), F32)],
        args=args, name="ple", casts=casts)


def _split3(x):
    hi = x.astype(BF16).astype(F32)
    r1 = x - hi
    mid = r1.astype(BF16).astype(F32)
    lo = (r1 - mid).astype(BF16).astype(F32)
    return hi, mid, lo


def _lane_pick(x, lane, idx):
    return jnp.sum(jnp.where(lane == idx, x, 0.0), axis=-1, keepdims=True)


def _gdn_kernel(q_ref, k_ref, v_ref, z_ref, ab_ref, hp_ref, nw_ref, tril_ref, o_ref, s_ref, *, lb, nh):
    c = GDN_CHUNK
    n_chunks = lb // c
    dk = GDN_D

    @pl.when(pl.program_id(2) == 0)
    def _():
        s_ref[...] = jnp.zeros_like(s_ref)

    ab = ab_ref[...]
    lane = lax.broadcasted_iota(jnp.int32, ab.shape, 1)
    hp = hp_ref[...]
    xa = ab + hp[1:2]
    softplus = jnp.maximum(xa, 0.0) + jnp.log1p(jnp.exp(-jnp.abs(xa)))
    g_lanes = -jnp.exp(hp[0:1]) * softplus
    beta_lanes = jax.nn.sigmoid(ab)

    def pieces_to_lanes(x, first, sign):
        hi, mid, lo = _split3(x)
        out = jnp.where(lane == first, sign * hi, 0.0)
        out = jnp.where(lane == first + 1, sign * mid, out)
        return jnp.where(lane == first + 2, sign * lo, out)

    row2 = lax.broadcasted_iota(jnp.int32, (c, 2 * c), 0)
    lane2 = lax.broadcasted_iota(jnp.int32, (c, 2 * c), 1)
    col2 = lane2 & (c - 1)
    incl2 = row2 >= col2
    strict2 = row2 > col2
    left = lane2 < c
    eye_left = jnp.where(left & (row2 == lane2), 1.0, 0.0)
    zeros_k = jnp.zeros((c, 2 * c), BF16)
    zeros_x = jnp.zeros((c, 2 * dk), BF16)

    items = []
    for hi in range(nh):
        head = pl.program_id(1) * nh + hi
        cols = slice(hi * dk, (hi + 1) * dk)
        qb = q_ref[:, cols]
        kb16 = k_ref[:, cols]
        q = qb.astype(F32)
        k = kb16.astype(F32)
        v = v_ref[:, cols].astype(F32)
        g_col = _lane_pick(g_lanes, lane, head)
        beta = _lane_pick(beta_lanes, lane, head + GDN_HEADS)
        gm = _dot(tril_ref[...], pieces_to_lanes(g_col, 0, 1.0).astype(BF16))
        big_g = jnp.sum(gm, axis=-1, keepdims=True)
        e_g = jnp.exp(big_g)
        am = jnp.where(lane < 3, pieces_to_lanes(big_g, 0, 1.0), jnp.where(lane < 6, 1.0, 0.0)).astype(BF16)
        bmat = jnp.where(lane < 3, 1.0, pieces_to_lanes(big_g, 3, -1.0)).astype(BF16)
        kbeta = k * beta
        x0 = jnp.concatenate([v * beta, kbeta * e_g], axis=-1).astype(BF16)
        qd = q * e_g
        kbeta16 = kbeta.astype(BF16)
        for n in range(n_chunks):
            r = slice(n * c, (n + 1) * c)
            g_last = big_g[(n + 1) * c - 1:(n + 1) * c, :]
            items.append(dict(
                am=am[r], bm2=jnp.concatenate([bmat[r], bmat[r]], axis=0),
                kb=kbeta16[r], k=kb16[r], k2=jnp.concatenate([kb16[r], kb16[r]], axis=0), q=qb[r],
                x0=jnp.concatenate([x0[r], zeros_x], axis=0), qd=qd[r],
                kd=(k[r] * jnp.exp(g_last - big_g[r])).astype(BF16), gl=jnp.exp(g_last)))

    for it in items:
        it["gamma2"] = jnp.exp(jnp.where(incl2, _dot_nt(it["am"], it["bm2"]), -jnp.inf))
    for it in items:
        it["tp"] = jnp.where(left, eye_left, jnp.where(strict2, -_dot_nt(it["kb"], it["k2"]) * it["gamma2"], 0.0))
    for it in items:
        it["attn"] = (_dot_nt(it["q"], it["k"]) * it["gamma2"][:, :c]).astype(BF16)

    for j in range(6):
        for it in items:
            it["tpb"] = it["tp"].astype(BF16)
        for it in items:
            upd = _dot(it["tpb"], jnp.concatenate([zeros_k, it["tpb"]], axis=0))
            it["tp"] = jnp.where(left, it["tp"] + upd, upd)
    for it in items:
        it["tpb"] = it["tp"].astype(BF16)
    for it in items:
        it["xb"] = _dot(it["tpb"], it["x0"]).astype(BF16)

    for it in items:
        it["au"] = _dot(it["attn"], it["xb"])
    for it in items:
        it["ku"] = _dot_tn(it["kd"], it["xb"])
    for it in items:
        it["lhs"] = jnp.concatenate([it["qd"] - it["au"][:, dk:], it["ku"][:, dk:]], axis=0).astype(BF16)

    states = [s_ref[hi] for hi in range(nh)]
    outs = [[None] * n_chunks for _ in range(nh)]
    for n in range(n_chunks):
        for hi in range(nh):
            it = items[hi * n_chunks + n]
            prod = _dot(it["lhs"], states[hi].astype(BF16))
            outs[hi][n] = prod[:c] + it["au"][:, :dk]
            states[hi] = states[hi] * it["gl"] - prod[c:] + it["ku"][:, :dk]
    for hi in range(nh):
        s_ref[hi] = states[hi]

    nw = nw_ref[...]
    for hi in range(nh):
        cols = slice(hi * dk, (hi + 1) * dk)
        o = jnp.concatenate(outs[hi], axis=0)
        o_ref[:, cols] = (_rms(o, nw) * _silu(z_ref[:, cols].astype(F32))).astype(o_ref.dtype)


def _gdn(main, side, a_log, dt_bias, norm_w, *, batch, seq, lb, nh):
    t = main.shape[0]
    ns = seq // lb
    hh = GDN_HEADS
    ng = hh // nh
    w = nh * GDN_D
    hp = jnp.zeros((8, LANES), F32)
    hp = hp.at[0, :hh].set(a_log).at[1, :hh].set(dt_bias)
    r = jnp.arange(lb)
    tril = ((r[:, None] >= r[None, :]) & (r[:, None] // GDN_CHUNK == r[None, :] // GDN_CHUNK)).astype(BF16)
    row = lambda b, h, s: b * ns + s
    return pl.pallas_call(
        functools.partial(_gdn_kernel, lb=lb, nh=nh),
        grid=(batch, ng, ns),
        in_specs=[
            pl.BlockSpec((lb, w), lambda b, h, s: (row(b, h, s), h)),
            pl.BlockSpec((lb, w), lambda b, h, s: (row(b, h, s), ng + h)),
            pl.BlockSpec((lb, w), lambda b, h, s: (row(b, h, s), 2 * ng + h)),
            pl.BlockSpec((lb, w), lambda b, h, s: (row(b, h, s), 3 * ng + h)),
            pl.BlockSpec((lb, LANES), lambda b, h, s: (row(b, h, s), 1)),
            pl.BlockSpec((8, LANES), lambda b, h, s: (0, 0)),
            pl.BlockSpec((1, GDN_D), lambda b, h, s: (0, 0)),
            pl.BlockSpec((lb, lb), lambda b, h, s: (0, 0)),
        ],
        out_specs=pl.BlockSpec((lb, w), lambda b, h, s: (row(b, h, s), h)),
        out_shape=jax.ShapeDtypeStruct((t, hh * GDN_D), BF16),
        scratch_shapes=[pltpu.VMEM((nh, GDN_D, GDN_D), F32)],
        compiler_params=_params(3), name="gdn",
    )(main, main, main, main, side, hp, norm_w.reshape(1, GDN_D), tril)


def _mla_up_kernel(cq_ref, ckv_ref, side_ref, tab_ref, qn_ref, kvn_ref, wq_ref, wk_ref, wvt_ref,
                   q_out, k_out, v_out):
    hh = MLA_HEADS
    tab = tab_ref[...]
    scale = (MLA_NOPE + MLA_ROPE) ** -0.5 * LOG2_E

    cq = _rms(cq_ref[...].astype(F32), qn_ref[...]).astype(BF16)
    ckv = _rms(ckv_ref[...].astype(F32), kvn_ref[...]).astype(BF16)
    q_all = _dot(cq, wq_ref[...])
    k_all = _dot(ckv, wk_ref[...])
    v_out[...] = _dot_nt(wvt_ref[...], ckv).astype(BF16)

    kr = side_ref[...] * tab
    kpe = (kr + pltpu.roll(kr, shift=MLA_ROPE, axis=1)).astype(BF16)
    rot = tab * scale
    for h in range(hh):
        q_out[:, h * 256:h * 256 + LANES] = (q_all[:, h * 256:h * 256 + LANES] * scale).astype(BF16)
        q_out[:, h * 256 + LANES:(h + 1) * 256] = (q_all[:, h * 256 + LANES:(h + 1) * 256] * rot).astype(BF16)
        k_out[:, h * 256:h * 256 + LANES] = k_all[:, h * LANES:(h + 1) * LANES].astype(BF16)
        k_out[:, h * 256 + LANES:(h + 1) * 256] = kpe


def _mla_up(main, side, tab, q_norm, kv_norm, wq, wk, wvt, *, bm):
    t = main.shape[0]
    r = MLA_RANK
    cq_blk = (GDN_HEADS * GDN_D * 4) // r
    return pl.pallas_call(
        _mla_up_kernel,
        grid=(t // bm,),
        in_specs=[
            pl.BlockSpec((bm, r), lambda i: (i, cq_blk)),
            pl.BlockSpec((bm, r), lambda i: (i, cq_blk + 1)),
            pl.BlockSpec((bm, LANES), lambda i: (i, 0)),
            pl.BlockSpec((bm, LANES), lambda i: (i, 0)),
            pl.BlockSpec((1, r), lambda i: (0, 0)),
            pl.BlockSpec((1, r), lambda i: (0, 0)),
            pl.BlockSpec(wq.shape, lambda i: (0, 0)),
            pl.BlockSpec(wk.shape, lambda i: (0, 0)),
            pl.BlockSpec(wvt.shape, lambda i: (0, 0)),
        ],
        out_specs=[pl.BlockSpec((bm, MLA_HEADS * 256), lambda i: (i, 0)),
                   pl.BlockSpec((bm, MLA_HEADS * 256), lambda i: (i, 0)),
                   pl.BlockSpec((MLA_HEADS * MLA_V, bm), lambda i: (0, i))],
        out_shape=[jax.ShapeDtypeStruct((t, MLA_HEADS * 256), BF16),
                   jax.ShapeDtypeStruct((t, MLA_HEADS * 256), BF16),
                   jax.ShapeDtypeStruct((MLA_HEADS * MLA_V, t), BF16)],
        compiler_params=_params(1), name="mla_up",
    )(main, main, side, tab, q_norm.reshape(1, r), kv_norm.reshape(1, r), wq, wk, wvt)


def _flash_kernel(q_ref, k_ref, vt_ref, o_ref, *, blk, nh):
    qi = pl.program_id(2)
    dq = 2 * LANES
    qs = [q_ref[:, h * dq:(h + 1) * dq] for h in range(nh)]

    def step(j, carry, masked):
        start = pl.multiple_of(j * blk, blk)
        ss = [_dot_nt(k_ref[pl.ds(start, blk), h * dq:(h + 1) * dq], qs[h]) for h in range(nh)]
        if masked:
            key = lax.broadcasted_iota(jnp.int32, ss[0].shape, 0)
            qry = lax.broadcasted_iota(jnp.int32, ss[0].shape, 1)
            ss = [jnp.where(key <= qry, s, -jnp.inf) for s in ss]
        m_new = [jnp.maximum(carry[h][0], jnp.max(ss[h], axis=0, keepdims=True)) for h in range(nh)]
        ps = [jnp.exp2(ss[h] - m_new[h]) for h in range(nh)]
        pv = [_dot(vt_ref[h * MLA_V:(h + 1) * MLA_V, pl.ds(start, blk)], ps[h].astype(BF16)) for h in range(nh)]
        out = []
        for h in range(nh):
            m, l, acc = carry[h]
            alpha = jnp.exp2(m - m_new[h])
            out.append((m_new[h], l * alpha + jnp.sum(ps[h], axis=0, keepdims=True), acc * alpha + pv[h]))
        return tuple(out)

    init = tuple((jnp.full((1, blk), -jnp.inf, F32), jnp.zeros((1, blk), F32),
                  jnp.zeros((MLA_V, blk), F32)) for _ in range(nh))
    carry = lax.fori_loop(0, qi, lambda j, cr: step(j, cr, False), init)
    carry = step(qi, carry, True)
    for h in range(nh):
        _, l, acc = carry[h]
        o_ref[:, h * MLA_V:(h + 1) * MLA_V] = (acc / l).T.astype(o_ref.dtype)


def _flash(qf, kf, vt, *, batch, seq, blk, nh, casts=()):
    t = qf.shape[0]
    nq = seq // blk
    hh = MLA_HEADS
    return _call(
        functools.partial(_flash_kernel, blk=blk, nh=nh),
        grid=(batch, hh // nh, nq),
        in_specs=[
            pl.BlockSpec((blk, nh * 256), lambda b, h, i: (b * nq + i, h)),
            pl.BlockSpec((seq, nh * 256), lambda b, h, i: (b, h)),
            pl.BlockSpec((nh * MLA_V, seq), lambda b, h, i: (h, b)),
        ],
        out_specs=[pl.BlockSpec((blk, nh * MLA_V), lambda b, h, i: (b * nq + i, h))],
        out_shape=[jax.ShapeDtypeStruct((t, hh * MLA_V), BF16)],
        args=(qf, kf, vt), name="mla_flash", casts=casts)


def _ret_kernel(lg_ref, q_ref, k_ref, v_ref, g_ref, nw_ref, o_ref,
                s_ref, dec_ref, xz_ref, *, c, nh):
    @pl.when(pl.program_id(2) == 0)
    def _():
        row = lax.broadcasted_iota(jnp.int32, (c, c), 0)
        col = lax.broadcasted_iota(jnp.int32, (c, c), 1)
        pos = lax.broadcasted_iota(jnp.int32, (c, LANES), 0).astype(F32)
        lane = lax.broadcasted_iota(jnp.int32, (c, LANES), 1)
        for h in range(nh):
            lg = lg_ref[h, :, 0:1]
            dec_ref[h] = jnp.exp(jnp.where(row >= col, (row - col).astype(F32) * lg, -jnp.inf))
            xi = jnp.exp((pos + 1.0) * lg)
            zeta = jnp.exp((c - 1.0 - pos) * lg)
            xz_ref[h] = jnp.where(lane == 0, xi, zeta)
        s_ref[...] = jnp.zeros_like(s_ref)

    hs = range(nh)
    qb = [q_ref[:, h * RET_DK:(h + 1) * RET_DK] for h in hs]
    kb = [k_ref[:, h * RET_DK:(h + 1) * RET_DK] for h in hs]
    v = [v_ref[:, h * RET_DV:(h + 1) * RET_DV] for h in hs]
    qk = [(_dot_nt(qb[h], kb[h]) * dec_ref[h]).astype(BF16) for h in hs]
    states = [s_ref[h] for h in hs]
    cross = [_dot((qb[h].astype(F32) * xz_ref[h, :, 0:1]).astype(BF16), states[h].astype(BF16)) for h in hs]
    inner = [_dot(qk[h], v[h]) for h in hs]
    upd = [_dot_tn((kb[h].astype(F32) * xz_ref[h, :, 1:2]).astype(BF16), v[h]) for h in hs]
    for h in hs:
        s_ref[h] = states[h] * xz_ref[h, c - 1:c, 0:1] + upd[h]
    for h in hs:
        o = inner[h] + cross[h]
        mu = jnp.mean(o, axis=-1, keepdims=True)
        xc = o - mu
        var = jnp.mean(xc * xc, axis=-1, keepdims=True)
        cols = slice(h * RET_DV, (h + 1) * RET_DV)
        y = xc * lax.rsqrt(var + NORM_EPS) * nw_ref[:, cols]
        o_ref[:, cols] = (g_ref[:, cols].astype(F32) * y).astype(o_ref.dtype)


def _retention(main, norm_w, *, batch, seq, c, nh, casts=()):
    t = main.shape[0]
    nc = seq // c
    hh = RET_HEADS
    ng = hh // nh
    log_gamma = jnp.log1p(-jnp.power(2.0, -5.0 - jnp.arange(hh, dtype=F32)))
    lg = jnp.broadcast_to(log_gamma[:, None, None], (hh, 1, LANES))
    row = lambda b, h, s: b * nc + s
    kq = RET_DK
    v_blk0 = (2 * hh * kq) // (nh * RET_DV)
    return _call(
        functools.partial(_ret_kernel, c=c, nh=nh),
        grid=(batch, ng, nc),
        in_specs=[
            pl.BlockSpec((nh, 1, LANES), lambda b, h, s: (h, 0, 0)),
            pl.BlockSpec((c, nh * kq), lambda b, h, s: (row(b, h, s), h)),
            pl.BlockSpec((c, nh * kq), lambda b, h, s: (row(b, h, s), ng + h)),
            pl.BlockSpec((c, nh * RET_DV), lambda b, h, s: (row(b, h, s), v_blk0 + h)),
            pl.BlockSpec((c, nh * RET_DV), lambda b, h, s: (row(b, h, s), v_blk0 + ng + h)),
            pl.BlockSpec((1, nh * RET_DV), lambda b, h, s: (0, h)),
        ],
        out_specs=[pl.BlockSpec((c, nh * RET_DV), lambda b, h, s: (row(b, h, s), h))],
        out_shape=[jax.ShapeDtypeStruct((t, hh * RET_DV), BF16)],
        scratch_shapes=[pltpu.VMEM((nh, kq, RET_DV), F32), pltpu.VMEM((nh, c, c), F32),
                        pltpu.VMEM((nh, c, LANES), F32)],
        args=(lg, main, main, main, main, norm_w.reshape(1, -1)), name="retention", casts=casts)


def _rotate_half_cols(w, half):
    return jnp.concatenate([-w[..., half:], w[..., :half]], axis=-1)


def _rope_tab(positions, dim):
    inv_freq = ROPE_THETA ** (-jnp.arange(0, dim, 2, dtype=F32) / dim)
    ang = positions.astype(F32).reshape(-1, 1) * inv_freq
    return jnp.cos(ang), jnp.sin(ang)


def kernel(x, p, positions, l0_attn_norm, l0_w_in, l0_gdn_conv, l0_gdn_A_log, l0_gdn_dt_bias, l0_gdn_norm, l0_mla_q_norm, l0_mla_w_uq, l0_mla_kv_norm, l0_mla_w_ukv, l0_w_out, l0_ffn_norm, l0_ffn_w_up, l0_ffn_conv_w, l0_ffn_conv_b, l0_ffn_w_down, l0_ple_proj, l0_ple_gate_norm, l0_ple_gate, l1_attn_norm, l1_w_in, l1_ret_norm, l1_w_out, l1_ffn_norm, l1_ffn_w_up, l1_ffn_conv_w, l1_ffn_conv_b, l1_ffn_w_down, l1_ple_proj, l1_ple_gate_norm, l1_ple_gate, final_norm):
    batch, seq, d = x.shape
    t = batch * seq
    x2 = x.reshape(t, d)
    p2 = p.reshape(p.shape[0], t, p.shape[-1])

    gq = GDN_HEADS * GDN_D * 4
    n_ab = 2 * GDN_HEADS
    c0 = gq + n_ab
    w_main = jnp.concatenate([l0_w_in[:, :gq], l0_w_in[:, c0:c0 + 2 * MLA_RANK]], axis=1).astype(BF16)
    w_kr = l0_w_in[:, c0 + 2 * MLA_RANK:]
    w_side = jnp.concatenate(
        [w_kr, _rotate_half_cols(w_kr, MLA_ROPE // 2), l0_w_in[:, gq:c0],
         jnp.zeros((d, LANES - n_ab), F32)], axis=1).astype(BF16)

    wq = l0_mla_w_uq.reshape(MLA_RANK, MLA_HEADS, MLA_NOPE + MLA_ROPE)
    wq_pe = wq[:, :, MLA_NOPE:]
    wq = jnp.concatenate([wq[:, :, :MLA_NOPE], wq_pe, _rotate_half_cols(wq_pe, MLA_ROPE // 2)], axis=-1)
    wq = wq.reshape(MLA_RANK, MLA_HEADS * 256).astype(BF16)
    wkv = l0_mla_w_ukv.reshape(MLA_RANK, MLA_HEADS, MLA_NOPE + MLA_V)
    wk = wkv[:, :, :MLA_NOPE].reshape(MLA_RANK, -1).astype(BF16)
    wvt = wkv[:, :, MLA_NOPE:].reshape(MLA_RANK, -1).T.astype(BF16)

    cos_r, sin_r = _rope_tab(positions, RET_DK)
    tab_r = jnp.concatenate([cos_r, sin_r], axis=1)
    stride = RET_DK // MLA_ROPE
    cos_m, sin_m = cos_r[:, ::stride], sin_r[:, ::stride]
    tab_m = jnp.concatenate([cos_m, cos_m, sin_m, sin_m], axis=1)

    main0, side0 = _in_proj0(x2, l0_attn_norm, w_main, w_side, l0_gdn_conv, seq=seq, bm=1024, bn=1024)
    y_a = _gdn(main0, side0, l0_gdn_A_log, l0_gdn_dt_bias, l0_gdn_norm,
               batch=batch, seq=seq, lb=256, nh=8)
    qf, kf, vt = _mla_up(main0, side0, tab_m, l0_mla_q_norm, l0_mla_kv_norm, wq, wk, wvt, bm=512)
    y_b, w_up = _flash(qf, kf, vt, batch=batch, seq=seq, blk=512, nh=4, casts=[l0_ffn_w_up])
    w_out0 = l0_w_out.astype(BF16)
    ha = GDN_HEADS * GDN_D
    (h,) = _proj_res([(y_a, w_out0[:ha]), (y_b, w_out0[ha:])], x2, bm=1024, bn=1024)
    act, w_down = _ffn_up(h, l0_ffn_norm, w_up, l0_ffn_conv_w, l0_ffn_conv_b, seq=seq, bm=1024, bn=512,
                          casts=[l0_ffn_w_down])
    h, w_gate = _proj_res([(act, w_down)], h, bm=1024, bn=512, casts=[l0_ple_gate])
    h, w_in1, w_out1 = _ple(h, p2, 0, l0_ple_gate_norm, l0_ple_proj.astype(BF16), w_gate, bm=512, bn=512,
                            casts=[l1_w_in, l1_w_out])

    main1 = _in_proj1(h, l1_attn_norm, w_in1, tab_r, bm=1024, bn=1024)
    y, w_up = _retention(main1, l1_ret_norm, batch=batch, seq=seq, c=256, nh=8, casts=[l1_ffn_w_up])
    (h,) = _proj_res([(y, w_out1)], h, bm=1024, bn=1024)
    act, w_down = _ffn_up(h, l1_ffn_norm, w_up, l1_ffn_conv_w, l1_ffn_conv_b, seq=seq, bm=1024, bn=512,
                          casts=[l1_ffn_w_down])
    h, w_gate = _proj_res([(act, w_down)], h, bm=1024, bn=512, casts=[l1_ple_gate])
    (h,) = _ple(h, p2, 1, l1_ple_gate_norm, l1_ple_proj.astype(BF16), w_gate, final_norm, bm=512, bn=512)
    return h.reshape(batch, seq, d)
```

```python
import functools

import jax
import jax.numpy as jnp
from jax import lax
from jax.experimental import pallas as pl
from jax.experimental.pallas import tpu as pltpu

F32 = jnp.float32
BF16 = jnp.bfloat16

NORM_EPS = 1e-6
ROPE_THETA = 10000.0

GDN_HEADS = 8
GDN_D = 128
GDN_CONV = 4
GDN_CHUNK = 64
MLA_HEADS = 8
MLA_RANK = 512
MLA_NOPE = 128
MLA_ROPE = 64
MLA_V = 128
RET_HEADS = 8
RET_DK = 256
RET_DV = 512
FFN_CONV = 3

LANES = 128
VMEM_LIMIT = 56 * 1024 * 1024
LOG2_E = 1.4426950408889634
CONV_HALO = 16


def _params(n_grid):
    return pltpu.CompilerParams(dimension_semantics=("arbitrary",) * n_grid,
                                vmem_limit_bytes=VMEM_LIMIT)


def _rms(x, g):
    return x * lax.rsqrt(jnp.mean(x * x, axis=-1, keepdims=True) + NORM_EPS) * g


def _dot(a, b):
    return jnp.dot(a, b, preferred_element_type=F32)


def _dot_nt(a, b):
    return lax.dot_general(a, b, (((1,), (1,)), ((), ())), preferred_element_type=F32)


def _dot_tn(a, b):
    return lax.dot_general(a, b, (((0,), (0,)), ((), ())), preferred_element_type=F32)


def _silu(x):
    return x * jax.nn.sigmoid(x)


def _call(kernel_fn, *, grid, in_specs, out_specs, out_shape, args, name, scratch_shapes=(), casts=()):
    n_in, n_out = len(in_specs), len(out_specs)
    n_steps = 1
    for g in grid:
        n_steps *= g

    def flat_step(*idx):
        step = idx[0]
        for g, i in zip(grid[1:], idx[1:]):
            step = step * g + i
        return step

    in_specs, out_specs, out_shape, args = list(in_specs), list(out_specs), list(out_shape), list(args)
    for w in casts:
        rows, cols = w.shape[0] // n_steps, w.shape[1]
        assert rows * n_steps == w.shape[0] and rows % 16 == 0, (w.shape, n_steps)
        spec = pl.BlockSpec((None, rows, cols), lambda *idx: (flat_step(*idx), 0, 0))
        in_specs.append(spec)
        out_specs.append(spec)
        out_shape.append(jax.ShapeDtypeStruct((n_steps, rows, cols), BF16))
        args.append(w.reshape(n_steps, rows, cols))

    def body(*refs):
        ins, extra_in = refs[:n_in], refs[n_in:n_in + len(casts)]
        outs = refs[n_in + len(casts):n_in + len(casts) + n_out]
        extra_out = refs[n_in + len(casts) + n_out:n_in + 2 * len(casts) + n_out]
        scratch = refs[n_in + 2 * len(casts) + n_out:]
        for src, dst in zip(extra_in, extra_out):
            dst[...] = src[...].astype(BF16)
        kernel_fn(*ins, *outs, *scratch)

    out = pl.pallas_call(
        body, grid=grid, in_specs=in_specs, out_specs=out_specs, out_shape=out_shape,
        scratch_shapes=list(scratch_shapes), compiler_params=_params(len(grid)), name=name,
    )(*args)
    return list(out[:n_out]) + [o.reshape(w.shape) for o, w in zip(out[n_out:], casts)]


def _in_proj0_kernel(x_ref, xh_ref, g_ref, w_ref, wt_ref, ws_ref, cw_ref, o_ref, os_ref, xn_ref, u_ref,
                     *, bm, blocks_per_seq, n_conv_blocks, n_main_blocks):
    i = pl.program_id(0)
    j = pl.program_id(1)

    @pl.when(j == 0)
    def _():
        g = g_ref[...]
        halo = _rms(xh_ref[...], g)
        halo = jnp.where(i % blocks_per_seq == 0, 0.0, halo)
        xn_ref[0:CONV_HALO, :] = halo.astype(BF16)
        xn = _rms(x_ref[...], g).astype(BF16)
        xn_ref[CONV_HALO:, :] = xn
        os_ref[...] = _dot(xn, ws_ref[...])

    def conv_silu():
        u_ref[...] = _dot(xn_ref[...], w_ref[...])
        cw = cw_ref[...]
        y = cw[GDN_CONV - 1:GDN_CONV] * u_ref[pl.ds(CONV_HALO, bm), :]
        for s in range(1, GDN_CONV):
            y = y + cw[GDN_CONV - 1 - s:GDN_CONV - s] * u_ref[pl.ds(CONV_HALO - s, bm), :]
        return _silu(y)

    @pl.when(j < n_conv_blocks - 1)
    def _():
        y = conv_silu()
        scale = jnp.where(j == 0, GDN_D ** -0.5, 1.0)
        for h in range(y.shape[1] // GDN_D):
            cols = slice(h * GDN_D, (h + 1) * GDN_D)
            yh = y[:, cols]
            inv = lax.rsqrt(jnp.sum(yh * yh, axis=-1, keepdims=True) + NORM_EPS) * scale
            o_ref[:, cols] = (yh * inv).astype(o_ref.dtype)

    @pl.when(j == n_conv_blocks - 1)
    def _():
        o_ref[...] = conv_silu().astype(o_ref.dtype)

    @pl.when((j >= n_conv_blocks) & (j < n_main_blocks))
    def _():
        o_ref[...] = _dot(xn_ref[CONV_HALO:, :], w_ref[...]).astype(o_ref.dtype)

    @pl.when(j >= n_main_blocks)
    def _():
        o_ref[...] = _dot(xn_ref[CONV_HALO:, :], wt_ref[...]).astype(o_ref.dtype)


def _in_proj0(x, g, w, w_tail, w_side, conv_w, *, seq, bm, bn):
    t, d = x.shape
    ns = w_side.shape[1]
    n_conv_blocks = conv_w.shape[1] // bn
    n_main_blocks = (GDN_HEADS * GDN_D * 4) // bn
    n_tail_blocks = w_tail.shape[1] // bn
    n = (n_main_blocks + n_tail_blocks) * bn
    hb = bm // CONV_HALO
    kern = functools.partial(_in_proj0_kernel, bm=bm, blocks_per_seq=seq // bm, n_conv_blocks=n_conv_blocks,
                             n_main_blocks=n_main_blocks)
    return pl.pallas_call(
        kern,
        grid=(t // bm, n_main_blocks + n_tail_blocks),
        in_specs=[
            pl.BlockSpec((bm, d), lambda i, j: (i, 0)),
            pl.BlockSpec((CONV_HALO, d), lambda i, j: (jnp.maximum(i * hb - 1, 0), 0)),
            pl.BlockSpec((1, d), lambda i, j: (0, 0)),
            pl.BlockSpec((d, bn), lambda i, j: (0, jnp.minimum(j, n_main_blocks - 1))),
            pl.BlockSpec((d, bn), lambda i, j: (0, jnp.maximum(j - n_main_blocks, 0))),
            pl.BlockSpec((d, ns), lambda i, j: (0, 0)),
            pl.BlockSpec((GDN_CONV, bn), lambda i, j: (0, jnp.minimum(j, n_conv_blocks - 1))),
        ],
        out_specs=[pl.BlockSpec((bm, bn), lambda i, j: (i, j)),
                   pl.BlockSpec((bm, ns), lambda i, j: (i, 0))],
        out_shape=[jax.ShapeDtypeStruct((t, n), BF16), jax.ShapeDtypeStruct((t, ns), F32)],
        scratch_shapes=[pltpu.VMEM((bm + CONV_HALO, d), BF16), pltpu.VMEM((bm + CONV_HALO, bn), F32)],
        compiler_params=_params(2), name="in_proj0",
    )(x, x, g.reshape(1, d), w, w_tail, w_side, conv_w)


def _in_proj1_kernel(x_ref, g_ref, w_ref, cos_ref, sin_ref, o_ref, xn_ref, *, n_q, n_k, n_v):
    j = pl.program_id(1)
    half = RET_DK // 2

    @pl.when(j == 0)
    def _():
        xn_ref[...] = _rms(x_ref[...], g_ref[...]).astype(BF16)

    def proj():
        return _dot(xn_ref[...], w_ref[...])

    def rope_store(scale):
        y = proj()
        cos = cos_ref[...]
        sin = sin_ref[...]
        for h in range(y.shape[1] // RET_DK):
            x1 = y[:, h * RET_DK:h * RET_DK + half]
            x2 = y[:, h * RET_DK + half:(h + 1) * RET_DK]
            o_ref[:, h * RET_DK:h * RET_DK + half] = ((x1 * cos - x2 * sin) * scale).astype(o_ref.dtype)
            o_ref[:, h * RET_DK + half:(h + 1) * RET_DK] = ((x2 * cos + x1 * sin) * scale).astype(o_ref.dtype)

    @pl.when(j < n_q)
    def _():
        rope_store(1.0)

    @pl.when((j >= n_q) & (j < n_q + n_k))
    def _():
        rope_store(RET_DK ** -0.5)

    @pl.when((j >= n_q + n_k) & (j < n_q + n_k + n_v))
    def _():
        o_ref[...] = proj().astype(o_ref.dtype)

    @pl.when(j >= n_q + n_k + n_v)
    def _():
        o_ref[...] = _silu(proj()).astype(o_ref.dtype)


def _in_proj1(x, g, w, cos, sin, *, bm, bn):
    t, d = x.shape
    n = w.shape[1]
    n_q = RET_HEADS * RET_DK // bn
    n_v = RET_HEADS * RET_DV // bn
    return pl.pallas_call(
        functools.partial(_in_proj1_kernel, n_q=n_q, n_k=n_q, n_v=n_v),
        grid=(t // bm, n // bn),
        in_specs=[pl.BlockSpec((bm, d), lambda i, j: (i, 0)),
                  pl.BlockSpec((1, d), lambda i, j: (0, 0)),
                  pl.BlockSpec((d, bn), lambda i, j: (0, j)),
                  pl.BlockSpec((bm, RET_DK // 2), lambda i, j: (i, 0)),
                  pl.BlockSpec((bm, RET_DK // 2), lambda i, j: (i, 0))],
        out_specs=pl.BlockSpec((bm, bn), lambda i, j: (i, j)),
        out_shape=jax.ShapeDtypeStruct((t, n), BF16),
        scratch_shapes=[pltpu.VMEM((bm, d), BF16)],
        compiler_params=_params(2), name="in_proj1",
    )(x, g.reshape(1, d), w, cos, sin)


def _proj_res_kernel(*refs, n_pairs):
    res_ref, o_ref = refs[2 * n_pairs], refs[2 * n_pairs + 1]
    acc = res_ref[...]
    for p in range(n_pairs):
        acc = acc + _dot(refs[2 * p][...], refs[2 * p + 1][...])
    o_ref[...] = acc


def _proj_res(pairs, res, *, bm, bn, casts=()):
    t, n = res.shape
    in_specs, args = [], []
    for y, w in pairs:
        kk = y.shape[1]
        in_specs += [pl.BlockSpec((bm, kk), lambda i, j: (i, 0)),
                     pl.BlockSpec((kk, bn), lambda i, j: (0, j))]
        args += [y, w]
    in_specs.append(pl.BlockSpec((bm, bn), lambda i, j: (i, j)))
    args.append(res)
    return _call(
        functools.partial(_proj_res_kernel, n_pairs=len(pairs)),
        grid=(t // bm, n // bn),
        in_specs=in_specs,
        out_specs=[pl.BlockSpec((bm, bn), lambda i, j: (i, j))],
        out_shape=[jax.ShapeDtypeStruct((t, n), F32)],
        args=args, name="proj_res", casts=casts)


def _ffn_up_kernel(h_ref, hh_ref, g_ref, wg_ref, wu_ref, cg_ref, cu_ref, bg_ref, bu_ref,
                   o_ref, xn_ref, ug_ref, uu_ref, *, bm, blocks_per_seq):
    i = pl.program_id(0)

    @pl.when(pl.program_id(1) == 0)
    def _():
        g = g_ref[...]
        halo = _rms(hh_ref[...], g)
        halo = jnp.where(i % blocks_per_seq == 0, 0.0, halo)
        xn_ref[0:CONV_HALO, :] = halo.astype(BF16)
        xn_ref[CONV_HALO:, :] = _rms(h_ref[...], g).astype(BF16)

    xn = xn_ref[...]
    ug_ref[...] = _dot(xn, wg_ref[...])
    uu_ref[...] = _dot(xn, wu_ref[...])

    def conv(u_ref, c_ref, b_ref):
        cw = c_ref[...]
        y = b_ref[...] + cw[FFN_CONV - 1:FFN_CONV] * u_ref[pl.ds(CONV_HALO, bm), :]
        for s in range(1, FFN_CONV):
            y = y + cw[FFN_CONV - 1 - s:FFN_CONV - s] * u_ref[pl.ds(CONV_HALO - s, bm), :]
        return y

    gate = conv(ug_ref, cg_ref, bg_ref)
    up = conv(uu_ref, cu_ref, bu_ref)
    o_ref[...] = (_silu(gate) * up).astype(o_ref.dtype)


def _ffn_up(h, g, w_up, conv_w, conv_b, *, seq, bm, bn, casts=()):
    t, d = h.shape
    dff = w_up.shape[1] // 2
    nj = dff // bn
    hb = bm // CONV_HALO
    kern = functools.partial(_ffn_up_kernel, bm=bm, blocks_per_seq=seq // bm)
    return _call(
        kern,
        grid=(t // bm, nj),
        in_specs=[
            pl.BlockSpec((bm, d), lambda i, j: (i, 0)),
            pl.BlockSpec((CONV_HALO, d), lambda i, j: (jnp.maximum(i * hb - 1, 0), 0)),
            pl.BlockSpec((1, d), lambda i, j: (0, 0)),
            pl.BlockSpec((d, bn), lambda i, j: (0, j)),
            pl.BlockSpec((d, bn), lambda i, j: (0, j + nj)),
            pl.BlockSpec((FFN_CONV, bn), lambda i, j: (0, j)),
            pl.BlockSpec((FFN_CONV, bn), lambda i, j: (0, j + nj)),
            pl.BlockSpec((1, bn), lambda i, j: (0, j)),
            pl.BlockSpec((1, bn), lambda i, j: (0, j + nj)),
        ],
        out_specs=[pl.BlockSpec((bm, bn), lambda i, j: (i, j))],
        out_shape=[jax.ShapeDtypeStruct((t, dff), BF16)],
        scratch_shapes=[pltpu.VMEM((bm + CONV_HALO, d), BF16),
                        pltpu.VMEM((bm + CONV_HALO, bn), F32),
                        pltpu.VMEM((bm + CONV_HALO, bn), F32)],
        args=(h, h, g.reshape(1, d), w_up, w_up, conv_w, conv_w,
              conv_b.reshape(1, -1), conv_b.reshape(1, -1)),
        name="ffn_up", casts=casts)


def _ple_kernel(h_ref, p_ref, gn_ref, wp_ref, wg_ref, *rest, bn, final):
    if final:
        fn_ref, o_ref = rest
    else:
        (o_ref,) = rest
    d = h_ref.shape[1]
    hn = _rms(h_ref[...], gn_ref[...]).astype(BF16)
    pb = p_ref[...].astype(BF16)
    ssq = None
    for c in range(d // bn):
        cols = slice(c * bn, (c + 1) * bn)
        gate = _dot(hn, wg_ref[:, cols])
        proj = _dot(pb, wp_ref[:, cols])
        out = h_ref[:, cols] + proj * jax.nn.sigmoid(gate)
        o_ref[:, cols] = out
        if final:
            part = jnp.sum(out * out, axis=-1, keepdims=True)
            ssq = part if ssq is None else ssq + part
    if final:
        scale = lax.rsqrt(ssq / d + NORM_EPS)
        o_ref[...] = o_ref[...] * scale * fn_ref[...]


def _ple(h, p, layer, gate_norm, w_proj, w_gate, final_norm=None, *, bm, bn, casts=()):
    t, d = h.shape
    dp = p.shape[-1]
    final = final_norm is not None
    in_specs = [pl.BlockSpec((bm, d), lambda i: (i, 0)),
                pl.BlockSpec((None, bm, dp), lambda i: (layer, i, 0)),
                pl.BlockSpec((1, d), lambda i: (0, 0)),
                pl.BlockSpec((dp, d), lambda i: (0, 0)),
                pl.BlockSpec((d, d), lambda i: (0, 0))]
    args = [h, p, gate_norm.reshape(1, d), w_proj, w_gate]
    if final:
        in_specs.append(pl.BlockSpec((1, d), lambda i: (0, 0)))
        args.append(final_norm.reshape(1, d))
    return _call(
        functools.partial(_ple_kernel, bn=bn, final=final),
        grid=(t // bm,),
        in_specs=in_specs,
        out_specs=[pl.BlockSpec((bm, d), lambda i: (i, 0))],
        out_shape=[jax.ShapeDtypeStruct((t, d), F32)],
        args=args, name="ple", casts=casts)


def _split3(x):
    hi = x.astype(BF16).astype(F32)
    r1 = x - hi
    mid = r1.astype(BF16).astype(F32)
    lo = (r1 - mid).astype(BF16).astype(F32)
    return hi, mid, lo


def _lane_pick(x, lane, idx):
    return jnp.sum(jnp.where(lane == idx, x, 0.0), axis=-1, keepdims=True)


def _gdn_kernel(q_ref, k_ref, v_ref, z_ref, ab_ref, hp_ref, nw_ref, tril_ref, o_ref, s_ref, *, lb, nh):
    c = GDN_CHUNK
    n_chunks = lb // c
    dk = GDN_D

    @pl.when(pl.program_id(2) == 0)
    def _():
        s_ref[...] = jnp.zeros_like(s_ref)

    ab = ab_ref[...]
    lane = lax.broadcasted_iota(jnp.int32, ab.shape, 1)
    hp = hp_ref[...]
    xa = ab + hp[1:2]
    softplus = jnp.maximum(xa, 0.0) + jnp.log1p(jnp.exp(-jnp.abs(xa)))
    g_lanes = -jnp.exp(hp[0:1]) * softplus
    beta_lanes = jax.nn.sigmoid(ab)

    def pieces_to_lanes(x, first, sign):
        hi, mid, lo = _split3(x)
        out = jnp.where(lane == first, sign * hi, 0.0)
        out = jnp.where(lane == first + 1, sign * mid, out)
        return jnp.where(lane == first + 2, sign * lo, out)

    row2 = lax.broadcasted_iota(jnp.int32, (c, 2 * c), 0)
    lane2 = lax.broadcasted_iota(jnp.int32, (c, 2 * c), 1)
    col2 = lane2 & (c - 1)
    incl2 = row2 >= col2
    strict2 = row2 > col2
    left = lane2 < c
    eye_left = jnp.where(left & (row2 == lane2), 1.0, 0.0)
    zeros_k = jnp.zeros((c, 2 * c), BF16)
    zeros_x = jnp.zeros((c, 2 * dk), BF16)

    items = []
    for hi in range(nh):
        head = pl.program_id(1) * nh + hi
        cols = slice(hi * dk, (hi + 1) * dk)
        qb = q_ref[:, cols]
        kb16 = k_ref[:, cols]
        q = qb.astype(F32)
        k = kb16.astype(F32)
        v = v_ref[:, cols].astype(F32)
        g_col = _lane_pick(g_lanes, lane, head)
        beta = _lane_pick(beta_lanes, lane, head + GDN_HEADS)
        gm = _dot(tril_ref[...], pieces_to_lanes(g_col, 0, 1.0).astype(BF16))
        big_g = jnp.sum(gm, axis=-1, keepdims=True)
        e_g = jnp.exp(big_g)
        am = jnp.where(lane < 3, pieces_to_lanes(big_g, 0, 1.0), jnp.where(lane < 6, 1.0, 0.0)).astype(BF16)
        bmat = jnp.where(lane < 3, 1.0, pieces_to_lanes(big_g, 3, -1.0)).astype(BF16)
        kbeta = k * beta
        x0 = jnp.concatenate([v * beta, kbeta * e_g], axis=-1).astype(BF16)
        qd = q * e_g
        kbeta16 = kbeta.astype(BF16)
        for n in range(n_chunks):
            r = slice(n * c, (n + 1) * c)
            g_last = big_g[(n + 1) * c - 1:(n + 1) * c, :]
            items.append(dict(
                am=am[r], bm2=jnp.concatenate([bmat[r], bmat[r]], axis=0),
                kb=kbeta16[r], k=kb16[r], k2=jnp.concatenate([kb16[r], kb16[r]], axis=0), q=qb[r],
                x0=jnp.concatenate([x0[r], zeros_x], axis=0), qd=qd[r],
                kd=(k[r] * jnp.exp(g_last - big_g[r])).astype(BF16), gl=jnp.exp(g_last)))

    for it in items:
        it["gamma2"] = jnp.exp(jnp.where(incl2, _dot_nt(it["am"], it["bm2"]), -jnp.inf))
    for it in items:
        it["tp"] = jnp.where(left, eye_left, jnp.where(strict2, -_dot_nt(it["kb"], it["k2"]) * it["gamma2"], 0.0))
    for it in items:
        it["attn"] = (_dot_nt(it["q"], it["k"]) * it["gamma2"][:, :c]).astype(BF16)

    for j in range(6):
        for it in items:
            it["tpb"] = it["tp"].astype(BF16)
        for it in items:
            upd = _dot(it["tpb"], jnp.concatenate([zeros_k, it["tpb"]], axis=0))
            it["tp"] = jnp.where(left, it["tp"] + upd, upd)
    for it in items:
        it["tpb"] = it["tp"].astype(BF16)
    for it in items:
        it["xb"] = _dot(it["tpb"], it["x0"]).astype(BF16)

    for it in items:
        it["au"] = _dot(it["attn"], it["xb"])
    for it in items:
        it["ku"] = _dot_tn(it["kd"], it["xb"])
    for it in items:
        it["lhs"] = jnp.concatenate([it["qd"] - it["au"][:, dk:], it["ku"][:, dk:]], axis=0).astype(BF16)

    states = [s_ref[hi] for hi in range(nh)]
    outs = [[None] * n_chunks for _ in range(nh)]
    for n in range(n_chunks):
        for hi in range(nh):
            it = items[hi * n_chunks + n]
            prod = _dot(it["lhs"], states[hi].astype(BF16))
            outs[hi][n] = prod[:c] + it["au"][:, :dk]
            states[hi] = states[hi] * it["gl"] - prod[c:] + it["ku"][:, :dk]
    for hi in range(nh):
        s_ref[hi] = states[hi]

    nw = nw_ref[...]
    for hi in range(nh):
        cols = slice(hi * dk, (hi + 1) * dk)
        o = jnp.concatenate(outs[hi], axis=0)
        o_ref[:, cols] = (_rms(o, nw) * _silu(z_ref[:, cols].astype(F32))).astype(o_ref.dtype)


def _gdn(main, side, a_log, dt_bias, norm_w, *, batch, seq, lb, nh):
    t = main.shape[0]
    ns = seq // lb
    hh = GDN_HEADS
    ng = hh // nh
    w = nh * GDN_D
    hp = jnp.zeros((8, LANES), F32)
    hp = hp.at[0, :hh].set(a_log).at[1, :hh].set(dt_bias)
    r = jnp.arange(lb)
    tril = ((r[:, None] >= r[None, :]) & (r[:, None] // GDN_CHUNK == r[None, :] // GDN_CHUNK)).astype(BF16)
    row = lambda b, h, s: b * ns + s
    return pl.pallas_call(
        functools.partial(_gdn_kernel, lb=lb, nh=nh),
        grid=(batch, ng, ns),
        in_specs=[
            pl.BlockSpec((lb, w), lambda b, h, s: (row(b, h, s), h)),
            pl.BlockSpec((lb, w), lambda b, h, s: (row(b, h, s), ng + h)),
            pl.BlockSpec((lb, w), lambda b, h, s: (row(b, h, s), 2 * ng + h)),
            pl.BlockSpec((lb, w), lambda b, h, s: (row(b, h, s), 3 * ng + h)),
            pl.BlockSpec((lb, LANES), lambda b, h, s: (row(b, h, s), 1)),
            pl.BlockSpec((8, LANES), lambda b, h, s: (0, 0)),
            pl.BlockSpec((1, GDN_D), lambda b, h, s: (0, 0)),
            pl.BlockSpec((lb, lb), lambda b, h, s: (0, 0)),
        ],
        out_specs=pl.BlockSpec((lb, w), lambda b, h, s: (row(b, h, s), h)),
        out_shape=jax.ShapeDtypeStruct((t, hh * GDN_D), BF16),
        scratch_shapes=[pltpu.VMEM((nh, GDN_D, GDN_D), F32)],
        compiler_params=_params(3), name="gdn",
    )(main, main, main, main, side, hp, norm_w.reshape(1, GDN_D), tril)


def _mla_up_kernel(cq_ref, ckv_ref, side_ref, tab_ref, qn_ref, kvn_ref, wq_ref, wk_ref, wvt_ref,
                   q_out, k_out, v_out):
    hh = MLA_HEADS
    tab = tab_ref[...]
    scale = (MLA_NOPE + MLA_ROPE) ** -0.5 * LOG2_E

    cq = _rms(cq_ref[...].astype(F32), qn_ref[...]).astype(BF16)
    ckv = _rms(ckv_ref[...].astype(F32), kvn_ref[...]).astype(BF16)
    q_all = _dot(cq, wq_ref[...])
    k_all = _dot(ckv, wk_ref[...])
    v_out[...] = _dot_nt(wvt_ref[...], ckv).astype(BF16)

    kr = side_ref[...] * tab
    kpe = (kr + pltpu.roll(kr, shift=MLA_ROPE, axis=1)).astype(BF16)
    rot = tab * scale
    for h in range(hh):
        q_out[:, h * 256:h * 256 + LANES] = (q_all[:, h * 256:h * 256 + LANES] * scale).astype(BF16)
        q_out[:, h * 256 + LANES:(h + 1) * 256] = (q_all[:, h * 256 + LANES:(h + 1) * 256] * rot).astype(BF16)
        k_out[:, h * 256:h * 256 + LANES] = k_all[:, h * LANES:(h + 1) * LANES].astype(BF16)
        k_out[:, h * 256 + LANES:(h + 1) * 256] = kpe


def _mla_up(main, side, tab, q_norm, kv_norm, wq, wk, wvt, *, bm):
    t = main.shape[0]
    r = MLA_RANK
    cq_blk = (GDN_HEADS * GDN_D * 4) // r
    return pl.pallas_call(
        _mla_up_kernel,
        grid=(t // bm,),
        in_specs=[
            pl.BlockSpec((bm, r), lambda i: (i, cq_blk)),
            pl.BlockSpec((bm, r), lambda i: (i, cq_blk + 1)),
            pl.BlockSpec((bm, LANES), lambda i: (i, 0)),
            pl.BlockSpec((bm, LANES), lambda i: (i, 0)),
            pl.BlockSpec((1, r), lambda i: (0, 0)),
            pl.BlockSpec((1, r), lambda i: (0, 0)),
            pl.BlockSpec(wq.shape, lambda i: (0, 0)),
            pl.BlockSpec(wk.shape, lambda i: (0, 0)),
            pl.BlockSpec(wvt.shape, lambda i: (0, 0)),
        ],
        out_specs=[pl.BlockSpec((bm, MLA_HEADS * 256), lambda i: (i, 0)),
                   pl.BlockSpec((bm, MLA_HEADS * 256), lambda i: (i, 0)),
                   pl.BlockSpec((MLA_HEADS * MLA_V, bm), lambda i: (0, i))],
        out_shape=[jax.ShapeDtypeStruct((t, MLA_HEADS * 256), BF16),
                   jax.ShapeDtypeStruct((t, MLA_HEADS * 256), BF16),
                   jax.ShapeDtypeStruct((MLA_HEADS * MLA_V, t), BF16)],
        compiler_params=_params(1), name="mla_up",
    )(main, main, side, tab, q_norm.reshape(1, r), kv_norm.reshape(1, r), wq, wk, wvt)


def _flash_kernel(q_ref, k_ref, vt_ref, o_ref, *, blk, nh):
    qi = pl.program_id(2)
    dq = 2 * LANES
    qs = [q_ref[:, h * dq:(h + 1) * dq] for h in range(nh)]

    def step(j, carry, masked):
        start = pl.multiple_of(j * blk, blk)
        ss = [_dot_nt(k_ref[pl.ds(start, blk), h * dq:(h + 1) * dq], qs[h]) for h in range(nh)]
        if masked:
            key = lax.broadcasted_iota(jnp.int32, ss[0].shape, 0)
            qry = lax.broadcasted_iota(jnp.int32, ss[0].shape, 1)
            ss = [jnp.where(key <= qry, s, -jnp.inf) for s in ss]
        m_new = [jnp.maximum(carry[h][0], jnp.max(ss[h], axis=0, keepdims=True)) for h in range(nh)]
        ps = [jnp.exp2(ss[h] - m_new[h]) for h in range(nh)]
        pv = [_dot(vt_ref[h * MLA_V:(h + 1) * MLA_V, pl.ds(start, blk)], ps[h].astype(BF16)) for h in range(nh)]
        out = []
        for h in range(nh):
            m, l, acc = carry[h]
            alpha = jnp.exp2(m - m_new[h])
            out.append((m_new[h], l * alpha + jnp.sum(ps[h], axis=0, keepdims=True), acc * alpha + pv[h]))
        return tuple(out)

    init = tuple((jnp.full((1, blk), -jnp.inf, F32), jnp.zeros((1, blk), F32),
                  jnp.zeros((MLA_V, blk), F32)) for _ in range(nh))
    carry = lax.fori_loop(0, qi, lambda j, cr: step(j, cr, False), init)
    carry = step(qi, carry, True)
    for h in range(nh):
        _, l, acc = carry[h]
        o_ref[:, h * MLA_V:(h + 1) * MLA_V] = (acc / l).T.astype(o_ref.dtype)


def _flash(qf, kf, vt, *, batch, seq, blk, nh, casts=()):
    t = qf.shape[0]
    nq = seq // blk
    hh = MLA_HEADS
    return _call(
        functools.partial(_flash_kernel, blk=blk, nh=nh),
        grid=(batch, hh // nh, nq),
        in_specs=[
            pl.BlockSpec((blk, nh * 256), lambda b, h, i: (b * nq + i, h)),
            pl.BlockSpec((seq, nh * 256), lambda b, h, i: (b, h)),
            pl.BlockSpec((nh * MLA_V, seq), lambda b, h, i: (h, b)),
        ],
        out_specs=[pl.BlockSpec((blk, nh * MLA_V), lambda b, h, i: (b * nq + i, h))],
        out_shape=[jax.ShapeDtypeStruct((t, hh * MLA_V), BF16)],
        args=(qf, kf, vt), name="mla_flash", casts=casts)


def _ret_kernel(lg_ref, q_ref, k_ref, v_ref, g_ref, nw_ref, o_ref,
                s_ref, dec_ref, xz_ref, *, c, nh):
    @pl.when(pl.program_id(2) == 0)
    def _():
        row = lax.broadcasted_iota(jnp.int32, (c, c), 0)
        col = lax.broadcasted_iota(jnp.int32, (c, c), 1)
        pos = lax.broadcasted_iota(jnp.int32, (c, LANES), 0).astype(F32)
        lane = lax.broadcasted_iota(jnp.int32, (c, LANES), 1)
        for h in range(nh):
            lg = lg_ref[h, :, 0:1]
            dec_ref[h] = jnp.exp(jnp.where(row >= col, (row - col).astype(F32) * lg, -jnp.inf))
            xi = jnp.exp((pos + 1.0) * lg)
            zeta = jnp.exp((c - 1.0 - pos) * lg)
            xz_ref[h] = jnp.where(lane == 0, xi, zeta)
        s_ref[...] = jnp.zeros_like(s_ref)

    hs = range(nh)
    qb = [q_ref[:, h * RET_DK:(h + 1) * RET_DK] for h in hs]
    kb = [k_ref[:, h * RET_DK:(h + 1) * RET_DK] for h in hs]
    v = [v_ref[:, h * RET_DV:(h + 1) * RET_DV] for h in hs]
    qk = [(_dot_nt(qb[h], kb[h]) * dec_ref[h]).astype(BF16) for h in hs]
    states = [s_ref[h] for h in hs]
    cross = [_dot((qb[h].astype(F32) * xz_ref[h, :, 0:1]).astype(BF16), states[h].astype(BF16)) for h in hs]
    inner = [_dot(qk[h], v[h]) for h in hs]
    upd = [_dot_tn((kb[h].astype(F32) * xz_ref[h, :, 1:2]).astype(BF16), v[h]) for h in hs]
    for h in hs:
        s_ref[h] = states[h] * xz_ref[h, c - 1:c, 0:1] + upd[h]
    for h in hs:
        o = inner[h] + cross[h]
        mu = jnp.mean(o, axis=-1, keepdims=True)
        xc = o - mu
        var = jnp.mean(xc * xc, axis=-1, keepdims=True)
        cols = slice(h * RET_DV, (h + 1) * RET_DV)
        y = xc * lax.rsqrt(var + NORM_EPS) * nw_ref[:, cols]
        o_ref[:, cols] = (g_ref[:, cols].astype(F32) * y).astype(o_ref.dtype)


def _retention(main, norm_w, *, batch, seq, c, nh, casts=()):
    t = main.shape[0]
    nc = seq // c
    hh = RET_HEADS
    ng = hh // nh
    log_gamma = jnp.log1p(-jnp.power(2.0, -5.0 - jnp.arange(hh, dtype=F32)))
    lg = jnp.broadcast_to(log_gamma[:, None, None], (hh, 1, LANES))
    row = lambda b, h, s: b * nc + s
    kq = RET_DK
    v_blk0 = (2 * hh * kq) // (nh * RET_DV)
    return _call(
        functools.partial(_ret_kernel, c=c, nh=nh),
        grid=(batch, ng, nc),
        in_specs=[
            pl.BlockSpec((nh, 1, LANES), lambda b, h, s: (h, 0, 0)),
            pl.BlockSpec((c, nh * kq), lambda b, h, s: (row(b, h, s), h)),
            pl.BlockSpec((c, nh * kq), lambda b, h, s: (row(b, h, s), ng + h)),
            pl.BlockSpec((c, nh * RET_DV), lambda b, h, s: (row(b, h, s), v_blk0 + h)),
            pl.BlockSpec((c, nh * RET_DV), lambda b, h, s: (row(b, h, s), v_blk0 + ng + h)),
            pl.BlockSpec((1, nh * RET_DV), lambda b, h, s: (0, h)),
        ],
        out_specs=[pl.BlockSpec((c, nh * RET_DV), lambda b, h, s: (row(b, h, s), h))],
        out_shape=[jax.ShapeDtypeStruct((t, hh * RET_DV), BF16)],
        scratch_shapes=[pltpu.VMEM((nh, kq, RET_DV), F32), pltpu.VMEM((nh, c, c), F32),
                        pltpu.VMEM((nh, c, LANES), F32)],
        args=(lg, main, main, main, main, norm_w.reshape(1, -1)), name="retention", casts=casts)


def _rotate_half_cols(w, half):
    return jnp.concatenate([-w[..., half:], w[..., :half]], axis=-1)


def _rope_tab(positions, dim):
    inv_freq = ROPE_THETA ** (-jnp.arange(0, dim, 2, dtype=F32) / dim)
    ang = positions.astype(F32).reshape(-1, 1) * inv_freq
    return jnp.cos(ang), jnp.sin(ang)


def kernel(x, p, positions, l0_attn_norm, l0_w_in, l0_gdn_conv, l0_gdn_A_log, l0_gdn_dt_bias, l0_gdn_norm, l0_mla_q_norm, l0_mla_w_uq, l0_mla_kv_norm, l0_mla_w_ukv, l0_w_out, l0_ffn_norm, l0_ffn_w_up, l0_ffn_conv_w, l0_ffn_conv_b, l0_ffn_w_down, l0_ple_proj, l0_ple_gate_norm, l0_ple_gate, l1_attn_norm, l1_w_in, l1_ret_norm, l1_w_out, l1_ffn_norm, l1_ffn_w_up, l1_ffn_conv_w, l1_ffn_conv_b, l1_ffn_w_down, l1_ple_proj, l1_ple_gate_norm, l1_ple_gate, final_norm):
    batch, seq, d = x.shape
    t = batch * seq
    x2 = x.reshape(t, d)
    p2 = p.reshape(p.shape[0], t, p.shape[-1])

    gq = GDN_HEADS * GDN_D * 4
    n_ab = 2 * GDN_HEADS
    c0 = gq + n_ab
    w_in0 = l0_w_in.astype(BF16)
    w_tail = w_in0[:, c0:c0 + 2 * MLA_RANK]
    w_kr = l0_w_in[:, c0 + 2 * MLA_RANK:]
    w_side = jnp.concatenate(
        [w_kr, _rotate_half_cols(w_kr, MLA_ROPE // 2), l0_w_in[:, gq:c0],
         jnp.zeros((d, LANES - n_ab), F32)], axis=1).astype(BF16)

    wq = l0_mla_w_uq.reshape(MLA_RANK, MLA_HEADS, MLA_NOPE + MLA_ROPE)
    wq_pe = wq[:, :, MLA_NOPE:]
    wq = jnp.concatenate([wq[:, :, :MLA_NOPE], wq_pe, _rotate_half_cols(wq_pe, MLA_ROPE // 2)], axis=-1)
    wq = wq.reshape(MLA_RANK, MLA_HEADS * 256).astype(BF16)
    wkv = l0_mla_w_ukv.reshape(MLA_RANK, MLA_HEADS, MLA_NOPE + MLA_V)
    wk = wkv[:, :, :MLA_NOPE].reshape(MLA_RANK, -1).astype(BF16)
    wvt = wkv[:, :, MLA_NOPE:].reshape(MLA_RANK, -1).T.astype(BF16)

    cos_m, sin_m = _rope_tab(positions, MLA_ROPE)
    tab_m = jnp.concatenate([cos_m, cos_m, sin_m, sin_m], axis=1)
    cos_r, sin_r = _rope_tab(positions, RET_DK)

    main0, side0 = _in_proj0(x2, l0_attn_norm, w_in0, w_tail, w_side, l0_gdn_conv, seq=seq, bm=1024, bn=1024)
    y_a = _gdn(main0, side0, l0_gdn_A_log, l0_gdn_dt_bias, l0_gdn_norm,
               batch=batch, seq=seq, lb=256, nh=8)
    qf, kf, vt = _mla_up(main0, side0, tab_m, l0_mla_q_norm, l0_mla_kv_norm, wq, wk, wvt, bm=512)
    y_b, w_up = _flash(qf, kf, vt, batch=batch, seq=seq, blk=512, nh=4, casts=[l0_ffn_w_up])
    w_out0 = l0_w_out.astype(BF16)
    ha = GDN_HEADS * GDN_D
    (h,) = _proj_res([(y_a, w_out0[:ha]), (y_b, w_out0[ha:])], x2, bm=1024, bn=1024)
    act, w_down = _ffn_up(h, l0_ffn_norm, w_up, l0_ffn_conv_w, l0_ffn_conv_b, seq=seq, bm=1024, bn=512,
                          casts=[l0_ffn_w_down])
    h, w_gate = _proj_res([(act, w_down)], h, bm=1024, bn=512, casts=[l0_ple_gate])
    h, w_in1, w_out1 = _ple(h, p2, 0, l0_ple_gate_norm, l0_ple_proj.astype(BF16), w_gate, bm=512, bn=512,
                            casts=[l1_w_in, l1_w_out])

    main1 = _in_proj1(h, l1_attn_norm, w_in1, cos_r, sin_r, bm=1024, bn=1024)
    y, w_up = _retention(main1, l1_ret_norm, batch=batch, seq=seq, c=256, nh=8, casts=[l1_ffn_w_up])
    (h,) = _proj_res([(y, w_out1)], h, bm=1024, bn=1024)
    act, w_down = _ffn_up(h, l1_ffn_norm, w_up, l1_ffn_conv_w, l1_ffn_conv_b, seq=seq, bm=1024, bn=512,
                          casts=[l1_ffn_w_down])
    h, w_gate = _proj_res([(act, w_down)], h, bm=1024, bn=512, casts=[l1_ple_gate])
    (h,) = _ple(h, p2, 1, l1_ple_gate_norm, l1_ple_proj.astype(BF16), w_gate, final_norm, bm=512, bn=512)
    return h.reshape(batch, seq, d)
```

```python
import functools

import jax
import jax.numpy as jnp
from jax import lax
from jax.experimental import pallas as pl
from jax.experimental.pallas import tpu as pltpu

F32 = jnp.float32
BF16 = jnp.bfloat16

NORM_EPS = 1e-6
ROPE_THETA = 10000.0

GDN_HEADS = 8
GDN_D = 128
GDN_CONV = 4
GDN_CHUNK = 64
MLA_HEADS = 8
MLA_RANK = 512
MLA_NOPE = 128
MLA_ROPE = 64
MLA_V = 128
RET_HEADS = 8
RET_DK = 256
RET_DV = 512
FFN_CONV = 3

LANES = 128
VMEM_LIMIT = 56 * 1024 * 1024
LOG2_E = 1.4426950408889634
CONV_HALO = 16


def _params(n_grid):
    return pltpu.CompilerParams(dimension_semantics=("arbitrary",) * n_grid,
                                vmem_limit_bytes=VMEM_LIMIT)


def _rms(x, g):
    return x * lax.rsqrt(jnp.mean(x * x, axis=-1, keepdims=True) + NORM_EPS) * g


def _dot(a, b):
    return jnp.dot(a, b, preferred_element_type=F32)


def _dot_nt(a, b):
    return lax.dot_general(a, b, (((1,), (1,)), ((), ())), preferred_element_type=F32)


def _dot_tn(a, b):
    return lax.dot_general(a, b, (((0,), (0,)), ((), ())), preferred_element_type=F32)


def _silu(x):
    return x * jax.nn.sigmoid(x)


def _call(kernel_fn, *, grid, in_specs, out_specs, out_shape, args, name, scratch_shapes=(), casts=()):
    n_in, n_out = len(in_specs), len(out_specs)
    n_steps = 1
    for g in grid:
        n_steps *= g

    def flat_step(*idx):
        step = idx[0]
        for g, i in zip(grid[1:], idx[1:]):
            step = step * g + i
        return step

    in_specs, out_specs, out_shape, args = list(in_specs), list(out_specs), list(out_shape), list(args)
    for w in casts:
        rows, cols = w.shape[0] // n_steps, w.shape[1]
        assert rows * n_steps == w.shape[0] and rows % 16 == 0, (w.shape, n_steps)
        spec = pl.BlockSpec((None, rows, cols), lambda *idx: (flat_step(*idx), 0, 0))
        in_specs.append(spec)
        out_specs.append(spec)
        out_shape.append(jax.ShapeDtypeStruct((n_steps, rows, cols), BF16))
        args.append(w.reshape(n_steps, rows, cols))

    def body(*refs):
        ins, extra_in = refs[:n_in], refs[n_in:n_in + len(casts)]
        outs = refs[n_in + len(casts):n_in + len(casts) + n_out]
        extra_out = refs[n_in + len(casts) + n_out:n_in + 2 * len(casts) + n_out]
        scratch = refs[n_in + 2 * len(casts) + n_out:]
        for src, dst in zip(extra_in, extra_out):
            dst[...] = src[...].astype(BF16)
        kernel_fn(*ins, *outs, *scratch)

    out = pl.pallas_call(
        body, grid=grid, in_specs=in_specs, out_specs=out_specs, out_shape=out_shape,
        scratch_shapes=list(scratch_shapes), compiler_params=_params(len(grid)), name=name,
    )(*args)
    return list(out[:n_out]) + [o.reshape(w.shape) for o, w in zip(out[n_out:], casts)]


def _in_proj0_kernel(x_ref, xh_ref, g_ref, w_ref, wt_ref, ws_ref, cw_ref, o_ref, os_ref, xn_ref, u_ref,
                     *, bm, blocks_per_seq, n_conv_blocks, n_main_blocks):
    i = pl.program_id(0)
    j = pl.program_id(1)

    @pl.when(j == 0)
    def _():
        g = g_ref[...]
        halo = _rms(xh_ref[...], g)
        halo = jnp.where(i % blocks_per_seq == 0, 0.0, halo)
        xn_ref[0:CONV_HALO, :] = halo.astype(BF16)
        xn = _rms(x_ref[...], g).astype(BF16)
        xn_ref[CONV_HALO:, :] = xn
        os_ref[...] = _dot(xn, ws_ref[...])

    def conv_silu():
        u_ref[...] = _dot(xn_ref[...], w_ref[...])
        cw = cw_ref[...]
        y = cw[GDN_CONV - 1:GDN_CONV] * u_ref[pl.ds(CONV_HALO, bm), :]
        for s in range(1, GDN_CONV):
            y = y + cw[GDN_CONV - 1 - s:GDN_CONV - s] * u_ref[pl.ds(CONV_HALO - s, bm), :]
        return _silu(y)

    @pl.when(j < n_conv_blocks - 1)
    def _():
        y = conv_silu()
        scale = jnp.where(j == 0, GDN_D ** -0.5, 1.0)
        for h in range(y.shape[1] // GDN_D):
            cols = slice(h * GDN_D, (h + 1) * GDN_D)
            yh = y[:, cols]
            inv = lax.rsqrt(jnp.sum(yh * yh, axis=-1, keepdims=True) + NORM_EPS) * scale
            o_ref[:, cols] = (yh * inv).astype(o_ref.dtype)

    @pl.when(j == n_conv_blocks - 1)
    def _():
        o_ref[...] = conv_silu().astype(o_ref.dtype)

    @pl.when((j >= n_conv_blocks) & (j < n_main_blocks))
    def _():
        o_ref[...] = _dot(xn_ref[CONV_HALO:, :], w_ref[...]).astype(o_ref.dtype)

    @pl.when(j >= n_main_blocks)
    def _():
        o_ref[...] = _dot(xn_ref[CONV_HALO:, :], wt_ref[...]).astype(o_ref.dtype)


def _in_proj0(x, g, w, w_tail, w_side, conv_w, *, seq, bm, bn):
    t, d = x.shape
    ns = w_side.shape[1]
    n_conv_blocks = conv_w.shape[1] // bn
    n_main_blocks = (GDN_HEADS * GDN_D * 4) // bn
    n_tail_blocks = w_tail.shape[1] // bn
    n = (n_main_blocks + n_tail_blocks) * bn
    hb = bm // CONV_HALO
    kern = functools.partial(_in_proj0_kernel, bm=bm, blocks_per_seq=seq // bm, n_conv_blocks=n_conv_blocks,
                             n_main_blocks=n_main_blocks)
    return pl.pallas_call(
        kern,
        grid=(t // bm, n_main_blocks + n_tail_blocks),
        in_specs=[
            pl.BlockSpec((bm, d), lambda i, j: (i, 0)),
            pl.BlockSpec((CONV_HALO, d), lambda i, j: (jnp.maximum(i * hb - 1, 0), 0)),
            pl.BlockSpec((1, d), lambda i, j: (0, 0)),
            pl.BlockSpec((d, bn), lambda i, j: (0, jnp.minimum(j, n_main_blocks - 1))),
            pl.BlockSpec((d, bn), lambda i, j: (0, jnp.maximum(j - n_main_blocks, 0))),
            pl.BlockSpec((d, ns), lambda i, j: (0, 0)),
            pl.BlockSpec((GDN_CONV, bn), lambda i, j: (0, jnp.minimum(j, n_conv_blocks - 1))),
        ],
        out_specs=[pl.BlockSpec((bm, bn), lambda i, j: (i, j)),
                   pl.BlockSpec((bm, ns), lambda i, j: (i, 0))],
        out_shape=[jax.ShapeDtypeStruct((t, n), BF16), jax.ShapeDtypeStruct((t, ns), F32)],
        scratch_shapes=[pltpu.VMEM((bm + CONV_HALO, d), BF16), pltpu.VMEM((bm + CONV_HALO, bn), F32)],
        compiler_params=_params(2), name="in_proj0",
    )(x, x, g.reshape(1, d), w, w_tail, w_side, conv_w)


def _in_proj1_kernel(x_ref, g_ref, w_ref, cs_ref, o_ref, xn_ref, *, n_q, n_k, n_v):
    j = pl.program_id(1)
    half = RET_DK // 2

    @pl.when(j == 0)
    def _():
        xn_ref[...] = _rms(x_ref[...], g_ref[...]).astype(BF16)

    def proj():
        return _dot(xn_ref[...], w_ref[...])

    def rope_store(scale):
        y = proj()
        cos = cs_ref[:, :half]
        sin = cs_ref[:, half:]
        for h in range(y.shape[1] // RET_DK):
            x1 = y[:, h * RET_DK:h * RET_DK + half]
            x2 = y[:, h * RET_DK + half:(h + 1) * RET_DK]
            o_ref[:, h * RET_DK:h * RET_DK + half] = ((x1 * cos - x2 * sin) * scale).astype(o_ref.dtype)
            o_ref[:, h * RET_DK + half:(h + 1) * RET_DK] = ((x2 * cos + x1 * sin) * scale).astype(o_ref.dtype)

    @pl.when(j < n_q)
    def _():
        rope_store(1.0)

    @pl.when((j >= n_q) & (j < n_q + n_k))
    def _():
        rope_store(RET_DK ** -0.5)

    @pl.when((j >= n_q + n_k) & (j < n_q + n_k + n_v))
    def _():
        o_ref[...] = proj().astype(o_ref.dtype)

    @pl.when(j >= n_q + n_k + n_v)
    def _():
        o_ref[...] = _silu(proj()).astype(o_ref.dtype)


def _in_proj1(x, g, w, tab, *, bm, bn):
    t, d = x.shape
    n = w.shape[1]
    n_q = RET_HEADS * RET_DK // bn
    n_v = RET_HEADS * RET_DV // bn
    return pl.pallas_call(
        functools.partial(_in_proj1_kernel, n_q=n_q, n_k=n_q, n_v=n_v),
        grid=(t // bm, n // bn),
        in_specs=[pl.BlockSpec((bm, d), lambda i, j: (i, 0)),
                  pl.BlockSpec((1, d), lambda i, j: (0, 0)),
                  pl.BlockSpec((d, bn), lambda i, j: (0, j)),
                  pl.BlockSpec((bm, RET_DK), lambda i, j: (i, 0))],
        out_specs=pl.BlockSpec((bm, bn), lambda i, j: (i, j)),
        out_shape=jax.ShapeDtypeStruct((t, n), BF16),
        scratch_shapes=[pltpu.VMEM((bm, d), BF16)],
        compiler_params=_params(2), name="in_proj1",
    )(x, g.reshape(1, d), w, tab)


def _proj_res_kernel(*refs, n_pairs):
    res_ref, o_ref = refs[2 * n_pairs], refs[2 * n_pairs + 1]
    acc = res_ref[...]
    for p in range(n_pairs):
        acc = acc + _dot(refs[2 * p][...], refs[2 * p + 1][...])
    o_ref[...] = acc


def _proj_res(pairs, res, *, bm, bn, casts=()):
    t, n = res.shape
    in_specs, args = [], []
    for y, w in pairs:
        kk = y.shape[1]
        in_specs += [pl.BlockSpec((bm, kk), lambda i, j: (i, 0)),
                     pl.BlockSpec((kk, bn), lambda i, j: (0, j))]
        args += [y, w]
    in_specs.append(pl.BlockSpec((bm, bn), lambda i, j: (i, j)))
    args.append(res)
    return _call(
        functools.partial(_proj_res_kernel, n_pairs=len(pairs)),
        grid=(t // bm, n // bn),
        in_specs=in_specs,
        out_specs=[pl.BlockSpec((bm, bn), lambda i, j: (i, j))],
        out_shape=[jax.ShapeDtypeStruct((t, n), F32)],
        args=args, name="proj_res", casts=casts)


def _ffn_up_kernel(h_ref, hh_ref, g_ref, wg_ref, wu_ref, cg_ref, cu_ref, bg_ref, bu_ref,
                   o_ref, xn_ref, ug_ref, uu_ref, *, bm, blocks_per_seq):
    i = pl.program_id(0)

    @pl.when(pl.program_id(1) == 0)
    def _():
        g = g_ref[...]
        halo = _rms(hh_ref[...], g)
        halo = jnp.where(i % blocks_per_seq == 0, 0.0, halo)
        xn_ref[0:CONV_HALO, :] = halo.astype(BF16)
        xn_ref[CONV_HALO:, :] = _rms(h_ref[...], g).astype(BF16)

    xn = xn_ref[...]
    ug_ref[...] = _dot(xn, wg_ref[...])
    uu_ref[...] = _dot(xn, wu_ref[...])

    def conv(u_ref, c_ref, b_ref):
        cw = c_ref[...]
        y = b_ref[...] + cw[FFN_CONV - 1:FFN_CONV] * u_ref[pl.ds(CONV_HALO, bm), :]
        for s in range(1, FFN_CONV):
            y = y + cw[FFN_CONV - 1 - s:FFN_CONV - s] * u_ref[pl.ds(CONV_HALO - s, bm), :]
        return y

    gate = conv(ug_ref, cg_ref, bg_ref)
    up = conv(uu_ref, cu_ref, bu_ref)
    o_ref[...] = (_silu(gate) * up).astype(o_ref.dtype)


def _ffn_up(h, g, w_up, conv_w, conv_b, *, seq, bm, bn, casts=()):
    t, d = h.shape
    dff = w_up.shape[1] // 2
    nj = dff // bn
    hb = bm // CONV_HALO
    kern = functools.partial(_ffn_up_kernel, bm=bm, blocks_per_seq=seq // bm)
    return _call(
        kern,
        grid=(t // bm, nj),
        in_specs=[
            pl.BlockSpec((bm, d), lambda i, j: (i, 0)),
            pl.BlockSpec((CONV_HALO, d), lambda i, j: (jnp.maximum(i * hb - 1, 0), 0)),
            pl.BlockSpec((1, d), lambda i, j: (0, 0)),
            pl.BlockSpec((d, bn), lambda i, j: (0, j)),
            pl.BlockSpec((d, bn), lambda i, j: (0, j + nj)),
            pl.BlockSpec((FFN_CONV, bn), lambda i, j: (0, j)),
            pl.BlockSpec((FFN_CONV, bn), lambda i, j: (0, j + nj)),
            pl.BlockSpec((1, bn), lambda i, j: (0, j)),
            pl.BlockSpec((1, bn), lambda i, j: (0, j + nj)),
        ],
        out_specs=[pl.BlockSpec((bm, bn), lambda i, j: (i, j))],
        out_shape=[jax.ShapeDtypeStruct((t, dff), BF16)],
        scratch_shapes=[pltpu.VMEM((bm + CONV_HALO, d), BF16),
                        pltpu.VMEM((bm + CONV_HALO, bn), F32),
                        pltpu.VMEM((bm + CONV_HALO, bn), F32)],
        args=(h, h, g.reshape(1, d), w_up, w_up, conv_w, conv_w,
              conv_b.reshape(1, -1), conv_b.reshape(1, -1)),
        name="ffn_up", casts=casts)


def _ple_kernel(h_ref, p_ref, gn_ref, wp_ref, wg_ref, *rest, bn, final):
    if final:
        fn_ref, o_ref = rest
    else:
        (o_ref,) = rest
    d = h_ref.shape[1]
    hn = _rms(h_ref[...], gn_ref[...]).astype(BF16)
    pb = p_ref[...].astype(BF16)
    ssq = None
    for c in range(d // bn):
        cols = slice(c * bn, (c + 1) * bn)
        gate = _dot(hn, wg_ref[:, cols])
        proj = _dot(pb, wp_ref[:, cols])
        out = h_ref[:, cols] + proj * jax.nn.sigmoid(gate)
        o_ref[:, cols] = out
        if final:
            part = jnp.sum(out * out, axis=-1, keepdims=True)
            ssq = part if ssq is None else ssq + part
    if final:
        scale = lax.rsqrt(ssq / d + NORM_EPS)
        o_ref[...] = o_ref[...] * scale * fn_ref[...]


def _ple(h, p, layer, gate_norm, w_proj, w_gate, final_norm=None, *, bm, bn, casts=()):
    t, d = h.shape
    dp = p.shape[-1]
    final = final_norm is not None
    in_specs = [pl.BlockSpec((bm, d), lambda i: (i, 0)),
                pl.BlockSpec((None, bm, dp), lambda i: (layer, i, 0)),
                pl.BlockSpec((1, d), lambda i: (0, 0)),
                pl.BlockSpec((dp, d), lambda i: (0, 0)),
                pl.BlockSpec((d, d), lambda i: (0, 0))]
    args = [h, p, gate_norm.reshape(1, d), w_proj, w_gate]
    if final:
        in_specs.append(pl.BlockSpec((1, d), lambda i: (0, 0)))
        args.append(final_norm.reshape(1, d))
    return _call(
        functools.partial(_ple_kernel, bn=bn, final=final),
        grid=(t // bm,),
        in_specs=in_specs,
        out_specs=[pl.BlockSpec((bm, d), lambda i: (i, 0))],
        out_shape=[jax.ShapeDtypeStruct((t, d), F32)],
        args=args, name="ple", casts=casts)


def _split3(x):
    hi = x.astype(BF16).astype(F32)
    r1 = x - hi
    mid = r1.astype(BF16).astype(F32)
    lo = (r1 - mid).astype(BF16).astype(F32)
    return hi, mid, lo


def _lane_pick(x, lane, idx):
    return jnp.sum(jnp.where(lane == idx, x, 0.0), axis=-1, keepdims=True)


def _gdn_kernel(q_ref, k_ref, v_ref, z_ref, ab_ref, hp_ref, nw_ref, tril_ref, o_ref, s_ref, *, lb, nh):
    c = GDN_CHUNK
    n_chunks = lb // c
    dk = GDN_D

    @pl.when(pl.program_id(2) == 0)
    def _():
        s_ref[...] = jnp.zeros_like(s_ref)

    ab = ab_ref[...]
    lane = lax.broadcasted_iota(jnp.int32, ab.shape, 1)
    hp = hp_ref[...]
    xa = ab + hp[1:2]
    softplus = jnp.maximum(xa, 0.0) + jnp.log1p(jnp.exp(-jnp.abs(xa)))
    g_lanes = -jnp.exp(hp[0:1]) * softplus
    beta_lanes = jax.nn.sigmoid(ab)

    def pieces_to_lanes(x, first, sign):
        hi, mid, lo = _split3(x)
        out = jnp.where(lane == first, sign * hi, 0.0)
        out = jnp.where(lane == first + 1, sign * mid, out)
        return jnp.where(lane == first + 2, sign * lo, out)

    row2 = lax.broadcasted_iota(jnp.int32, (c, 2 * c), 0)
    lane2 = lax.broadcasted_iota(jnp.int32, (c, 2 * c), 1)
    col2 = lane2 & (c - 1)
    incl2 = row2 >= col2
    strict2 = row2 > col2
    left = lane2 < c
    eye_left = jnp.where(left & (row2 == lane2), 1.0, 0.0)
    zeros_k = jnp.zeros((c, 2 * c), BF16)
    zeros_x = jnp.zeros((c, 2 * dk), BF16)

    items = []
    for hi in range(nh):
        head = pl.program_id(1) * nh + hi
        cols = slice(hi * dk, (hi + 1) * dk)
        qb = q_ref[:, cols]
        kb16 = k_ref[:, cols]
        q = qb.astype(F32)
        k = kb16.astype(F32)
        v = v_ref[:, cols].astype(F32)
        g_col = _lane_pick(g_lanes, lane, head)
        beta = _lane_pick(beta_lanes, lane, head + GDN_HEADS)
        gm = _dot(tril_ref[...], pieces_to_lanes(g_col, 0, 1.0).astype(BF16))
        big_g = jnp.sum(gm, axis=-1, keepdims=True)
        e_g = jnp.exp(big_g)
        am = jnp.where(lane < 3, pieces_to_lanes(big_g, 0, 1.0), jnp.where(lane < 6, 1.0, 0.0)).astype(BF16)
        bmat = jnp.where(lane < 3, 1.0, pieces_to_lanes(big_g, 3, -1.0)).astype(BF16)
        kbeta = k * beta
        x0 = jnp.concatenate([v * beta, kbeta * e_g], axis=-1).astype(BF16)
        qd = q * e_g
        kbeta16 = kbeta.astype(BF16)
        for n in range(n_chunks):
            r = slice(n * c, (n + 1) * c)
            g_last = big_g[(n + 1) * c - 1:(n + 1) * c, :]
            items.append(dict(
                am=am[r], bm2=jnp.concatenate([bmat[r], bmat[r]], axis=0),
                kb=kbeta16[r], k=kb16[r], k2=jnp.concatenate([kb16[r], kb16[r]], axis=0), q=qb[r],
                x0=jnp.concatenate([x0[r], zeros_x], axis=0), qd=qd[r],
                kd=(k[r] * jnp.exp(g_last - big_g[r])).astype(BF16), gl=jnp.exp(g_last)))

    for it in items:
        it["gamma2"] = jnp.exp(jnp.where(incl2, _dot_nt(it["am"], it["bm2"]), -jnp.inf))
    for it in items:
        it["tp"] = jnp.where(left, eye_left, jnp.where(strict2, -_dot_nt(it["kb"], it["k2"]) * it["gamma2"], 0.0))
    for it in items:
        it["attn"] = (_dot_nt(it["q"], it["k"]) * it["gamma2"][:, :c]).astype(BF16)

    for j in range(6):
        for it in items:
            it["tpb"] = it["tp"].astype(BF16)
        for it in items:
            upd = _dot(it["tpb"], jnp.concatenate([zeros_k, it["tpb"]], axis=0))
            it["tp"] = jnp.where(left, it["tp"] + upd, upd)
    for it in items:
        it["tpb"] = it["tp"].astype(BF16)
    for it in items:
        it["xb"] = _dot(it["tpb"], it["x0"]).astype(BF16)

    for it in items:
        it["au"] = _dot(it["attn"], it["xb"])
    for it in items:
        it["ku"] = _dot_tn(it["kd"], it["xb"])
    for it in items:
        it["lhs"] = jnp.concatenate([it["qd"] - it["au"][:, dk:], it["ku"][:, dk:]], axis=0).astype(BF16)

    states = [s_ref[hi] for hi in range(nh)]
    outs = [[None] * n_chunks for _ in range(nh)]
    for n in range(n_chunks):
        for hi in range(nh):
            it = items[hi * n_chunks + n]
            prod = _dot(it["lhs"], states[hi].astype(BF16))
            outs[hi][n] = prod[:c] + it["au"][:, :dk]
            states[hi] = states[hi] * it["gl"] - prod[c:] + it["ku"][:, :dk]
    for hi in range(nh):
        s_ref[hi] = states[hi]

    nw = nw_ref[...]
    for hi in range(nh):
        cols = slice(hi * dk, (hi + 1) * dk)
        o = jnp.concatenate(outs[hi], axis=0)
        o_ref[:, cols] = (_rms(o, nw) * _silu(z_ref[:, cols].astype(F32))).astype(o_ref.dtype)


def _gdn(main, side, a_log, dt_bias, norm_w, *, batch, seq, lb, nh):
    t = main.shape[0]
    ns = seq // lb
    hh = GDN_HEADS
    ng = hh // nh
    w = nh * GDN_D
    hp = jnp.zeros((8, LANES), F32)
    hp = hp.at[0, :hh].set(a_log).at[1, :hh].set(dt_bias)
    r = jnp.arange(lb)
    tril = ((r[:, None] >= r[None, :]) & (r[:, None] // GDN_CHUNK == r[None, :] // GDN_CHUNK)).astype(BF16)
    row = lambda b, h, s: b * ns + s
    return pl.pallas_call(
        functools.partial(_gdn_kernel, lb=lb, nh=nh),
        grid=(batch, ng, ns),
        in_specs=[
            pl.BlockSpec((lb, w), lambda b, h, s: (row(b, h, s), h)),
            pl.BlockSpec((lb, w), lambda b, h, s: (row(b, h, s), ng + h)),
            pl.BlockSpec((lb, w), lambda b, h, s: (row(b, h, s), 2 * ng + h)),
            pl.BlockSpec((lb, w), lambda b, h, s: (row(b, h, s), 3 * ng + h)),
            pl.BlockSpec((lb, LANES), lambda b, h, s: (row(b, h, s), 1)),
            pl.BlockSpec((8, LANES), lambda b, h, s: (0, 0)),
            pl.BlockSpec((1, GDN_D), lambda b, h, s: (0, 0)),
            pl.BlockSpec((lb, lb), lambda b, h, s: (0, 0)),
        ],
        out_specs=pl.BlockSpec((lb, w), lambda b, h, s: (row(b, h, s), h)),
        out_shape=jax.ShapeDtypeStruct((t, hh * GDN_D), BF16),
        scratch_shapes=[pltpu.VMEM((nh, GDN_D, GDN_D), F32)],
        compiler_params=_params(3), name="gdn",
    )(main, main, main, main, side, hp, norm_w.reshape(1, GDN_D), tril)


def _mla_up_kernel(cq_ref, ckv_ref, side_ref, tab_ref, qn_ref, kvn_ref, wq_ref, wk_ref, wvt_ref,
                   q_out, k_out, v_out):
    hh = MLA_HEADS
    tab = tab_ref[...]
    scale = (MLA_NOPE + MLA_ROPE) ** -0.5 * LOG2_E

    cq = _rms(cq_ref[...].astype(F32), qn_ref[...]).astype(BF16)
    ckv = _rms(ckv_ref[...].astype(F32), kvn_ref[...]).astype(BF16)
    q_all = _dot(cq, wq_ref[...])
    k_all = _dot(ckv, wk_ref[...])
    v_out[...] = _dot_nt(wvt_ref[...], ckv).astype(BF16)

    kr = side_ref[...] * tab
    kpe = (kr + pltpu.roll(kr, shift=MLA_ROPE, axis=1)).astype(BF16)
    rot = tab * scale
    for h in range(hh):
        q_out[:, h * 256:h * 256 + LANES] = (q_all[:, h * 256:h * 256 + LANES] * scale).astype(BF16)
        q_out[:, h * 256 + LANES:(h + 1) * 256] = (q_all[:, h * 256 + LANES:(h + 1) * 256] * rot).astype(BF16)
        k_out[:, h * 256:h * 256 + LANES] = k_all[:, h * LANES:(h + 1) * LANES].astype(BF16)
        k_out[:, h * 256 + LANES:(h + 1) * 256] = kpe


def _mla_up(main, side, tab, q_norm, kv_norm, wq, wk, wvt, *, bm):
    t = main.shape[0]
    r = MLA_RANK
    cq_blk = (GDN_HEADS * GDN_D * 4) // r
    return pl.pallas_call(
        _mla_up_kernel,
        grid=(t // bm,),
        in_specs=[
            pl.BlockSpec((bm, r), lambda i: (i, cq_blk)),
            pl.BlockSpec((bm, r), lambda i: (i, cq_blk + 1)),
            pl.BlockSpec((bm, LANES), lambda i: (i, 0)),
            pl.BlockSpec((bm, LANES), lambda i: (i, 0)),
            pl.BlockSpec((1, r), lambda i: (0, 0)),
            pl.BlockSpec((1, r), lambda i: (0, 0)),
            pl.BlockSpec(wq.shape, lambda i: (0, 0)),
            pl.BlockSpec(wk.shape, lambda i: (0, 0)),
            pl.BlockSpec(wvt.shape, lambda i: (0, 0)),
        ],
        out_specs=[pl.BlockSpec((bm, MLA_HEADS * 256), lambda i: (i, 0)),
                   pl.BlockSpec((bm, MLA_HEADS * 256), lambda i: (i, 0)),
                   pl.BlockSpec((MLA_HEADS * MLA_V, bm), lambda i: (0, i))],
        out_shape=[jax.ShapeDtypeStruct((t, MLA_HEADS * 256), BF16),
                   jax.ShapeDtypeStruct((t, MLA_HEADS * 256), BF16),
                   jax.ShapeDtypeStruct((MLA_HEADS * MLA_V, t), BF16)],
        compiler_params=_params(1), name="mla_up",
    )(main, main, side, tab, q_norm.reshape(1, r), kv_norm.reshape(1, r), wq, wk, wvt)


def _flash_kernel(q_ref, k_ref, vt_ref, o_ref, *, blk, nh):
    qi = pl.program_id(2)
    dq = 2 * LANES
    qs = [q_ref[:, h * dq:(h + 1) * dq] for h in range(nh)]

    def step(j, carry, masked):
        start = pl.multiple_of(j * blk, blk)
        ss = [_dot_nt(k_ref[pl.ds(start, blk), h * dq:(h + 1) * dq], qs[h]) for h in range(nh)]
        if masked:
            key = lax.broadcasted_iota(jnp.int32, ss[0].shape, 0)
            qry = lax.broadcasted_iota(jnp.int32, ss[0].shape, 1)
            ss = [jnp.where(key <= qry, s, -jnp.inf) for s in ss]
        m_new = [jnp.maximum(carry[h][0], jnp.max(ss[h], axis=0, keepdims=True)) for h in range(nh)]
        ps = [jnp.exp2(ss[h] - m_new[h]) for h in range(nh)]
        pv = [_dot(vt_ref[h * MLA_V:(h + 1) * MLA_V, pl.ds(start, blk)], ps[h].astype(BF16)) for h in range(nh)]
        out = []
        for h in range(nh):
            m, l, acc = carry[h]
            alpha = jnp.exp2(m - m_new[h])
            out.append((m_new[h], l * alpha + jnp.sum(ps[h], axis=0, keepdims=True), acc * alpha + pv[h]))
        return tuple(out)

    init = tuple((jnp.full((1, blk), -jnp.inf, F32), jnp.zeros((1, blk), F32),
                  jnp.zeros((MLA_V, blk), F32)) for _ in range(nh))
    carry = lax.fori_loop(0, qi, lambda j, cr: step(j, cr, False), init)
    carry = step(qi, carry, True)
    for h in range(nh):
        _, l, acc = carry[h]
        o_ref[:, h * MLA_V:(h + 1) * MLA_V] = (acc / l).T.astype(o_ref.dtype)


def _flash(qf, kf, vt, *, batch, seq, blk, nh, casts=()):
    t = qf.shape[0]
    nq = seq // blk
    hh = MLA_HEADS
    return _call(
        functools.partial(_flash_kernel, blk=blk, nh=nh),
        grid=(batch, hh // nh, nq),
        in_specs=[
            pl.BlockSpec((blk, nh * 256), lambda b, h, i: (b * nq + i, h)),
            pl.BlockSpec((seq, nh * 256), lambda b, h, i: (b, h)),
            pl.BlockSpec((nh * MLA_V, seq), lambda b, h, i: (h, b)),
        ],
        out_specs=[pl.BlockSpec((blk, nh * MLA_V), lambda b, h, i: (b * nq + i, h))],
        out_shape=[jax.ShapeDtypeStruct((t, hh * MLA_V), BF16)],
        args=(qf, kf, vt), name="mla_flash", casts=casts)


def _ret_kernel(lg_ref, q_ref, k_ref, v_ref, g_ref, nw_ref, o_ref,
                s_ref, dec_ref, xz_ref, *, c, nh):
    @pl.when(pl.program_id(2) == 0)
    def _():
        row = lax.broadcasted_iota(jnp.int32, (c, c), 0)
        col = lax.broadcasted_iota(jnp.int32, (c, c), 1)
        pos = lax.broadcasted_iota(jnp.int32, (c, LANES), 0).astype(F32)
        lane = lax.broadcasted_iota(jnp.int32, (c, LANES), 1)
        for h in range(nh):
            lg = lg_ref[h, :, 0:1]
            dec_ref[h] = jnp.exp(jnp.where(row >= col, (row - col).astype(F32) * lg, -jnp.inf))
            xi = jnp.exp((pos + 1.0) * lg)
            zeta = jnp.exp((c - 1.0 - pos) * lg)
            xz_ref[h] = jnp.where(lane == 0, xi, zeta)
        s_ref[...] = jnp.zeros_like(s_ref)

    hs = range(nh)
    qb = [q_ref[:, h * RET_DK:(h + 1) * RET_DK] for h in hs]
    kb = [k_ref[:, h * RET_DK:(h + 1) * RET_DK] for h in hs]
    v = [v_ref[:, h * RET_DV:(h + 1) * RET_DV] for h in hs]
    qk = [(_dot_nt(qb[h], kb[h]) * dec_ref[h]).astype(BF16) for h in hs]
    states = [s_ref[h] for h in hs]
    cross = [_dot((qb[h].astype(F32) * xz_ref[h, :, 0:1]).astype(BF16), states[h].astype(BF16)) for h in hs]
    inner = [_dot(qk[h], v[h]) for h in hs]
    upd = [_dot_tn((kb[h].astype(F32) * xz_ref[h, :, 1:2]).astype(BF16), v[h]) for h in hs]
    for h in hs:
        s_ref[h] = states[h] * xz_ref[h, c - 1:c, 0:1] + upd[h]
    for h in hs:
        o = inner[h] + cross[h]
        mu = jnp.mean(o, axis=-1, keepdims=True)
        xc = o - mu
        var = jnp.mean(xc * xc, axis=-1, keepdims=True)
        cols = slice(h * RET_DV, (h + 1) * RET_DV)
        y = xc * lax.rsqrt(var + NORM_EPS) * nw_ref[:, cols]
        o_ref[:, cols] = g_ref[:, cols] * y.astype(o_ref.dtype)


def _retention(main, norm_w, *, batch, seq, c, nh, casts=()):
    t = main.shape[0]
    nc = seq // c
    hh = RET_HEADS
    ng = hh // nh
    log_gamma = jnp.log1p(-jnp.power(2.0, -5.0 - jnp.arange(hh, dtype=F32)))
    lg = jnp.broadcast_to(log_gamma[:, None, None], (hh, 1, LANES))
    row = lambda b, h, s: b * nc + s
    kq = RET_DK
    v_blk0 = (2 * hh * kq) // (nh * RET_DV)
    return _call(
        functools.partial(_ret_kernel, c=c, nh=nh),
        grid=(batch, ng, nc),
        in_specs=[
            pl.BlockSpec((nh, 1, LANES), lambda b, h, s: (h, 0, 0)),
            pl.BlockSpec((c, nh * kq), lambda b, h, s: (row(b, h, s), h)),
            pl.BlockSpec((c, nh * kq), lambda b, h, s: (row(b, h, s), ng + h)),
            pl.BlockSpec((c, nh * RET_DV), lambda b, h, s: (row(b, h, s), v_blk0 + h)),
            pl.BlockSpec((c, nh * RET_DV), lambda b, h, s: (row(b, h, s), v_blk0 + ng + h)),
            pl.BlockSpec((1, nh * RET_DV), lambda b, h, s: (0, h)),
        ],
        out_specs=[pl.BlockSpec((c, nh * RET_DV), lambda b, h, s: (row(b, h, s), h))],
        out_shape=[jax.ShapeDtypeStruct((t, hh * RET_DV), BF16)],
        scratch_shapes=[pltpu.VMEM((nh, kq, RET_DV), F32), pltpu.VMEM((nh, c, c), F32),
                        pltpu.VMEM((nh, c, LANES), F32)],
        args=(lg, main, main, main, main, norm_w.reshape(1, -1)), name="retention", casts=casts)


def _rotate_half_cols(w, half):
    return jnp.concatenate([-w[..., half:], w[..., :half]], axis=-1)


def _rope_tab(positions, dim):
    inv_freq = ROPE_THETA ** (-jnp.arange(0, dim, 2, dtype=F32) / dim)
    ang = positions.astype(F32).reshape(-1, 1) * inv_freq
    return jnp.cos(ang), jnp.sin(ang)


def kernel(x, p, positions, l0_attn_norm, l0_w_in, l0_gdn_conv, l0_gdn_A_log, l0_gdn_dt_bias, l0_gdn_norm, l0_mla_q_norm, l0_mla_w_uq, l0_mla_kv_norm, l0_mla_w_ukv, l0_w_out, l0_ffn_norm, l0_ffn_w_up, l0_ffn_conv_w, l0_ffn_conv_b, l0_ffn_w_down, l0_ple_proj, l0_ple_gate_norm, l0_ple_gate, l1_attn_norm, l1_w_in, l1_ret_norm, l1_w_out, l1_ffn_norm, l1_ffn_w_up, l1_ffn_conv_w, l1_ffn_conv_b, l1_ffn_w_down, l1_ple_proj, l1_ple_gate_norm, l1_ple_gate, final_norm):
    batch, seq, d = x.shape
    t = batch * seq
    x2 = x.reshape(t, d)
    p2 = p.reshape(p.shape[0], t, p.shape[-1])

    gq = GDN_HEADS * GDN_D * 4
    n_ab = 2 * GDN_HEADS
    c0 = gq + n_ab
    w_in0 = l0_w_in.astype(BF16)
    w_tail = w_in0[:, c0:c0 + 2 * MLA_RANK]
    w_kr = l0_w_in[:, c0 + 2 * MLA_RANK:]
    w_side = jnp.concatenate(
        [w_kr, _rotate_half_cols(w_kr, MLA_ROPE // 2), l0_w_in[:, gq:c0],
         jnp.zeros((d, LANES - n_ab), F32)], axis=1).astype(BF16)

    wq = l0_mla_w_uq.reshape(MLA_RANK, MLA_HEADS, MLA_NOPE + MLA_ROPE)
    wq_pe = wq[:, :, MLA_NOPE:]
    wq = jnp.concatenate([wq[:, :, :MLA_NOPE], wq_pe, _rotate_half_cols(wq_pe, MLA_ROPE // 2)], axis=-1)
    wq = wq.reshape(MLA_RANK, MLA_HEADS * 256).astype(BF16)
    wkv = l0_mla_w_ukv.reshape(MLA_RANK, MLA_HEADS, MLA_NOPE + MLA_V)
    wk = wkv[:, :, :MLA_NOPE].reshape(MLA_RANK, -1).astype(BF16)
    wvt = wkv[:, :, MLA_NOPE:].reshape(MLA_RANK, -1).T.astype(BF16)

    cos_m, sin_m = _rope_tab(positions, MLA_ROPE)
    tab_m = jnp.concatenate([cos_m, cos_m, sin_m, sin_m], axis=1)
    tab_r = jnp.concatenate(_rope_tab(positions, RET_DK), axis=1)

    main0, side0 = _in_proj0(x2, l0_attn_norm, w_in0, w_tail, w_side, l0_gdn_conv, seq=seq, bm=1024, bn=1024)
    y_a = _gdn(main0, side0, l0_gdn_A_log, l0_gdn_dt_bias, l0_gdn_norm,
               batch=batch, seq=seq, lb=256, nh=8)
    qf, kf, vt = _mla_up(main0, side0, tab_m, l0_mla_q_norm, l0_mla_kv_norm, wq, wk, wvt, bm=512)
    y_b, w_up = _flash(qf, kf, vt, batch=batch, seq=seq, blk=512, nh=4, casts=[l0_ffn_w_up])
    w_out0 = l0_w_out.astype(BF16)
    ha = GDN_HEADS * GDN_D
    (h,) = _proj_res([(y_a, w_out0[:ha]), (y_b, w_out0[ha:])], x2, bm=1024, bn=1024)
    act, w_down = _ffn_up(h, l0_ffn_norm, w_up, l0_ffn_conv_w, l0_ffn_conv_b, seq=seq, bm=1024, bn=512,
                          casts=[l0_ffn_w_down])
    h, w_gate = _proj_res([(act, w_down)], h, bm=1024, bn=512, casts=[l0_ple_gate])
    h, w_in1, w_out1 = _ple(h, p2, 0, l0_ple_gate_norm, l0_ple_proj.astype(BF16), w_gate, bm=512, bn=512,
                            casts=[l1_w_in, l1_w_out])

    main1 = _in_proj1(h, l1_attn_norm, w_in1, tab_r, bm=1024, bn=1024)
    y, w_up = _retention(main1, l1_ret_norm, batch=batch, seq=seq, c=256, nh=8, casts=[l1_ffn_w_up])
    (h,) = _proj_res([(y, w_out1)], h, bm=1024, bn=1024)
    act, w_down = _ffn_up(h, l1_ffn_norm, w_up, l1_ffn_conv_w, l1_ffn_conv_b, seq=seq, bm=1024, bn=512,
                          casts=[l1_ffn_w_down])
    h, w_gate = _proj_res([(act, w_down)], h, bm=1024, bn=512, casts=[l1_ple_gate])
    (h,) = _ple(h, p2, 1, l1_ple_gate_norm, l1_ple_proj.astype(BF16), w_gate, final_norm, bm=512, bn=512)
    return h.reshape(batch, seq, d)
```

```python
import functools

import jax
import jax.numpy as jnp
from jax import lax
from jax.experimental import pallas as pl
from jax.experimental.pallas import tpu as pltpu

F32 = jnp.float32
BF16 = jnp.bfloat16

NORM_EPS = 1e-6
ROPE_THETA = 10000.0

GDN_HEADS = 8
GDN_D = 128
GDN_CONV = 4
GDN_CHUNK = 64
MLA_HEADS = 8
MLA_RANK = 512
MLA_NOPE = 128
MLA_ROPE = 64
MLA_V = 128
RET_HEADS = 8
RET_DK = 256
RET_DV = 512
FFN_CONV = 3

LANES = 128
VMEM_LIMIT = 56 * 1024 * 1024
LOG2_E = 1.4426950408889634
CONV_HALO = 16


def _params(n_grid):
    return pltpu.CompilerParams(dimension_semantics=("arbitrary",) * n_grid,
                                vmem_limit_bytes=VMEM_LIMIT)


def _rms(x, g):
    return x * lax.rsqrt(jnp.mean(x * x, axis=-1, keepdims=True) + NORM_EPS) * g


def _dot(a, b):
    return jnp.dot(a, b, preferred_element_type=F32)


def _dot_nt(a, b):
    return lax.dot_general(a, b, (((1,), (1,)), ((), ())), preferred_element_type=F32)


def _dot_tn(a, b):
    return lax.dot_general(a, b, (((0,), (0,)), ((), ())), preferred_element_type=F32)


def _silu(x):
    return x * jax.nn.sigmoid(x)


def _call(kernel_fn, *, grid, in_specs, out_specs, out_shape, args, name, scratch_shapes=(), casts=()):
    n_in, n_out = len(in_specs), len(out_specs)
    n_steps = 1
    for g in grid:
        n_steps *= g

    def flat_step(*idx):
        step = idx[0]
        for g, i in zip(grid[1:], idx[1:]):
            step = step * g + i
        return step

    in_specs, out_specs, out_shape, args = list(in_specs), list(out_specs), list(out_shape), list(args)
    for w in casts:
        rows, cols = w.shape[0] // n_steps, w.shape[1]
        assert rows * n_steps == w.shape[0] and rows % 16 == 0, (w.shape, n_steps)
        spec = pl.BlockSpec((None, rows, cols), lambda *idx: (flat_step(*idx), 0, 0))
        in_specs.append(spec)
        out_specs.append(spec)
        out_shape.append(jax.ShapeDtypeStruct((n_steps, rows, cols), BF16))
        args.append(w.reshape(n_steps, rows, cols))

    def body(*refs):
        ins, extra_in = refs[:n_in], refs[n_in:n_in + len(casts)]
        outs = refs[n_in + len(casts):n_in + len(casts) + n_out]
        extra_out = refs[n_in + len(casts) + n_out:n_in + 2 * len(casts) + n_out]
        scratch = refs[n_in + 2 * len(casts) + n_out:]
        for src, dst in zip(extra_in, extra_out):
            dst[...] = src[...].astype(BF16)
        kernel_fn(*ins, *outs, *scratch)

    out = pl.pallas_call(
        body, grid=grid, in_specs=in_specs, out_specs=out_specs, out_shape=out_shape,
        scratch_shapes=list(scratch_shapes), compiler_params=_params(len(grid)), name=name,
    )(*args)
    return list(out[:n_out]) + [o.reshape(w.shape) for o, w in zip(out[n_out:], casts)]


def _in_proj0_kernel(x_ref, xh_ref, g_ref, w_ref, wt_ref, ws_ref, cw_ref, o_ref, os_ref, xn_ref, u_ref,
                     *, bm, blocks_per_seq, n_conv_blocks, n_main_blocks):
    i = pl.program_id(0)
    j = pl.program_id(1)

    @pl.when(j == 0)
    def _():
        g = g_ref[...]
        halo = _rms(xh_ref[...], g)
        halo = jnp.where(i % blocks_per_seq == 0, 0.0, halo)
        xn_ref[0:CONV_HALO, :] = halo.astype(BF16)
        xn = _rms(x_ref[...], g).astype(BF16)
        xn_ref[CONV_HALO:, :] = xn
        os_ref[...] = _dot(xn, ws_ref[...])

    def conv_silu():
        u_ref[...] = _dot(xn_ref[...], w_ref[...])
        cw = cw_ref[...]
        y = cw[GDN_CONV - 1:GDN_CONV] * u_ref[pl.ds(CONV_HALO, bm), :]
        for s in range(1, GDN_CONV):
            y = y + cw[GDN_CONV - 1 - s:GDN_CONV - s] * u_ref[pl.ds(CONV_HALO - s, bm), :]
        return _silu(y)

    @pl.when(j < n_conv_blocks - 1)
    def _():
        y = conv_silu()
        scale = jnp.where(j == 0, GDN_D ** -0.5, 1.0)
        for h in range(y.shape[1] // GDN_D):
            cols = slice(h * GDN_D, (h + 1) * GDN_D)
            yh = y[:, cols]
            inv = lax.rsqrt(jnp.sum(yh * yh, axis=-1, keepdims=True) + NORM_EPS) * scale
            o_ref[:, cols] = (yh * inv).astype(o_ref.dtype)

    @pl.when(j == n_conv_blocks - 1)
    def _():
        o_ref[...] = conv_silu().astype(o_ref.dtype)

    @pl.when((j >= n_conv_blocks) & (j < n_main_blocks))
    def _():
        o_ref[...] = _dot(xn_ref[CONV_HALO:, :], w_ref[...]).astype(o_ref.dtype)

    @pl.when(j >= n_main_blocks)
    def _():
        o_ref[...] = _dot(xn_ref[CONV_HALO:, :], wt_ref[...]).astype(o_ref.dtype)


def _in_proj0(x, g, w, w_tail, w_side, conv_w, *, seq, bm, bn):
    t, d = x.shape
    ns = w_side.shape[1]
    n_conv_blocks = conv_w.shape[1] // bn
    n_main_blocks = (GDN_HEADS * GDN_D * 4) // bn
    n_tail_blocks = w_tail.shape[1] // bn
    n = (n_main_blocks + n_tail_blocks) * bn
    hb = bm // CONV_HALO
    kern = functools.partial(_in_proj0_kernel, bm=bm, blocks_per_seq=seq // bm, n_conv_blocks=n_conv_blocks,
                             n_main_blocks=n_main_blocks)
    return pl.pallas_call(
        kern,
        grid=(t // bm, n_main_blocks + n_tail_blocks),
        in_specs=[
            pl.BlockSpec((bm, d), lambda i, j: (i, 0)),
            pl.BlockSpec((CONV_HALO, d), lambda i, j: (jnp.maximum(i * hb - 1, 0), 0)),
            pl.BlockSpec((1, d), lambda i, j: (0, 0)),
            pl.BlockSpec((d, bn), lambda i, j: (0, jnp.minimum(j, n_main_blocks - 1))),
            pl.BlockSpec((d, bn), lambda i, j: (0, jnp.maximum(j - n_main_blocks, 0))),
            pl.BlockSpec((d, ns), lambda i, j: (0, 0)),
            pl.BlockSpec((GDN_CONV, bn), lambda i, j: (0, jnp.minimum(j, n_conv_blocks - 1))),
        ],
        out_specs=[pl.BlockSpec((bm, bn), lambda i, j: (i, j)),
                   pl.BlockSpec((bm, ns), lambda i, j: (i, 0))],
        out_shape=[jax.ShapeDtypeStruct((t, n), BF16), jax.ShapeDtypeStruct((t, ns), F32)],
        scratch_shapes=[pltpu.VMEM((bm + CONV_HALO, d), BF16), pltpu.VMEM((bm + CONV_HALO, bn), F32)],
        compiler_params=_params(2), name="in_proj0",
    )(x, x, g.reshape(1, d), w, w_tail, w_side, conv_w)


def _in_proj1_kernel(x_ref, g_ref, w_ref, cs_ref, o_ref, xn_ref, *, n_q, n_k, n_v):
    j = pl.program_id(1)
    half = RET_DK // 2

    @pl.when(j == 0)
    def _():
        xn_ref[...] = _rms(x_ref[...], g_ref[...]).astype(BF16)

    def proj():
        return _dot(xn_ref[...], w_ref[...])

    def rope_store(scale):
        y = proj()
        cos = cs_ref[:, :half]
        sin = cs_ref[:, half:]
        for h in range(y.shape[1] // RET_DK):
            x1 = y[:, h * RET_DK:h * RET_DK + half]
            x2 = y[:, h * RET_DK + half:(h + 1) * RET_DK]
            o_ref[:, h * RET_DK:h * RET_DK + half] = ((x1 * cos - x2 * sin) * scale).astype(o_ref.dtype)
            o_ref[:, h * RET_DK + half:(h + 1) * RET_DK] = ((x2 * cos + x1 * sin) * scale).astype(o_ref.dtype)

    @pl.when(j < n_q)
    def _():
        rope_store(1.0)

    @pl.when((j >= n_q) & (j < n_q + n_k))
    def _():
        rope_store(RET_DK ** -0.5)

    @pl.when((j >= n_q + n_k) & (j < n_q + n_k + n_v))
    def _():
        o_ref[...] = proj().astype(o_ref.dtype)

    @pl.when(j >= n_q + n_k + n_v)
    def _():
        o_ref[...] = _silu(proj()).astype(o_ref.dtype)


def _in_proj1(x, g, w, tab, *, bm, bn):
    t, d = x.shape
    n = w.shape[1]
    n_q = RET_HEADS * RET_DK // bn
    n_v = RET_HEADS * RET_DV // bn
    return pl.pallas_call(
        functools.partial(_in_proj1_kernel, n_q=n_q, n_k=n_q, n_v=n_v),
        grid=(t // bm, n // bn),
        in_specs=[pl.BlockSpec((bm, d), lambda i, j: (i, 0)),
                  pl.BlockSpec((1, d), lambda i, j: (0, 0)),
                  pl.BlockSpec((d, bn), lambda i, j: (0, j)),
                  pl.BlockSpec((bm, RET_DK), lambda i, j: (i, 0))],
        out_specs=pl.BlockSpec((bm, bn), lambda i, j: (i, j)),
        out_shape=jax.ShapeDtypeStruct((t, n), BF16),
        scratch_shapes=[pltpu.VMEM((bm, d), BF16)],
        compiler_params=_params(2), name="in_proj1",
    )(x, g.reshape(1, d), w, tab)


def _proj_res_kernel(*refs, n_pairs):
    res_ref, o_ref = refs[2 * n_pairs], refs[2 * n_pairs + 1]
    acc = res_ref[...]
    for p in range(n_pairs):
        acc = acc + _dot(refs[2 * p][...], refs[2 * p + 1][...])
    o_ref[...] = acc


def _proj_res(pairs, res, *, bm, bn, casts=()):
    t, n = res.shape
    in_specs, args = [], []
    for y, w in pairs:
        kk = y.shape[1]
        in_specs += [pl.BlockSpec((bm, kk), lambda i, j: (i, 0)),
                     pl.BlockSpec((kk, bn), lambda i, j: (0, j))]
        args += [y, w]
    in_specs.append(pl.BlockSpec((bm, bn), lambda i, j: (i, j)))
    args.append(res)
    return _call(
        functools.partial(_proj_res_kernel, n_pairs=len(pairs)),
        grid=(t // bm, n // bn),
        in_specs=in_specs,
        out_specs=[pl.BlockSpec((bm, bn), lambda i, j: (i, j))],
        out_shape=[jax.ShapeDtypeStruct((t, n), F32)],
        args=args, name="proj_res", casts=casts)


def _ffn_up_kernel(h_ref, hh_ref, g_ref, wg_ref, wu_ref, cg_ref, cu_ref, bg_ref, bu_ref,
                   o_ref, xn_ref, ug_ref, uu_ref, *, bm, blocks_per_seq):
    i = pl.program_id(0)

    @pl.when(pl.program_id(1) == 0)
    def _():
        g = g_ref[...]
        halo = _rms(hh_ref[...], g)
        halo = jnp.where(i % blocks_per_seq == 0, 0.0, halo)
        xn_ref[0:CONV_HALO, :] = halo.astype(BF16)
        xn_ref[CONV_HALO:, :] = _rms(h_ref[...], g).astype(BF16)

    xn = xn_ref[...]
    ug_ref[...] = _dot(xn, wg_ref[...])
    uu_ref[...] = _dot(xn, wu_ref[...])

    def conv(u_ref, c_ref, b_ref):
        cw = c_ref[...]
        y = b_ref[...] + cw[FFN_CONV - 1:FFN_CONV] * u_ref[pl.ds(CONV_HALO, bm), :]
        for s in range(1, FFN_CONV):
            y = y + cw[FFN_CONV - 1 - s:FFN_CONV - s] * u_ref[pl.ds(CONV_HALO - s, bm), :]
        return y

    gate = conv(ug_ref, cg_ref, bg_ref)
    up = conv(uu_ref, cu_ref, bu_ref)
    o_ref[...] = (_silu(gate) * up).astype(o_ref.dtype)


def _ffn_up(h, g, w_up, conv_w, conv_b, *, seq, bm, bn, casts=()):
    t, d = h.shape
    dff = w_up.shape[1] // 2
    nj = dff // bn
    hb = bm // CONV_HALO
    kern = functools.partial(_ffn_up_kernel, bm=bm, blocks_per_seq=seq // bm)
    return _call(
        kern,
        grid=(t // bm, nj),
        in_specs=[
            pl.BlockSpec((bm, d), lambda i, j: (i, 0)),
            pl.BlockSpec((CONV_HALO, d), lambda i, j: (jnp.maximum(i * hb - 1, 0), 0)),
            pl.BlockSpec((1, d), lambda i, j: (0, 0)),
            pl.BlockSpec((d, bn), lambda i, j: (0, j)),
            pl.BlockSpec((d, bn), lambda i, j: (0, j + nj)),
            pl.BlockSpec((FFN_CONV, bn), lambda i, j: (0, j)),
            pl.BlockSpec((FFN_CONV, bn), lambda i, j: (0, j + nj)),
            pl.BlockSpec((1, bn), lambda i, j: (0, j)),
            pl.BlockSpec((1, bn), lambda i, j: (0, j + nj)),
        ],
        out_specs=[pl.BlockSpec((bm, bn), lambda i, j: (i, j))],
        out_shape=[jax.ShapeDtypeStruct((t, dff), BF16)],
        scratch_shapes=[pltpu.VMEM((bm + CONV_HALO, d), BF16),
                        pltpu.VMEM((bm + CONV_HALO, bn), F32),
                        pltpu.VMEM((bm + CONV_HALO, bn), F32)],
        args=(h, h, g.reshape(1, d), w_up, w_up, conv_w, conv_w,
              conv_b.reshape(1, -1), conv_b.reshape(1, -1)),
        name="ffn_up", casts=casts)


def _ple_kernel(h_ref, p_ref, gn_ref, wp_ref, wg_ref, *rest, bn, final):
    if final:
        fn_ref, o_ref = rest
    else:
        (o_ref,) = rest
    d = h_ref.shape[1]
    hn = _rms(h_ref[...], gn_ref[...]).astype(BF16)
    pb = p_ref[...].astype(BF16)
    ssq = None
    for c in range(d // bn):
        cols = slice(c * bn, (c + 1) * bn)
        gate = _dot(hn, wg_ref[:, cols])
        proj = _dot(pb, wp_ref[:, cols])
        out = h_ref[:, cols] + proj * jax.nn.sigmoid(gate)
        o_ref[:, cols] = out
        if final:
            part = jnp.sum(out * out, axis=-1, keepdims=True)
            ssq = part if ssq is None else ssq + part
    if final:
        scale = lax.rsqrt(ssq / d + NORM_EPS)
        o_ref[...] = o_ref[...] * scale * fn_ref[...]


def _ple(h, p, layer, gate_norm, w_proj, w_gate, final_norm=None, *, bm, bn, casts=()):
    t, d = h.shape
    dp = p.shape[-1]
    final = final_norm is not None
    in_specs = [pl.BlockSpec((bm, d), lambda i: (i, 0)),
                pl.BlockSpec((None, bm, dp), lambda i: (layer, i, 0)),
                pl.BlockSpec((1, d), lambda i: (0, 0)),
                pl.BlockSpec((dp, d), lambda i: (0, 0)),
                pl.BlockSpec((d, d), lambda i: (0, 0))]
    args = [h, p, gate_norm.reshape(1, d), w_proj, w_gate]
    if final:
        in_specs.append(pl.BlockSpec((1, d), lambda i: (0, 0)))
        args.append(final_norm.reshape(1, d))
    return _call(
        functools.partial(_ple_kernel, bn=bn, final=final),
        grid=(t // bm,),
        in_specs=in_specs,
        out_specs=[pl.BlockSpec((bm, d), lambda i: (i, 0))],
        out_shape=[jax.ShapeDtypeStruct((t, d), F32)],
        args=args, name="ple", casts=casts)


def _split3(x):
    hi = x.astype(BF16).astype(F32)
    r1 = x - hi
    mid = r1.astype(BF16).astype(F32)
    lo = (r1 - mid).astype(BF16).astype(F32)
    return hi, mid, lo


def _lane_pick(x, lane, idx):
    return jnp.sum(jnp.where(lane == idx, x, 0.0), axis=-1, keepdims=True)


def _gdn_kernel(q_ref, k_ref, v_ref, z_ref, ab_ref, hp_ref, nw_ref, tril_ref, o_ref, s_ref, *, lb, nh):
    c = GDN_CHUNK
    n_chunks = lb // c
    dk = GDN_D

    @pl.when(pl.program_id(2) == 0)
    def _():
        s_ref[...] = jnp.zeros_like(s_ref)

    ab = ab_ref[...]
    lane = lax.broadcasted_iota(jnp.int32, ab.shape, 1)
    hp = hp_ref[...]
    xa = ab + hp[1:2]
    softplus = jnp.maximum(xa, 0.0) + jnp.log1p(jnp.exp(-jnp.abs(xa)))
    g_lanes = -jnp.exp(hp[0:1]) * softplus
    beta_lanes = jax.nn.sigmoid(ab)

    def pieces_to_lanes(x, first, sign):
        hi, mid, lo = _split3(x)
        out = jnp.where(lane == first, sign * hi, 0.0)
        out = jnp.where(lane == first + 1, sign * mid, out)
        return jnp.where(lane == first + 2, sign * lo, out)

    row2 = lax.broadcasted_iota(jnp.int32, (c, 2 * c), 0)
    lane2 = lax.broadcasted_iota(jnp.int32, (c, 2 * c), 1)
    col2 = lane2 & (c - 1)
    incl2 = row2 >= col2
    strict2 = row2 > col2
    left = lane2 < c
    eye_left = jnp.where(left & (row2 == lane2), 1.0, 0.0)
    zeros_k = jnp.zeros((c, 2 * c), BF16)
    zeros_x = jnp.zeros((c, 2 * dk), BF16)

    items = []
    for hi in range(nh):
        head = pl.program_id(1) * nh + hi
        cols = slice(hi * dk, (hi + 1) * dk)
        qb = q_ref[:, cols]
        kb16 = k_ref[:, cols]
        q = qb.astype(F32)
        k = kb16.astype(F32)
        v = v_ref[:, cols].astype(F32)
        g_col = _lane_pick(g_lanes, lane, head)
        beta = _lane_pick(beta_lanes, lane, head + GDN_HEADS)
        gm = _dot(tril_ref[...], pieces_to_lanes(g_col, 0, 1.0).astype(BF16))
        big_g = jnp.sum(gm, axis=-1, keepdims=True)
        e_g = jnp.exp(big_g)
        am = jnp.where(lane < 3, pieces_to_lanes(big_g, 0, 1.0), jnp.where(lane < 6, 1.0, 0.0)).astype(BF16)
        bmat = jnp.where(lane < 3, 1.0, pieces_to_lanes(big_g, 3, -1.0)).astype(BF16)
        kbeta = k * beta
        x0 = jnp.concatenate([v * beta, kbeta * e_g], axis=-1).astype(BF16)
        qd = q * e_g
        kbeta16 = kbeta.astype(BF16)
        for n in range(n_chunks):
            r = slice(n * c, (n + 1) * c)
            g_last = big_g[(n + 1) * c - 1:(n + 1) * c, :]
            items.append(dict(
                am=am[r], bm2=jnp.concatenate([bmat[r], bmat[r]], axis=0),
                kb=kbeta16[r], k=kb16[r], k2=jnp.concatenate([kb16[r], kb16[r]], axis=0), q=qb[r],
                x0=jnp.concatenate([x0[r], zeros_x], axis=0), qd=qd[r],
                kd=(k[r] * jnp.exp(g_last - big_g[r])).astype(BF16), gl=jnp.exp(g_last)))

    for it in items:
        it["gamma2"] = jnp.exp(jnp.where(incl2, _dot_nt(it["am"], it["bm2"]), -jnp.inf))
    for it in items:
        it["tp"] = jnp.where(left, eye_left, jnp.where(strict2, -_dot_nt(it["kb"], it["k2"]) * it["gamma2"], 0.0))
    for it in items:
        it["attn"] = (_dot_nt(it["q"], it["k"]) * it["gamma2"][:, :c]).astype(BF16)

    for j in range(6):
        for it in items:
            it["tpb"] = it["tp"].astype(BF16)
        for it in items:
            upd = _dot(it["tpb"], jnp.concatenate([zeros_k, it["tpb"]], axis=0))
            it["tp"] = jnp.where(left, it["tp"] + upd, upd)
    for it in items:
        it["tpb"] = it["tp"].astype(BF16)
    for it in items:
        it["xb"] = _dot(it["tpb"], it["x0"]).astype(BF16)

    for it in items:
        it["au"] = _dot(it["attn"], it["xb"])
    for it in items:
        it["ku"] = _dot_tn(it["kd"], it["xb"])
    for it in items:
        it["lhs"] = jnp.concatenate([it["qd"] - it["au"][:, dk:], it["ku"][:, dk:]], axis=0).astype(BF16)

    states = [s_ref[hi] for hi in range(nh)]
    outs = [[None] * n_chunks for _ in range(nh)]
    for n in range(n_chunks):
        for hi in range(nh):
            it = items[hi * n_chunks + n]
            prod = _dot(it["lhs"], states[hi].astype(BF16))
            outs[hi][n] = prod[:c] + it["au"][:, :dk]
            states[hi] = states[hi] * it["gl"] - prod[c:] + it["ku"][:, :dk]
    for hi in range(nh):
        s_ref[hi] = states[hi]

    nw = nw_ref[...]
    for hi in range(nh):
        cols = slice(hi * dk, (hi + 1) * dk)
        o = jnp.concatenate(outs[hi], axis=0)
        o_ref[:, cols] = (_rms(o, nw) * _silu(z_ref[:, cols].astype(F32))).astype(o_ref.dtype)


def _gdn(main, side, a_log, dt_bias, norm_w, *, batch, seq, lb, nh):
    t = main.shape[0]
    ns = seq // lb
    hh = GDN_HEADS
    ng = hh // nh
    w = nh * GDN_D
    hp = jnp.zeros((8, LANES), F32)
    hp = hp.at[0, :hh].set(a_log).at[1, :hh].set(dt_bias)
    r = jnp.arange(lb)
    tril = ((r[:, None] >= r[None, :]) & (r[:, None] // GDN_CHUNK == r[None, :] // GDN_CHUNK)).astype(BF16)
    row = lambda b, h, s: b * ns + s
    return pl.pallas_call(
        functools.partial(_gdn_kernel, lb=lb, nh=nh),
        grid=(batch, ng, ns),
        in_specs=[
            pl.BlockSpec((lb, w), lambda b, h, s: (row(b, h, s), h)),
            pl.BlockSpec((lb, w), lambda b, h, s: (row(b, h, s), ng + h)),
            pl.BlockSpec((lb, w), lambda b, h, s: (row(b, h, s), 2 * ng + h)),
            pl.BlockSpec((lb, w), lambda b, h, s: (row(b, h, s), 3 * ng + h)),
            pl.BlockSpec((lb, LANES), lambda b, h, s: (row(b, h, s), 1)),
            pl.BlockSpec((8, LANES), lambda b, h, s: (0, 0)),
            pl.BlockSpec((1, GDN_D), lambda b, h, s: (0, 0)),
            pl.BlockSpec((lb, lb), lambda b, h, s: (0, 0)),
        ],
        out_specs=pl.BlockSpec((lb, w), lambda b, h, s: (row(b, h, s), h)),
        out_shape=jax.ShapeDtypeStruct((t, hh * GDN_D), BF16),
        scratch_shapes=[pltpu.VMEM((nh, GDN_D, GDN_D), F32)],
        compiler_params=_params(3), name="gdn",
    )(main, main, main, main, side, hp, norm_w.reshape(1, GDN_D), tril)


def _mla_up_kernel(cq_ref, ckv_ref, side_ref, tab_ref, qn_ref, kvn_ref, wq_ref, wk_ref, wvt_ref,
                   q_out, k_out, v_out):
    hh = MLA_HEADS
    tab = tab_ref[...]
    scale = (MLA_NOPE + MLA_ROPE) ** -0.5 * LOG2_E

    cq = _rms(cq_ref[...].astype(F32), qn_ref[...]).astype(BF16)
    ckv = _rms(ckv_ref[...].astype(F32), kvn_ref[...]).astype(BF16)
    q_all = _dot(cq, wq_ref[...])
    k_all = _dot(ckv, wk_ref[...])
    v_out[...] = _dot_nt(wvt_ref[...], ckv).astype(BF16)

    kr = side_ref[...] * tab
    kpe = (kr + pltpu.roll(kr, shift=MLA_ROPE, axis=1)).astype(BF16)
    rot = tab * scale
    for h in range(hh):
        q_out[:, h * 256:h * 256 + LANES] = (q_all[:, h * 256:h * 256 + LANES] * scale).astype(BF16)
        q_out[:, h * 256 + LANES:(h + 1) * 256] = (q_all[:, h * 256 + LANES:(h + 1) * 256] * rot).astype(BF16)
        k_out[:, h * 256:h * 256 + LANES] = k_all[:, h * LANES:(h + 1) * LANES].astype(BF16)
        k_out[:, h * 256 + LANES:(h + 1) * 256] = kpe


def _mla_up(main, side, tab, q_norm, kv_norm, wq, wk, wvt, *, bm):
    t = main.shape[0]
    r = MLA_RANK
    cq_blk = (GDN_HEADS * GDN_D * 4) // r
    return pl.pallas_call(
        _mla_up_kernel,
        grid=(t // bm,),
        in_specs=[
            pl.BlockSpec((bm, r), lambda i: (i, cq_blk)),
            pl.BlockSpec((bm, r), lambda i: (i, cq_blk + 1)),
            pl.BlockSpec((bm, LANES), lambda i: (i, 0)),
            pl.BlockSpec((bm, LANES), lambda i: (i, 0)),
            pl.BlockSpec((1, r), lambda i: (0, 0)),
            pl.BlockSpec((1, r), lambda i: (0, 0)),
            pl.BlockSpec(wq.shape, lambda i: (0, 0)),
            pl.BlockSpec(wk.shape, lambda i: (0, 0)),
            pl.BlockSpec(wvt.shape, lambda i: (0, 0)),
        ],
        out_specs=[pl.BlockSpec((bm, MLA_HEADS * 256), lambda i: (i, 0)),
                   pl.BlockSpec((bm, MLA_HEADS * 256), lambda i: (i, 0)),
                   pl.BlockSpec((MLA_HEADS * MLA_V, bm), lambda i: (0, i))],
        out_shape=[jax.ShapeDtypeStruct((t, MLA_HEADS * 256), BF16),
                   jax.ShapeDtypeStruct((t, MLA_HEADS * 256), BF16),
                   jax.ShapeDtypeStruct((MLA_HEADS * MLA_V, t), BF16)],
        compiler_params=_params(1), name="mla_up",
    )(main, main, side, tab, q_norm.reshape(1, r), kv_norm.reshape(1, r), wq, wk, wvt)


def _flash_kernel(q_ref, k_ref, vt_ref, o_ref, *, blk, nh):
    qi = pl.program_id(2)
    dq = 2 * LANES
    qs = [q_ref[:, h * dq:(h + 1) * dq] for h in range(nh)]

    def step(j, carry, masked):
        start = pl.multiple_of(j * blk, blk)
        ss = [_dot_nt(k_ref[pl.ds(start, blk), h * dq:(h + 1) * dq], qs[h]) for h in range(nh)]
        if masked:
            key = lax.broadcasted_iota(jnp.int32, ss[0].shape, 0)
            qry = lax.broadcasted_iota(jnp.int32, ss[0].shape, 1)
            ss = [jnp.where(key <= qry, s, -jnp.inf) for s in ss]
        m_new = [jnp.maximum(carry[h][0], jnp.max(ss[h], axis=0, keepdims=True)) for h in range(nh)]
        ps = [jnp.exp2(ss[h] - m_new[h]) for h in range(nh)]
        pv = [_dot(vt_ref[h * MLA_V:(h + 1) * MLA_V, pl.ds(start, blk)], ps[h].astype(BF16)) for h in range(nh)]
        out = []
        for h in range(nh):
            m, l, acc = carry[h]
            alpha = jnp.exp2(m - m_new[h])
            out.append((m_new[h], l * alpha + jnp.sum(ps[h], axis=0, keepdims=True), acc * alpha + pv[h]))
        return tuple(out)

    init = tuple((jnp.full((1, blk), -jnp.inf, F32), jnp.zeros((1, blk), F32),
                  jnp.zeros((MLA_V, blk), F32)) for _ in range(nh))
    carry = lax.fori_loop(0, qi, lambda j, cr: step(j, cr, False), init)
    carry = step(qi, carry, True)
    for h in range(nh):
        _, l, acc = carry[h]
        o_ref[:, h * MLA_V:(h + 1) * MLA_V] = (acc / l).T.astype(o_ref.dtype)


def _flash(qf, kf, vt, *, batch, seq, blk, nh, casts=()):
    t = qf.shape[0]
    nq = seq // blk
    hh = MLA_HEADS
    return _call(
        functools.partial(_flash_kernel, blk=blk, nh=nh),
        grid=(batch, hh // nh, nq),
        in_specs=[
            pl.BlockSpec((blk, nh * 256), lambda b, h, i: (b * nq + i, h)),
            pl.BlockSpec((seq, nh * 256), lambda b, h, i: (b, h)),
            pl.BlockSpec((nh * MLA_V, seq), lambda b, h, i: (h, b)),
        ],
        out_specs=[pl.BlockSpec((blk, nh * MLA_V), lambda b, h, i: (b * nq + i, h))],
        out_shape=[jax.ShapeDtypeStruct((t, hh * MLA_V), BF16)],
        args=(qf, kf, vt), name="mla_flash", casts=casts)


def _ret_kernel(lg_ref, q_ref, k_ref, v_ref, g_ref, nw_ref, o_ref,
                s_ref, dec_ref, xz_ref, *, c, nh):
    @pl.when(pl.program_id(2) == 0)
    def _():
        row = lax.broadcasted_iota(jnp.int32, (c, c), 0)
        col = lax.broadcasted_iota(jnp.int32, (c, c), 1)
        pos = lax.broadcasted_iota(jnp.int32, (c, LANES), 0).astype(F32)
        lane = lax.broadcasted_iota(jnp.int32, (c, LANES), 1)
        for h in range(nh):
            lg = lg_ref[h, :, 0:1]
            dec_ref[h] = jnp.exp(jnp.where(row >= col, (row - col).astype(F32) * lg, -jnp.inf))
            xi = jnp.exp((pos + 1.0) * lg)
            zeta = jnp.exp((c - 1.0 - pos) * lg)
            xz_ref[h] = jnp.where(lane == 0, xi, zeta)
        s_ref[...] = jnp.zeros_like(s_ref)

    hs = range(nh)
    qb = [q_ref[:, h * RET_DK:(h + 1) * RET_DK] for h in hs]
    kb = [k_ref[:, h * RET_DK:(h + 1) * RET_DK] for h in hs]
    v = [v_ref[:, h * RET_DV:(h + 1) * RET_DV] for h in hs]
    qk = [(_dot_nt(qb[h], kb[h]) * dec_ref[h]).astype(BF16) for h in hs]
    states = [s_ref[h] for h in hs]
    cross = [_dot((qb[h].astype(F32) * xz_ref[h, :, 0:1]).astype(BF16), states[h].astype(BF16)) for h in hs]
    inner = [_dot(qk[h], v[h]) for h in hs]
    upd = [_dot_tn((kb[h].astype(F32) * xz_ref[h, :, 1:2]).astype(BF16), v[h]) for h in hs]
    for h in hs:
        s_ref[h] = states[h] * xz_ref[h, c - 1:c, 0:1] + upd[h]
    for h in hs:
        o = inner[h] + cross[h]
        mu = jnp.mean(o, axis=-1, keepdims=True)
        xc = o - mu
        var = jnp.mean(xc * xc, axis=-1, keepdims=True)
        cols = slice(h * RET_DV, (h + 1) * RET_DV)
        y = xc * lax.rsqrt(var + NORM_EPS) * nw_ref[:, cols]
        o_ref[:, cols] = g_ref[:, cols] * y.astype(o_ref.dtype)


def _retention(main, norm_w, *, batch, seq, c, nh, casts=()):
    t = main.shape[0]
    nc = seq // c
    hh = RET_HEADS
    ng = hh // nh
    log_gamma = jnp.log1p(-jnp.power(2.0, -5.0 - jnp.arange(hh, dtype=F32)))
    lg = jnp.broadcast_to(log_gamma[:, None, None], (hh, 1, LANES))
    row = lambda b, h, s: b * nc + s
    kq = RET_DK
    v_blk0 = (2 * hh * kq) // (nh * RET_DV)
    return _call(
        functools.partial(_ret_kernel, c=c, nh=nh),
        grid=(batch, ng, nc),
        in_specs=[
            pl.BlockSpec((nh, 1, LANES), lambda b, h, s: (h, 0, 0)),
            pl.BlockSpec((c, nh * kq), lambda b, h, s: (row(b, h, s), h)),
            pl.BlockSpec((c, nh * kq), lambda b, h, s: (row(b, h, s), ng + h)),
            pl.BlockSpec((c, nh * RET_DV), lambda b, h, s: (row(b, h, s), v_blk0 + h)),
            pl.BlockSpec((c, nh * RET_DV), lambda b, h, s: (row(b, h, s), v_blk0 + ng + h)),
            pl.BlockSpec((1, nh * RET_DV), lambda b, h, s: (0, h)),
        ],
        out_specs=[pl.BlockSpec((c, nh * RET_DV), lambda b, h, s: (row(b, h, s), h))],
        out_shape=[jax.ShapeDtypeStruct((t, hh * RET_DV), BF16)],
        scratch_shapes=[pltpu.VMEM((nh, kq, RET_DV), F32), pltpu.VMEM((nh, c, c), F32),
                        pltpu.VMEM((nh, c, LANES), F32)],
        args=(lg, main, main, main, main, norm_w.reshape(1, -1)), name="retention", casts=casts)


def _rotate_half_cols(w, half):
    return jnp.concatenate([-w[..., half:], w[..., :half]], axis=-1)


def _rope_tab(positions, dim):
    inv_freq = ROPE_THETA ** (-jnp.arange(0, dim, 2, dtype=F32) / dim)
    ang = positions.astype(F32).reshape(-1, 1) * inv_freq
    return jnp.cos(ang), jnp.sin(ang)


def kernel(x, p, positions, l0_attn_norm, l0_w_in, l0_gdn_conv, l0_gdn_A_log, l0_gdn_dt_bias, l0_gdn_norm, l0_mla_q_norm, l0_mla_w_uq, l0_mla_kv_norm, l0_mla_w_ukv, l0_w_out, l0_ffn_norm, l0_ffn_w_up, l0_ffn_conv_w, l0_ffn_conv_b, l0_ffn_w_down, l0_ple_proj, l0_ple_gate_norm, l0_ple_gate, l1_attn_norm, l1_w_in, l1_ret_norm, l1_w_out, l1_ffn_norm, l1_ffn_w_up, l1_ffn_conv_w, l1_ffn_conv_b, l1_ffn_w_down, l1_ple_proj, l1_ple_gate_norm, l1_ple_gate, final_norm):
    batch, seq, d = x.shape
    t = batch * seq
    x2 = x.reshape(t, d)
    p2 = p.reshape(p.shape[0], t, p.shape[-1])

    gq = GDN_HEADS * GDN_D * 4
    n_ab = 2 * GDN_HEADS
    c0 = gq + n_ab
    w_in0 = l0_w_in.astype(BF16)
    w_tail = w_in0[:, c0:c0 + 2 * MLA_RANK]
    w_kr = l0_w_in[:, c0 + 2 * MLA_RANK:]
    w_side = jnp.concatenate(
        [w_kr, _rotate_half_cols(w_kr, MLA_ROPE // 2), l0_w_in[:, gq:c0],
         jnp.zeros((d, LANES - n_ab), F32)], axis=1).astype(BF16)

    wq = l0_mla_w_uq.reshape(MLA_RANK, MLA_HEADS, MLA_NOPE + MLA_ROPE)
    wq_pe = wq[:, :, MLA_NOPE:]
    wq = jnp.concatenate([wq[:, :, :MLA_NOPE], wq_pe, _rotate_half_cols(wq_pe, MLA_ROPE // 2)], axis=-1)
    wq = wq.reshape(MLA_RANK, MLA_HEADS * 256).astype(BF16)
    wkv = l0_mla_w_ukv.reshape(MLA_RANK, MLA_HEADS, MLA_NOPE + MLA_V)
    wk = wkv[:, :, :MLA_NOPE].reshape(MLA_RANK, -1).astype(BF16)
    wvt = wkv[:, :, MLA_NOPE:].reshape(MLA_RANK, -1).T.astype(BF16)

    cos_m, sin_m = _rope_tab(positions, MLA_ROPE)
    tab_m = jnp.concatenate([cos_m, cos_m, sin_m, sin_m], axis=1)
    tab_r = jnp.concatenate(_rope_tab(positions, RET_DK), axis=1)

    main0, side0 = _in_proj0(x2, l0_attn_norm, w_in0, w_tail, w_side, l0_gdn_conv, seq=seq, bm=1024, bn=1024)
    y_a = _gdn(main0, side0, l0_gdn_A_log, l0_gdn_dt_bias, l0_gdn_norm,
               batch=batch, seq=seq, lb=256, nh=8)
    qf, kf, vt = _mla_up(main0, side0, tab_m, l0_mla_q_norm, l0_mla_kv_norm, wq, wk, wvt, bm=512)
    y_b, w_up = _flash(qf, kf, vt, batch=batch, seq=seq, blk=512, nh=4, casts=[l0_ffn_w_up])
    w_out0 = l0_w_out.astype(BF16)
    ha = GDN_HEADS * GDN_D
    (h,) = _proj_res([(y_a, w_out0[:ha]), (y_b, w_out0[ha:])], x2, bm=1024, bn=1024)
    act, w_down = _ffn_up(h, l0_ffn_norm, w_up, l0_ffn_conv_w, l0_ffn_conv_b, seq=seq, bm=1024, bn=512,
                          casts=[l0_ffn_w_down])
    h, w_gate = _proj_res([(act, w_down)], h, bm=1024, bn=512, casts=[l0_ple_gate])
    h, w_in1, w_out1 = _ple(h, p2, 0, l0_ple_gate_norm, l0_ple_proj.astype(BF16), w_gate, bm=512, bn=512,
                            casts=[l1_w_in, l1_w_out])

    main1 = _in_proj1(h, l1_attn_norm, w_in1, tab_r, bm=1024, bn=2048)
    y, w_up = _retention(main1, l1_ret_norm, batch=batch, seq=seq, c=256, nh=8, casts=[l1_ffn_w_up])
    (h,) = _proj_res([(y, w_out1)], h, bm=1024, bn=1024)
    act, w_down = _ffn_up(h, l1_ffn_norm, w_up, l1_ffn_conv_w, l1_ffn_conv_b, seq=seq, bm=1024, bn=512,
                          casts=[l1_ffn_w_down])
    h, w_gate = _proj_res([(act, w_down)], h, bm=1024, bn=512, casts=[l1_ple_gate])
    (h,) = _ple(h, p2, 1, l1_ple_gate_norm, l1_ple_proj.astype(BF16), w_gate, final_norm, bm=512, bn=512)
    return h.reshape(batch, seq, d)
```

```python
import functools

import jax
import jax.numpy as jnp
from jax import lax
from jax.experimental import pallas as pl
from jax.experimental.pallas import tpu as pltpu

F32 = jnp.float32
BF16 = jnp.bfloat16

NORM_EPS = 1e-6
ROPE_THETA = 10000.0

GDN_HEADS = 8
GDN_D = 128
GDN_CONV = 4
GDN_CHUNK = 64
MLA_HEADS = 8
MLA_RANK = 512
MLA_NOPE = 128
MLA_ROPE = 64
MLA_V = 128
RET_HEADS = 8
RET_DK = 256
RET_DV = 512
FFN_CONV = 3

LANES = 128
VMEM_LIMIT = 56 * 1024 * 1024
ROW_BLOCK = 1024
COL_BLOCK = 1024
COL_BLOCK_IN1 = 2048
COL_BLOCK_FFN = 512
PLE_ROWS, PLE_COLS = 512, 512
MLA_UP_ROWS = 512
GDN_TOKENS, GDN_HEADS_PER_STEP = 256, 8
FLASH_BLOCK, FLASH_HEADS_PER_STEP = 512, 4
RET_CHUNK, RET_HEADS_PER_STEP = 256, 8

LOG2_E = 1.4426950408889634
CONV_HALO = 16


def _params(n_grid):
    return pltpu.CompilerParams(dimension_semantics=("arbitrary",) * n_grid,
                                vmem_limit_bytes=VMEM_LIMIT)


def _rms(x, g):
    return x * lax.rsqrt(jnp.mean(x * x, axis=-1, keepdims=True) + NORM_EPS) * g


def _dot(a, b):
    return jnp.dot(a, b, preferred_element_type=F32)


def _dot_nt(a, b):
    return lax.dot_general(a, b, (((1,), (1,)), ((), ())), preferred_element_type=F32)


def _dot_tn(a, b):
    return lax.dot_general(a, b, (((0,), (0,)), ((), ())), preferred_element_type=F32)


def _silu(x):
    return x * jax.nn.sigmoid(x)


def _call(kernel_fn, *, grid, in_specs, out_specs, out_shape, args, name, scratch_shapes=(), casts=()):
    n_in, n_out = len(in_specs), len(out_specs)
    n_steps = 1
    for g in grid:
        n_steps *= g

    def flat_step(*idx):
        step = idx[0]
        for g, i in zip(grid[1:], idx[1:]):
            step = step * g + i
        return step

    in_specs, out_specs, out_shape, args = list(in_specs), list(out_specs), list(out_shape), list(args)
    for w in casts:
        rows, cols = w.shape[0] // n_steps, w.shape[1]
        assert rows * n_steps == w.shape[0] and rows % 16 == 0, (w.shape, n_steps)
        spec = pl.BlockSpec((None, rows, cols), lambda *idx: (flat_step(*idx), 0, 0))
        in_specs.append(spec)
        out_specs.append(spec)
        out_shape.append(jax.ShapeDtypeStruct((n_steps, rows, cols), BF16))
        args.append(w.reshape(n_steps, rows, cols))

    def body(*refs):
        ins, extra_in = refs[:n_in], refs[n_in:n_in + len(casts)]
        outs = refs[n_in + len(casts):n_in + len(casts) + n_out]
        extra_out = refs[n_in + len(casts) + n_out:n_in + 2 * len(casts) + n_out]
        scratch = refs[n_in + 2 * len(casts) + n_out:]
        for src, dst in zip(extra_in, extra_out):
            dst[...] = src[...].astype(BF16)
        kernel_fn(*ins, *outs, *scratch)

    out = pl.pallas_call(
        body, grid=grid, in_specs=in_specs, out_specs=out_specs, out_shape=out_shape,
        scratch_shapes=list(scratch_shapes), compiler_params=_params(len(grid)), name=name,
    )(*args)
    return list(out[:n_out]) + [o.reshape(w.shape) for o, w in zip(out[n_out:], casts)]


def _in_proj0_kernel(x_ref, xh_ref, g_ref, w_ref, wt_ref, ws_ref, cw_ref, o_ref, os_ref, xn_ref, u_ref,
                     *, bm, blocks_per_seq, n_conv_blocks, n_main_blocks):
    i = pl.program_id(0)
    j = pl.program_id(1)

    @pl.when(j == 0)
    def _():
        g = g_ref[...]
        halo = _rms(xh_ref[...], g)
        halo = jnp.where(i % blocks_per_seq == 0, 0.0, halo)
        xn_ref[0:CONV_HALO, :] = halo.astype(BF16)
        xn = _rms(x_ref[...], g).astype(BF16)
        xn_ref[CONV_HALO:, :] = xn
        os_ref[...] = _dot(xn, ws_ref[...])

    def conv_silu():
        u_ref[...] = _dot(xn_ref[...], w_ref[...])
        cw = cw_ref[...]
        y = cw[GDN_CONV - 1:GDN_CONV] * u_ref[pl.ds(CONV_HALO, bm), :]
        for s in range(1, GDN_CONV):
            y = y + cw[GDN_CONV - 1 - s:GDN_CONV - s] * u_ref[pl.ds(CONV_HALO - s, bm), :]
        return _silu(y)

    @pl.when(j < n_conv_blocks - 1)
    def _():
        y = conv_silu()
        scale = jnp.where(j == 0, GDN_D ** -0.5, 1.0)
        for h in range(y.shape[1] // GDN_D):
            cols = slice(h * GDN_D, (h + 1) * GDN_D)
            yh = y[:, cols]
            inv = lax.rsqrt(jnp.sum(yh * yh, axis=-1, keepdims=True) + NORM_EPS) * scale
            o_ref[:, cols] = (yh * inv).astype(o_ref.dtype)

    @pl.when(j == n_conv_blocks - 1)
    def _():
        o_ref[...] = conv_silu().astype(o_ref.dtype)

    @pl.when((j >= n_conv_blocks) & (j < n_main_blocks))
    def _():
        o_ref[...] = _dot(xn_ref[CONV_HALO:, :], w_ref[...]).astype(o_ref.dtype)

    @pl.when(j >= n_main_blocks)
    def _():
        o_ref[...] = _dot(xn_ref[CONV_HALO:, :], wt_ref[...]).astype(o_ref.dtype)


def _in_proj0(x, g, w, w_tail, w_side, conv_w, *, seq, bm, bn):
    t, d = x.shape
    ns = w_side.shape[1]
    n_conv_blocks = conv_w.shape[1] // bn
    n_main_blocks = (GDN_HEADS * GDN_D * 4) // bn
    n_tail_blocks = w_tail.shape[1] // bn
    n = (n_main_blocks + n_tail_blocks) * bn
    hb = bm // CONV_HALO
    kern = functools.partial(_in_proj0_kernel, bm=bm, blocks_per_seq=seq // bm, n_conv_blocks=n_conv_blocks,
                             n_main_blocks=n_main_blocks)
    return pl.pallas_call(
        kern,
        grid=(t // bm, n_main_blocks + n_tail_blocks),
        in_specs=[
            pl.BlockSpec((bm, d), lambda i, j: (i, 0)),
            pl.BlockSpec((CONV_HALO, d), lambda i, j: (jnp.maximum(i * hb - 1, 0), 0)),
            pl.BlockSpec((1, d), lambda i, j: (0, 0)),
            pl.BlockSpec((d, bn), lambda i, j: (0, jnp.minimum(j, n_main_blocks - 1))),
            pl.BlockSpec((d, bn), lambda i, j: (0, jnp.maximum(j - n_main_blocks, 0))),
            pl.BlockSpec((d, ns), lambda i, j: (0, 0)),
            pl.BlockSpec((GDN_CONV, bn), lambda i, j: (0, jnp.minimum(j, n_conv_blocks - 1))),
        ],
        out_specs=[pl.BlockSpec((bm, bn), lambda i, j: (i, j)),
                   pl.BlockSpec((bm, ns), lambda i, j: (i, 0))],
        out_shape=[jax.ShapeDtypeStruct((t, n), BF16), jax.ShapeDtypeStruct((t, ns), F32)],
        scratch_shapes=[pltpu.VMEM((bm + CONV_HALO, d), BF16), pltpu.VMEM((bm + CONV_HALO, bn), F32)],
        compiler_params=_params(2), name="in_proj0",
    )(x, x, g.reshape(1, d), w, w_tail, w_side, conv_w)


def _in_proj1_kernel(x_ref, g_ref, w_ref, cs_ref, o_ref, xn_ref, *, n_q, n_k, n_v):
    j = pl.program_id(1)
    half = RET_DK // 2

    @pl.when(j == 0)
    def _():
        xn_ref[...] = _rms(x_ref[...], g_ref[...]).astype(BF16)

    def proj():
        return _dot(xn_ref[...], w_ref[...])

    def rope_store(scale):
        y = proj()
        cos = cs_ref[:, :half]
        sin = cs_ref[:, half:]
        for h in range(y.shape[1] // RET_DK):
            x1 = y[:, h * RET_DK:h * RET_DK + half]
            x2 = y[:, h * RET_DK + half:(h + 1) * RET_DK]
            o_ref[:, h * RET_DK:h * RET_DK + half] = ((x1 * cos - x2 * sin) * scale).astype(o_ref.dtype)
            o_ref[:, h * RET_DK + half:(h + 1) * RET_DK] = ((x2 * cos + x1 * sin) * scale).astype(o_ref.dtype)

    @pl.when(j < n_q)
    def _():
        rope_store(1.0)

    @pl.when((j >= n_q) & (j < n_q + n_k))
    def _():
        rope_store(RET_DK ** -0.5)

    @pl.when((j >= n_q + n_k) & (j < n_q + n_k + n_v))
    def _():
        o_ref[...] = proj().astype(o_ref.dtype)

    @pl.when(j >= n_q + n_k + n_v)
    def _():
        o_ref[...] = _silu(proj()).astype(o_ref.dtype)


def _in_proj1(x, g, w, tab, *, bm, bn):
    t, d = x.shape
    n = w.shape[1]
    n_q = RET_HEADS * RET_DK // bn
    n_v = RET_HEADS * RET_DV // bn
    return pl.pallas_call(
        functools.partial(_in_proj1_kernel, n_q=n_q, n_k=n_q, n_v=n_v),
        grid=(t // bm, n // bn),
        in_specs=[pl.BlockSpec((bm, d), lambda i, j: (i, 0)),
                  pl.BlockSpec((1, d), lambda i, j: (0, 0)),
                  pl.BlockSpec((d, bn), lambda i, j: (0, j)),
                  pl.BlockSpec((bm, RET_DK), lambda i, j: (i, 0))],
        out_specs=pl.BlockSpec((bm, bn), lambda i, j: (i, j)),
        out_shape=jax.ShapeDtypeStruct((t, n), BF16),
        scratch_shapes=[pltpu.VMEM((bm, d), BF16)],
        compiler_params=_params(2), name="in_proj1",
    )(x, g.reshape(1, d), w, tab)


def _proj_res_kernel(*refs, n_pairs):
    res_ref, o_ref = refs[2 * n_pairs], refs[2 * n_pairs + 1]
    acc = res_ref[...]
    for p in range(n_pairs):
        acc = acc + _dot(refs[2 * p][...], refs[2 * p + 1][...])
    o_ref[...] = acc


def _proj_res(pairs, res, *, bm, bn, casts=()):
    t, n = res.shape
    in_specs, args = [], []
    for y, w in pairs:
        kk = y.shape[1]
        in_specs += [pl.BlockSpec((bm, kk), lambda i, j: (i, 0)),
                     pl.BlockSpec((kk, bn), lambda i, j: (0, j))]
        args += [y, w]
    in_specs.append(pl.BlockSpec((bm, bn), lambda i, j: (i, j)))
    args.append(res)
    return _call(
        functools.partial(_proj_res_kernel, n_pairs=len(pairs)),
        grid=(t // bm, n // bn),
        in_specs=in_specs,
        out_specs=[pl.BlockSpec((bm, bn), lambda i, j: (i, j))],
        out_shape=[jax.ShapeDtypeStruct((t, n), F32)],
        args=args, name="proj_res", casts=casts)


def _ffn_up_kernel(h_ref, hh_ref, g_ref, wg_ref, wu_ref, cg_ref, cu_ref, bg_ref, bu_ref,
                   o_ref, xn_ref, ug_ref, uu_ref, *, bm, blocks_per_seq):
    i = pl.program_id(0)

    @pl.when(pl.program_id(1) == 0)
    def _():
        g = g_ref[...]
        halo = _rms(hh_ref[...], g)
        halo = jnp.where(i % blocks_per_seq == 0, 0.0, halo)
        xn_ref[0:CONV_HALO, :] = halo.astype(BF16)
        xn_ref[CONV_HALO:, :] = _rms(h_ref[...], g).astype(BF16)

    xn = xn_ref[...]
    ug_ref[...] = _dot(xn, wg_ref[...])
    uu_ref[...] = _dot(xn, wu_ref[...])

    def conv(u_ref, c_ref, b_ref):
        cw = c_ref[...]
        y = b_ref[...] + cw[FFN_CONV - 1:FFN_CONV] * u_ref[pl.ds(CONV_HALO, bm), :]
        for s in range(1, FFN_CONV):
            y = y + cw[FFN_CONV - 1 - s:FFN_CONV - s] * u_ref[pl.ds(CONV_HALO - s, bm), :]
        return y

    gate = conv(ug_ref, cg_ref, bg_ref)
    up = conv(uu_ref, cu_ref, bu_ref)
    o_ref[...] = (_silu(gate) * up).astype(o_ref.dtype)


def _ffn_up(h, g, w_up, conv_w, conv_b, *, seq, bm, bn, casts=()):
    t, d = h.shape
    dff = w_up.shape[1] // 2
    nj = dff // bn
    hb = bm // CONV_HALO
    kern = functools.partial(_ffn_up_kernel, bm=bm, blocks_per_seq=seq // bm)
    return _call(
        kern,
        grid=(t // bm, nj),
        in_specs=[
            pl.BlockSpec((bm, d), lambda i, j: (i, 0)),
            pl.BlockSpec((CONV_HALO, d), lambda i, j: (jnp.maximum(i * hb - 1, 0), 0)),
            pl.BlockSpec((1, d), lambda i, j: (0, 0)),
            pl.BlockSpec((d, bn), lambda i, j: (0, j)),
            pl.BlockSpec((d, bn), lambda i, j: (0, j + nj)),
            pl.BlockSpec((FFN_CONV, bn), lambda i, j: (0, j)),
            pl.BlockSpec((FFN_CONV, bn), lambda i, j: (0, j + nj)),
            pl.BlockSpec((1, bn), lambda i, j: (0, j)),
            pl.BlockSpec((1, bn), lambda i, j: (0, j + nj)),
        ],
        out_specs=[pl.BlockSpec((bm, bn), lambda i, j: (i, j))],
        out_shape=[jax.ShapeDtypeStruct((t, dff), BF16)],
        scratch_shapes=[pltpu.VMEM((bm + CONV_HALO, d), BF16),
                        pltpu.VMEM((bm + CONV_HALO, bn), F32),
                        pltpu.VMEM((bm + CONV_HALO, bn), F32)],
        args=(h, h, g.reshape(1, d), w_up, w_up, conv_w, conv_w,
              conv_b.reshape(1, -1), conv_b.reshape(1, -1)),
        name="ffn_up", casts=casts)


def _ple_kernel(h_ref, p_ref, gn_ref, wp_ref, wg_ref, *rest, bn, final):
    if final:
        fn_ref, o_ref = rest
    else:
        (o_ref,) = rest
    d = h_ref.shape[1]
    hn = _rms(h_ref[...], gn_ref[...]).astype(BF16)
    pb = p_ref[...].astype(BF16)
    ssq = None
    for c in range(d // bn):
        cols = slice(c * bn, (c + 1) * bn)
        gate = _dot(hn, wg_ref[:, cols])
        proj = _dot(pb, wp_ref[:, cols])
        out = h_ref[:, cols] + proj * jax.nn.sigmoid(gate)
        o_ref[:, cols] = out
        if final:
            part = jnp.sum(out * out, axis=-1, keepdims=True)
            ssq = part if ssq is None else ssq + part
    if final:
        scale = lax.rsqrt(ssq / d + NORM_EPS)
        o_ref[...] = o_ref[...] * scale * fn_ref[...]


def _ple(h, p, layer, gate_norm, w_proj, w_gate, final_norm=None, *, bm, bn, casts=()):
    t, d = h.shape
    dp = p.shape[-1]
    final = final_norm is not None
    in_specs = [pl.BlockSpec((bm, d), lambda i: (i, 0)),
                pl.BlockSpec((None, bm, dp), lambda i: (layer, i, 0)),
                pl.BlockSpec((1, d), lambda i: (0, 0)),
                pl.BlockSpec((dp, d), lambda i: (0, 0)),
                pl.BlockSpec((d, d), lambda i: (0, 0))]
    args = [h, p, gate_norm.reshape(1, d), w_proj, w_gate]
    if final:
        in_specs.append(pl.BlockSpec((1, d), lambda i: (0, 0)))
        args.append(final_norm.reshape(1, d))
    return _call(
        functools.partial(_ple_kernel, bn=bn, final=final),
        grid=(t // bm,),
        in_specs=in_specs,
        out_specs=[pl.BlockSpec((bm, d), lambda i: (i, 0))],
        out_shape=[jax.ShapeDtypeStruct((t, d), F32)],
        args=args, name="ple", casts=casts)


def _split3(x):
    hi = x.astype(BF16).astype(F32)
    r1 = x - hi
    mid = r1.astype(BF16).astype(F32)
    lo = (r1 - mid).astype(BF16).astype(F32)
    return hi, mid, lo


def _lane_pick(x, lane, idx):
    return jnp.sum(jnp.where(lane == idx, x, 0.0), axis=-1, keepdims=True)


def _gdn_kernel(q_ref, k_ref, v_ref, z_ref, ab_ref, hp_ref, nw_ref, tril_ref, o_ref, s_ref, *, lb, nh):
    c = GDN_CHUNK
    n_chunks = lb // c
    dk = GDN_D

    @pl.when(pl.program_id(2) == 0)
    def _():
        s_ref[...] = jnp.zeros_like(s_ref)

    ab = ab_ref[...]
    lane = lax.broadcasted_iota(jnp.int32, ab.shape, 1)
    hp = hp_ref[...]
    xa = ab + hp[1:2]
    softplus = jnp.maximum(xa, 0.0) + jnp.log1p(jnp.exp(-jnp.abs(xa)))
    g_lanes = -jnp.exp(hp[0:1]) * softplus
    beta_lanes = jax.nn.sigmoid(ab)

    def pieces_to_lanes(x, first, sign):
        hi, mid, lo = _split3(x)
        out = jnp.where(lane == first, sign * hi, 0.0)
        out = jnp.where(lane == first + 1, sign * mid, out)
        return jnp.where(lane == first + 2, sign * lo, out)

    row2 = lax.broadcasted_iota(jnp.int32, (c, 2 * c), 0)
    lane2 = lax.broadcasted_iota(jnp.int32, (c, 2 * c), 1)
    col2 = lane2 & (c - 1)
    incl2 = row2 >= col2
    strict2 = row2 > col2
    left = lane2 < c
    eye_left = jnp.where(left & (row2 == lane2), 1.0, 0.0)
    zeros_k = jnp.zeros((c, 2 * c), BF16)
    zeros_x = jnp.zeros((c, 2 * dk), BF16)

    items = []
    for hi in range(nh):
        head = pl.program_id(1) * nh + hi
        cols = slice(hi * dk, (hi + 1) * dk)
        qb = q_ref[:, cols]
        kb16 = k_ref[:, cols]
        q = qb.astype(F32)
        k = kb16.astype(F32)
        v = v_ref[:, cols].astype(F32)
        g_col = _lane_pick(g_lanes, lane, head)
        beta = _lane_pick(beta_lanes, lane, head + GDN_HEADS)
        gm = _dot(tril_ref[...], pieces_to_lanes(g_col, 0, 1.0).astype(BF16))
        big_g = jnp.sum(gm, axis=-1, keepdims=True)
        e_g = jnp.exp(big_g)
        am = jnp.where(lane < 3, pieces_to_lanes(big_g, 0, 1.0), jnp.where(lane < 6, 1.0, 0.0)).astype(BF16)
        bmat = jnp.where(lane < 3, 1.0, pieces_to_lanes(big_g, 3, -1.0)).astype(BF16)
        kbeta = k * beta
        x0 = jnp.concatenate([v * beta, kbeta * e_g], axis=-1).astype(BF16)
        qd = q * e_g
        kbeta16 = kbeta.astype(BF16)
        for n in range(n_chunks):
            r = slice(n * c, (n + 1) * c)
            g_last = big_g[(n + 1) * c - 1:(n + 1) * c, :]
            items.append(dict(
                am=am[r], bm2=jnp.concatenate([bmat[r], bmat[r]], axis=0),
                kb=kbeta16[r], k=kb16[r], k2=jnp.concatenate([kb16[r], kb16[r]], axis=0), q=qb[r],
                x0=jnp.concatenate([x0[r], zeros_x], axis=0), qd=qd[r],
                kd=(k[r] * jnp.exp(g_last - big_g[r])).astype(BF16), gl=jnp.exp(g_last)))

    for it in items:
        it["gamma2"] = jnp.exp(jnp.where(incl2, _dot_nt(it["am"], it["bm2"]), -jnp.inf))
    for it in items:
        it["tp"] = jnp.where(left, eye_left, jnp.where(strict2, -_dot_nt(it["kb"], it["k2"]) * it["gamma2"], 0.0))
    for it in items:
        it["attn"] = (_dot_nt(it["q"], it["k"]) * it["gamma2"][:, :c]).astype(BF16)

    for j in range(6):
        for it in items:
            it["tpb"] = it["tp"].astype(BF16)
        for it in items:
            upd = _dot(it["tpb"], jnp.concatenate([zeros_k, it["tpb"]], axis=0))
            it["tp"] = jnp.where(left, it["tp"] + upd, upd)
    for it in items:
        it["tpb"] = it["tp"].astype(BF16)
    for it in items:
        it["xb"] = _dot(it["tpb"], it["x0"]).astype(BF16)

    for it in items:
        it["au"] = _dot(it["attn"], it["xb"])
    for it in items:
        it["ku"] = _dot_tn(it["kd"], it["xb"])
    for it in items:
        it["lhs"] = jnp.concatenate([it["qd"] - it["au"][:, dk:], it["ku"][:, dk:]], axis=0).astype(BF16)

    states = [s_ref[hi] for hi in range(nh)]
    outs = [[None] * n_chunks for _ in range(nh)]
    for n in range(n_chunks):
        for hi in range(nh):
            it = items[hi * n_chunks + n]
            prod = _dot(it["lhs"], states[hi].astype(BF16))
            outs[hi][n] = prod[:c] + it["au"][:, :dk]
            states[hi] = states[hi] * it["gl"] - prod[c:] + it["ku"][:, :dk]
    for hi in range(nh):
        s_ref[hi] = states[hi]

    nw = nw_ref[...]
    for hi in range(nh):
        cols = slice(hi * dk, (hi + 1) * dk)
        o = jnp.concatenate(outs[hi], axis=0)
        o_ref[:, cols] = (_rms(o, nw) * _silu(z_ref[:, cols].astype(F32))).astype(o_ref.dtype)


def _gdn(main, side, a_log, dt_bias, norm_w, *, batch, seq, lb, nh):
    t = main.shape[0]
    ns = seq // lb
    hh = GDN_HEADS
    ng = hh // nh
    w = nh * GDN_D
    hp = jnp.zeros((8, LANES), F32)
    hp = hp.at[0, :hh].set(a_log).at[1, :hh].set(dt_bias)
    r = jnp.arange(lb)
    tril = ((r[:, None] >= r[None, :]) & (r[:, None] // GDN_CHUNK == r[None, :] // GDN_CHUNK)).astype(BF16)
    row = lambda b, h, s: b * ns + s
    return pl.pallas_call(
        functools.partial(_gdn_kernel, lb=lb, nh=nh),
        grid=(batch, ng, ns),
        in_specs=[
            pl.BlockSpec((lb, w), lambda b, h, s: (row(b, h, s), h)),
            pl.BlockSpec((lb, w), lambda b, h, s: (row(b, h, s), ng + h)),
            pl.BlockSpec((lb, w), lambda b, h, s: (row(b, h, s), 2 * ng + h)),
            pl.BlockSpec((lb, w), lambda b, h, s: (row(b, h, s), 3 * ng + h)),
            pl.BlockSpec((lb, LANES), lambda b, h, s: (row(b, h, s), 1)),
            pl.BlockSpec((8, LANES), lambda b, h, s: (0, 0)),
            pl.BlockSpec((1, GDN_D), lambda b, h, s: (0, 0)),
            pl.BlockSpec((lb, lb), lambda b, h, s: (0, 0)),
        ],
        out_specs=pl.BlockSpec((lb, w), lambda b, h, s: (row(b, h, s), h)),
        out_shape=jax.ShapeDtypeStruct((t, hh * GDN_D), BF16),
        scratch_shapes=[pltpu.VMEM((nh, GDN_D, GDN_D), F32)],
        compiler_params=_params(3), name="gdn",
    )(main, main, main, main, side, hp, norm_w.reshape(1, GDN_D), tril)


def _mla_up_kernel(cq_ref, ckv_ref, side_ref, tab_ref, qn_ref, kvn_ref, wq_ref, wk_ref, wvt_ref,
                   q_out, k_out, v_out):
    hh = MLA_HEADS
    tab = tab_ref[...]
    scale = (MLA_NOPE + MLA_ROPE) ** -0.5 * LOG2_E

    cq = _rms(cq_ref[...].astype(F32), qn_ref[...]).astype(BF16)
    ckv = _rms(ckv_ref[...].astype(F32), kvn_ref[...]).astype(BF16)
    q_all = _dot(cq, wq_ref[...])
    k_all = _dot(ckv, wk_ref[...])
    v_out[...] = _dot_nt(wvt_ref[...], ckv).astype(BF16)

    kr = side_ref[...] * tab
    kpe = (kr + pltpu.roll(kr, shift=MLA_ROPE, axis=1)).astype(BF16)
    rot = tab * scale
    for h in range(hh):
        q_out[:, h * 256:h * 256 + LANES] = (q_all[:, h * 256:h * 256 + LANES] * scale).astype(BF16)
        q_out[:, h * 256 + LANES:(h + 1) * 256] = (q_all[:, h * 256 + LANES:(h + 1) * 256] * rot).astype(BF16)
        k_out[:, h * 256:h * 256 + LANES] = k_all[:, h * LANES:(h + 1) * LANES].astype(BF16)
        k_out[:, h * 256 + LANES:(h + 1) * 256] = kpe


def _mla_up(main, side, tab, q_norm, kv_norm, wq, wk, wvt, *, bm):
    t = main.shape[0]
    r = MLA_RANK
    cq_blk = (GDN_HEADS * GDN_D * 4) // r
    return pl.pallas_call(
        _mla_up_kernel,
        grid=(t // bm,),
        in_specs=[
            pl.BlockSpec((bm, r), lambda i: (i, cq_blk)),
            pl.BlockSpec((bm, r), lambda i: (i, cq_blk + 1)),
            pl.BlockSpec((bm, LANES), lambda i: (i, 0)),
            pl.BlockSpec((bm, LANES), lambda i: (i, 0)),
            pl.BlockSpec((1, r), lambda i: (0, 0)),
            pl.BlockSpec((1, r), lambda i: (0, 0)),
            pl.BlockSpec(wq.shape, lambda i: (0, 0)),
            pl.BlockSpec(wk.shape, lambda i: (0, 0)),
            pl.BlockSpec(wvt.shape, lambda i: (0, 0)),
        ],
        out_specs=[pl.BlockSpec((bm, MLA_HEADS * 256), lambda i: (i, 0)),
                   pl.BlockSpec((bm, MLA_HEADS * 256), lambda i: (i, 0)),
                   pl.BlockSpec((MLA_HEADS * MLA_V, bm), lambda i: (0, i))],
        out_shape=[jax.ShapeDtypeStruct((t, MLA_HEADS * 256), BF16),
                   jax.ShapeDtypeStruct((t, MLA_HEADS * 256), BF16),
                   jax.ShapeDtypeStruct((MLA_HEADS * MLA_V, t), BF16)],
        compiler_params=_params(1), name="mla_up",
    )(main, main, side, tab, q_norm.reshape(1, r), kv_norm.reshape(1, r), wq, wk, wvt)


def _flash_kernel(q_ref, k_ref, vt_ref, o_ref, *, blk, nh):
    qi = pl.program_id(2)
    dq = 2 * LANES
    qs = [q_ref[:, h * dq:(h + 1) * dq] for h in range(nh)]

    def step(j, carry, masked):
        start = pl.multiple_of(j * blk, blk)
        ss = [_dot_nt(k_ref[pl.ds(start, blk), h * dq:(h + 1) * dq], qs[h]) for h in range(nh)]
        if masked:
            key = lax.broadcasted_iota(jnp.int32, ss[0].shape, 0)
            qry = lax.broadcasted_iota(jnp.int32, ss[0].shape, 1)
            ss = [jnp.where(key <= qry, s, -jnp.inf) for s in ss]
        m_new = [jnp.maximum(carry[h][0], jnp.max(ss[h], axis=0, keepdims=True)) for h in range(nh)]
        ps = [jnp.exp2(ss[h] - m_new[h]) for h in range(nh)]
        pv = [_dot(vt_ref[h * MLA_V:(h + 1) * MLA_V, pl.ds(start, blk)], ps[h].astype(BF16)) for h in range(nh)]
        out = []
        for h in range(nh):
            m, l, acc = carry[h]
            alpha = jnp.exp2(m - m_new[h])
            out.append((m_new[h], l * alpha + jnp.sum(ps[h], axis=0, keepdims=True), acc * alpha + pv[h]))
        return tuple(out)

    init = tuple((jnp.full((1, blk), -jnp.inf, F32), jnp.zeros((1, blk), F32),
                  jnp.zeros((MLA_V, blk), F32)) for _ in range(nh))
    carry = lax.fori_loop(0, qi, lambda j, cr: step(j, cr, False), init)
    carry = step(qi, carry, True)
    for h in range(nh):
        _, l, acc = carry[h]
        o_ref[:, h * MLA_V:(h + 1) * MLA_V] = (acc / l).T.astype(o_ref.dtype)


def _flash(qf, kf, vt, *, batch, seq, blk, nh, casts=()):
    t = qf.shape[0]
    nq = seq // blk
    hh = MLA_HEADS
    return _call(
        functools.partial(_flash_kernel, blk=blk, nh=nh),
        grid=(batch, hh // nh, nq),
        in_specs=[
            pl.BlockSpec((blk, nh * 256), lambda b, h, i: (b * nq + i, h)),
            pl.BlockSpec((seq, nh * 256), lambda b, h, i: (b, h)),
            pl.BlockSpec((nh * MLA_V, seq), lambda b, h, i: (h, b)),
        ],
        out_specs=[pl.BlockSpec((blk, nh * MLA_V), lambda b, h, i: (b * nq + i, h))],
        out_shape=[jax.ShapeDtypeStruct((t, hh * MLA_V), BF16)],
        args=(qf, kf, vt), name="mla_flash", casts=casts)


def _ret_kernel(lg_ref, q_ref, k_ref, v_ref, g_ref, nw_ref, o_ref,
                s_ref, dec_ref, xz_ref, *, c, nh):
    @pl.when(pl.program_id(2) == 0)
    def _():
        row = lax.broadcasted_iota(jnp.int32, (c, c), 0)
        col = lax.broadcasted_iota(jnp.int32, (c, c), 1)
        pos = lax.broadcasted_iota(jnp.int32, (c, LANES), 0).astype(F32)
        lane = lax.broadcasted_iota(jnp.int32, (c, LANES), 1)
        for h in range(nh):
            lg = lg_ref[h, :, 0:1]
            dec_ref[h] = jnp.exp(jnp.where(row >= col, (row - col).astype(F32) * lg, -jnp.inf))
            xi = jnp.exp((pos + 1.0) * lg)
            zeta = jnp.exp((c - 1.0 - pos) * lg)
            xz_ref[h] = jnp.where(lane == 0, xi, zeta)
        s_ref[...] = jnp.zeros_like(s_ref)

    hs = range(nh)
    qb = [q_ref[:, h * RET_DK:(h + 1) * RET_DK] for h in hs]
    kb = [k_ref[:, h * RET_DK:(h + 1) * RET_DK] for h in hs]
    v = [v_ref[:, h * RET_DV:(h + 1) * RET_DV] for h in hs]
    qk = [(_dot_nt(qb[h], kb[h]) * dec_ref[h]).astype(BF16) for h in hs]
    states = [s_ref[h] for h in hs]
    cross = [_dot((qb[h].astype(F32) * xz_ref[h, :, 0:1]).astype(BF16), states[h].astype(BF16)) for h in hs]
    inner = [_dot(qk[h], v[h]) for h in hs]
    upd = [_dot_tn((kb[h].astype(F32) * xz_ref[h, :, 1:2]).astype(BF16), v[h]) for h in hs]
    for h in hs:
        s_ref[h] = states[h] * xz_ref[h, c - 1:c, 0:1] + upd[h]
    for h in hs:
        o = inner[h] + cross[h]
        mu = jnp.mean(o, axis=-1, keepdims=True)
        xc = o - mu
        var = jnp.mean(xc * xc, axis=-1, keepdims=True)
        cols = slice(h * RET_DV, (h + 1) * RET_DV)
        y = xc * lax.rsqrt(var + NORM_EPS) * nw_ref[:, cols]
        o_ref[:, cols] = g_ref[:, cols] * y.astype(o_ref.dtype)


def _retention(main, norm_w, *, batch, seq, c, nh, casts=()):
    t = main.shape[0]
    nc = seq // c
    hh = RET_HEADS
    ng = hh // nh
    log_gamma = jnp.log1p(-jnp.power(2.0, -5.0 - jnp.arange(hh, dtype=F32)))
    lg = jnp.broadcast_to(log_gamma[:, None, None], (hh, 1, LANES))
    row = lambda b, h, s: b * nc + s
    kq = RET_DK
    v_blk0 = (2 * hh * kq) // (nh * RET_DV)
    return _call(
        functools.partial(_ret_kernel, c=c, nh=nh),
        grid=(batch, ng, nc),
        in_specs=[
            pl.BlockSpec((nh, 1, LANES), lambda b, h, s: (h, 0, 0)),
            pl.BlockSpec((c, nh * kq), lambda b, h, s: (row(b, h, s), h)),
            pl.BlockSpec((c, nh * kq), lambda b, h, s: (row(b, h, s), ng + h)),
            pl.BlockSpec((c, nh * RET_DV), lambda b, h, s: (row(b, h, s), v_blk0 + h)),
            pl.BlockSpec((c, nh * RET_DV), lambda b, h, s: (row(b, h, s), v_blk0 + ng + h)),
            pl.BlockSpec((1, nh * RET_DV), lambda b, h, s: (0, h)),
        ],
        out_specs=[pl.BlockSpec((c, nh * RET_DV), lambda b, h, s: (row(b, h, s), h))],
        out_shape=[jax.ShapeDtypeStruct((t, hh * RET_DV), BF16)],
        scratch_shapes=[pltpu.VMEM((nh, kq, RET_DV), F32), pltpu.VMEM((nh, c, c), F32),
                        pltpu.VMEM((nh, c, LANES), F32)],
        args=(lg, main, main, main, main, norm_w.reshape(1, -1)), name="retention", casts=casts)


def _rotate_half_cols(w, half):
    return jnp.concatenate([-w[..., half:], w[..., :half]], axis=-1)


def _rope_tab(positions, dim):
    inv_freq = ROPE_THETA ** (-jnp.arange(0, dim, 2, dtype=F32) / dim)
    ang = positions.astype(F32).reshape(-1, 1) * inv_freq
    return jnp.cos(ang), jnp.sin(ang)


def kernel(x, p, positions, l0_attn_norm, l0_w_in, l0_gdn_conv, l0_gdn_A_log, l0_gdn_dt_bias, l0_gdn_norm, l0_mla_q_norm, l0_mla_w_uq, l0_mla_kv_norm, l0_mla_w_ukv, l0_w_out, l0_ffn_norm, l0_ffn_w_up, l0_ffn_conv_w, l0_ffn_conv_b, l0_ffn_w_down, l0_ple_proj, l0_ple_gate_norm, l0_ple_gate, l1_attn_norm, l1_w_in, l1_ret_norm, l1_w_out, l1_ffn_norm, l1_ffn_w_up, l1_ffn_conv_w, l1_ffn_conv_b, l1_ffn_w_down, l1_ple_proj, l1_ple_gate_norm, l1_ple_gate, final_norm):
    batch, seq, d = x.shape
    t = batch * seq
    x2 = x.reshape(t, d)
    p2 = p.reshape(p.shape[0], t, p.shape[-1])

    gq = GDN_HEADS * GDN_D * 4
    n_ab = 2 * GDN_HEADS
    c0 = gq + n_ab
    w_in0 = l0_w_in.astype(BF16)
    w_tail = w_in0[:, c0:c0 + 2 * MLA_RANK]
    w_kr = l0_w_in[:, c0 + 2 * MLA_RANK:]
    w_side = jnp.concatenate(
        [w_kr, _rotate_half_cols(w_kr, MLA_ROPE // 2), l0_w_in[:, gq:c0],
         jnp.zeros((d, LANES - n_ab), F32)], axis=1).astype(BF16)

    wq = l0_mla_w_uq.reshape(MLA_RANK, MLA_HEADS, MLA_NOPE + MLA_ROPE)
    wq_pe = wq[:, :, MLA_NOPE:]
    wq = jnp.concatenate([wq[:, :, :MLA_NOPE], wq_pe, _rotate_half_cols(wq_pe, MLA_ROPE // 2)], axis=-1)
    wq = wq.reshape(MLA_RANK, MLA_HEADS * 256).astype(BF16)
    wkv = l0_mla_w_ukv.reshape(MLA_RANK, MLA_HEADS, MLA_NOPE + MLA_V)
    wk = wkv[:, :, :MLA_NOPE].reshape(MLA_RANK, -1).astype(BF16)
    wvt = wkv[:, :, MLA_NOPE:].reshape(MLA_RANK, -1).T.astype(BF16)

    cos_m, sin_m = _rope_tab(positions, MLA_ROPE)
    tab_m = jnp.concatenate([cos_m, cos_m, sin_m, sin_m], axis=1)
    tab_r = jnp.concatenate(_rope_tab(positions, RET_DK), axis=1)

    main0, side0 = _in_proj0(x2, l0_attn_norm, w_in0, w_tail, w_side, l0_gdn_conv, seq=seq,
                             bm=ROW_BLOCK, bn=COL_BLOCK)
    y_a = _gdn(main0, side0, l0_gdn_A_log, l0_gdn_dt_bias, l0_gdn_norm,
               batch=batch, seq=seq, lb=GDN_TOKENS, nh=GDN_HEADS_PER_STEP)
    qf, kf, vt = _mla_up(main0, side0, tab_m, l0_mla_q_norm, l0_mla_kv_norm, wq, wk, wvt, bm=MLA_UP_ROWS)
    y_b, w_up = _flash(qf, kf, vt, batch=batch, seq=seq, blk=FLASH_BLOCK, nh=FLASH_HEADS_PER_STEP,
                       casts=[l0_ffn_w_up])
    w_out0 = l0_w_out.astype(BF16)
    ha = GDN_HEADS * GDN_D
    (h,) = _proj_res([(y_a, w_out0[:ha]), (y_b, w_out0[ha:])], x2, bm=ROW_BLOCK, bn=COL_BLOCK)
    act, w_down = _ffn_up(h, l0_ffn_norm, w_up, l0_ffn_conv_w, l0_ffn_conv_b, seq=seq,
                          bm=ROW_BLOCK, bn=COL_BLOCK_FFN, casts=[l0_ffn_w_down])
    h, w_gate = _proj_res([(act, w_down)], h, bm=ROW_BLOCK, bn=COL_BLOCK_FFN, casts=[l0_ple_gate])
    h, w_in1, w_out1 = _ple(h, p2, 0, l0_ple_gate_norm, l0_ple_proj.astype(BF16), w_gate,
                            bm=PLE_ROWS, bn=PLE_COLS, casts=[l1_w_in, l1_w_out])

    main1 = _in_proj1(h, l1_attn_norm, w_in1, tab_r, bm=ROW_BLOCK, bn=COL_BLOCK_IN1)
    y, w_up = _retention(main1, l1_ret_norm, batch=batch, seq=seq, c=RET_CHUNK, nh=RET_HEADS_PER_STEP,
                         casts=[l1_ffn_w_up])
    (h,) = _proj_res([(y, w_out1)], h, bm=ROW_BLOCK, bn=COL_BLOCK)
    act, w_down = _ffn_up(h, l1_ffn_norm, w_up, l1_ffn_conv_w, l1_ffn_conv_b, seq=seq,
                          bm=ROW_BLOCK, bn=COL_BLOCK_FFN, casts=[l1_ffn_w_down])
    h, w_gate = _proj_res([(act, w_down)], h, bm=ROW_BLOCK, bn=COL_BLOCK_FFN, casts=[l1_ple_gate])
    (h,) = _ple(h, p2, 1, l1_ple_gate_norm, l1_ple_proj.astype(BF16), w_gate, final_norm,
                bm=PLE_ROWS, bn=PLE_COLS)
    return h.reshape(batch, seq, d)
```

```python
import functools

import jax
import jax.numpy as jnp
from jax import lax
from jax.experimental import pallas as pl
from jax.experimental.pallas import tpu as pltpu

F32 = jnp.float32
BF16 = jnp.bfloat16

NORM_EPS = 1e-6
ROPE_THETA = 10000.0

GDN_HEADS = 8
GDN_D = 128
GDN_CONV = 4
GDN_CHUNK = 64
MLA_HEADS = 8
MLA_RANK = 512
MLA_NOPE = 128
MLA_ROPE = 64
MLA_V = 128
RET_HEADS = 8
RET_DK = 256
RET_DV = 512
FFN_CONV = 3

LANES = 128
VMEM_LIMIT = 56 * 1024 * 1024
ROW_BLOCK = 1024
COL_BLOCK = 1024
COL_BLOCK_IN1 = 2048
COL_BLOCK_FFN = 512
PLE_ROWS, PLE_COLS = 512, 512
MLA_UP_ROWS = 512
GDN_TOKENS, GDN_HEADS_PER_STEP = 256, 8
FLASH_BLOCK, FLASH_HEADS_PER_STEP = 512, 4
RET_CHUNK, RET_HEADS_PER_STEP = 256, 8

LOG2_E = 1.4426950408889634
CONV_HALO = 16


def _params(n_grid):
    return pltpu.CompilerParams(dimension_semantics=("arbitrary",) * n_grid,
                                vmem_limit_bytes=VMEM_LIMIT)


def _rms(x, g):
    return x * lax.rsqrt(jnp.mean(x * x, axis=-1, keepdims=True) + NORM_EPS) * g


def _dot(a, b):
    return jnp.dot(a, b, preferred_element_type=F32)


def _dot_nt(a, b):
    return lax.dot_general(a, b, (((1,), (1,)), ((), ())), preferred_element_type=F32)


def _dot_tn(a, b):
    return lax.dot_general(a, b, (((0,), (0,)), ((), ())), preferred_element_type=F32)


def _silu(x):
    return x * jax.nn.sigmoid(x)


def _call(kernel_fn, *, grid, in_specs, out_specs, out_shape, args, name, scratch_shapes=(), casts=()):
    n_in, n_out = len(in_specs), len(out_specs)
    n_steps = 1
    for g in grid:
        n_steps *= g

    def flat_step(*idx):
        step = idx[0]
        for g, i in zip(grid[1:], idx[1:]):
            step = step * g + i
        return step

    in_specs, out_specs, out_shape, args = list(in_specs), list(out_specs), list(out_shape), list(args)
    for w in casts:
        rows, cols = w.shape[0] // n_steps, w.shape[1]
        assert rows * n_steps == w.shape[0] and rows % 16 == 0, (w.shape, n_steps)
        spec = pl.BlockSpec((None, rows, cols), lambda *idx: (flat_step(*idx), 0, 0))
        in_specs.append(spec)
        out_specs.append(spec)
        out_shape.append(jax.ShapeDtypeStruct((n_steps, rows, cols), BF16))
        args.append(w.reshape(n_steps, rows, cols))

    def body(*refs):
        ins, extra_in = refs[:n_in], refs[n_in:n_in + len(casts)]
        outs = refs[n_in + len(casts):n_in + len(casts) + n_out]
        extra_out = refs[n_in + len(casts) + n_out:n_in + 2 * len(casts) + n_out]
        scratch = refs[n_in + 2 * len(casts) + n_out:]
        for src, dst in zip(extra_in, extra_out):
            dst[...] = src[...].astype(BF16)
        kernel_fn(*ins, *outs, *scratch)

    out = pl.pallas_call(
        body, grid=grid, in_specs=in_specs, out_specs=out_specs, out_shape=out_shape,
        scratch_shapes=list(scratch_shapes), compiler_params=_params(len(grid)), name=name,
    )(*args)
    return list(out[:n_out]) + [o.reshape(w.shape) for o, w in zip(out[n_out:], casts)]


def _in_proj0_kernel(x_ref, xh_ref, g_ref, w_ref, wt_ref, ws_ref, cw_ref, o_ref, os_ref, xn_ref, u_ref,
                     *, bm, blocks_per_seq, n_conv_blocks, n_main_blocks):
    i = pl.program_id(0)
    j = pl.program_id(1)

    @pl.when(j == 0)
    def _():
        g = g_ref[...]
        halo = _rms(xh_ref[...], g)
        halo = jnp.where(i % blocks_per_seq == 0, 0.0, halo)
        xn_ref[0:CONV_HALO, :] = halo.astype(BF16)
        xn = _rms(x_ref[...], g).astype(BF16)
        xn_ref[CONV_HALO:, :] = xn
        os_ref[...] = _dot(xn, ws_ref[...])

    def conv_silu():
        u_ref[...] = _dot(xn_ref[...], w_ref[...])
        cw = cw_ref[...]
        y = cw[GDN_CONV - 1:GDN_CONV] * u_ref[pl.ds(CONV_HALO, bm), :]
        for s in range(1, GDN_CONV):
            y = y + cw[GDN_CONV - 1 - s:GDN_CONV - s] * u_ref[pl.ds(CONV_HALO - s, bm), :]
        return _silu(y)

    @pl.when(j < n_conv_blocks - 1)
    def _():
        y = conv_silu()
        scale = jnp.where(j == 0, GDN_D ** -0.5, 1.0)
        for h in range(y.shape[1] // GDN_D):
            cols = slice(h * GDN_D, (h + 1) * GDN_D)
            yh = y[:, cols]
            inv = lax.rsqrt(jnp.sum(yh * yh, axis=-1, keepdims=True) + NORM_EPS) * scale
            o_ref[:, cols] = (yh * inv).astype(o_ref.dtype)

    @pl.when(j == n_conv_blocks - 1)
    def _():
        o_ref[...] = conv_silu().astype(o_ref.dtype)

    @pl.when((j >= n_conv_blocks) & (j < n_main_blocks))
    def _():
        o_ref[...] = _dot(xn_ref[CONV_HALO:, :], w_ref[...]).astype(o_ref.dtype)

    @pl.when(j >= n_main_blocks)
    def _():
        o_ref[...] = _dot(xn_ref[CONV_HALO:, :], wt_ref[...]).astype(o_ref.dtype)


def _in_proj0(x, g, w, w_tail, w_side, conv_w, *, seq, bm, bn):
    t, d = x.shape
    ns = w_side.shape[1]
    n_conv_blocks = conv_w.shape[1] // bn
    n_main_blocks = (GDN_HEADS * GDN_D * 4) // bn
    n_tail_blocks = w_tail.shape[1] // bn
    n = (n_main_blocks + n_tail_blocks) * bn
    hb = bm // CONV_HALO
    kern = functools.partial(_in_proj0_kernel, bm=bm, blocks_per_seq=seq // bm, n_conv_blocks=n_conv_blocks,
                             n_main_blocks=n_main_blocks)
    return pl.pallas_call(
        kern,
        grid=(t // bm, n_main_blocks + n_tail_blocks),
        in_specs=[
            pl.BlockSpec((bm, d), lambda i, j: (i, 0)),
            pl.BlockSpec((CONV_HALO, d), lambda i, j: (jnp.maximum(i * hb - 1, 0), 0)),
            pl.BlockSpec((1, d), lambda i, j: (0, 0)),
            pl.BlockSpec((d, bn), lambda i, j: (0, jnp.minimum(j, n_main_blocks - 1))),
            pl.BlockSpec((d, bn), lambda i, j: (0, jnp.maximum(j - n_main_blocks, 0))),
            pl.BlockSpec((d, ns), lambda i, j: (0, 0)),
            pl.BlockSpec((GDN_CONV, bn), lambda i, j: (0, jnp.minimum(j, n_conv_blocks - 1))),
        ],
        out_specs=[pl.BlockSpec((bm, bn), lambda i, j: (i, j)),
                   pl.BlockSpec((bm, ns), lambda i, j: (i, 0))],
        out_shape=[jax.ShapeDtypeStruct((t, n), BF16), jax.ShapeDtypeStruct((t, ns), F32)],
        scratch_shapes=[pltpu.VMEM((bm + CONV_HALO, d), BF16), pltpu.VMEM((bm + CONV_HALO, bn), F32)],
        compiler_params=_params(2), name="in_proj0",
    )(x, x, g.reshape(1, d), w, w_tail, w_side, conv_w)


def _in_proj1_kernel(x_ref, g_ref, w_ref, cs_ref, o_ref, xn_ref, *, n_q, n_k, n_v):
    j = pl.program_id(1)
    half = RET_DK // 2

    @pl.when(j == 0)
    def _():
        xn_ref[...] = _rms(x_ref[...], g_ref[...]).astype(BF16)

    def proj():
        return _dot(xn_ref[...], w_ref[...])

    def rope_store(scale):
        y = proj()
        cos = cs_ref[:, :half]
        sin = cs_ref[:, half:]
        for h in range(y.shape[1] // RET_DK):
            x1 = y[:, h * RET_DK:h * RET_DK + half]
            x2 = y[:, h * RET_DK + half:(h + 1) * RET_DK]
            o_ref[:, h * RET_DK:h * RET_DK + half] = ((x1 * cos - x2 * sin) * scale).astype(o_ref.dtype)
            o_ref[:, h * RET_DK + half:(h + 1) * RET_DK] = ((x2 * cos + x1 * sin) * scale).astype(o_ref.dtype)

    @pl.when(j < n_q)
    def _():
        rope_store(1.0)

    @pl.when((j >= n_q) & (j < n_q + n_k))
    def _():
        rope_store(RET_DK ** -0.5)

    @pl.when((j >= n_q + n_k) & (j < n_q + n_k + n_v))
    def _():
        o_ref[...] = proj().astype(o_ref.dtype)

    @pl.when(j >= n_q + n_k + n_v)
    def _():
        o_ref[...] = _silu(proj()).astype(o_ref.dtype)


def _in_proj1(x, g, w, tab, *, bm, bn):
    t, d = x.shape
    n = w.shape[1]
    n_q = RET_HEADS * RET_DK // bn
    n_v = RET_HEADS * RET_DV // bn
    return pl.pallas_call(
        functools.partial(_in_proj1_kernel, n_q=n_q, n_k=n_q, n_v=n_v),
        grid=(t // bm, n // bn),
        in_specs=[pl.BlockSpec((bm, d), lambda i, j: (i, 0)),
                  pl.BlockSpec((1, d), lambda i, j: (0, 0)),
                  pl.BlockSpec((d, bn), lambda i, j: (0, j)),
                  pl.BlockSpec((bm, RET_DK), lambda i, j: (i, 0))],
        out_specs=pl.BlockSpec((bm, bn), lambda i, j: (i, j)),
        out_shape=jax.ShapeDtypeStruct((t, n), BF16),
        scratch_shapes=[pltpu.VMEM((bm, d), BF16)],
        compiler_params=_params(2), name="in_proj1",
    )(x, g.reshape(1, d), w, tab)


def _proj_res_kernel(*refs, n_pairs):
    res_ref, o_ref = refs[2 * n_pairs], refs[2 * n_pairs + 1]
    acc = res_ref[...]
    for p in range(n_pairs):
        acc = acc + _dot(refs[2 * p][...], refs[2 * p + 1][...])
    o_ref[...] = acc


def _proj_res(pairs, res, *, bm, bn, casts=()):
    t, n = res.shape
    in_specs, args = [], []
    for y, w in pairs:
        kk = y.shape[1]
        in_specs += [pl.BlockSpec((bm, kk), lambda i, j: (i, 0)),
                     pl.BlockSpec((kk, bn), lambda i, j: (0, j))]
        args += [y, w]
    in_specs.append(pl.BlockSpec((bm, bn), lambda i, j: (i, j)))
    args.append(res)
    return _call(
        functools.partial(_proj_res_kernel, n_pairs=len(pairs)),
        grid=(t // bm, n // bn),
        in_specs=in_specs,
        out_specs=[pl.BlockSpec((bm, bn), lambda i, j: (i, j))],
        out_shape=[jax.ShapeDtypeStruct((t, n), F32)],
        args=args, name="proj_res", casts=casts)


def _ffn_up_kernel(h_ref, hh_ref, g_ref, wg_ref, wu_ref, cg_ref, cu_ref, bg_ref, bu_ref,
                   o_ref, xn_ref, ug_ref, uu_ref, *, bm, blocks_per_seq):
    i = pl.program_id(0)

    @pl.when(pl.program_id(1) == 0)
    def _():
        g = g_ref[...]
        halo = _rms(hh_ref[...], g)
        halo = jnp.where(i % blocks_per_seq == 0, 0.0, halo)
        xn_ref[0:CONV_HALO, :] = halo.astype(BF16)
        xn_ref[CONV_HALO:, :] = _rms(h_ref[...], g).astype(BF16)

    xn = xn_ref[...]
    ug_ref[...] = _dot(xn, wg_ref[...])
    uu_ref[...] = _dot(xn, wu_ref[...])

    def conv(u_ref, c_ref, b_ref):
        cw = c_ref[...]
        y = b_ref[...] + cw[FFN_CONV - 1:FFN_CONV] * u_ref[pl.ds(CONV_HALO, bm), :]
        for s in range(1, FFN_CONV):
            y = y + cw[FFN_CONV - 1 - s:FFN_CONV - s] * u_ref[pl.ds(CONV_HALO - s, bm), :]
        return y

    gate = conv(ug_ref, cg_ref, bg_ref)
    up = conv(uu_ref, cu_ref, bu_ref)
    o_ref[...] = (_silu(gate) * up).astype(o_ref.dtype)


def _ffn_up(h, g, w_up, conv_w, conv_b, *, seq, bm, bn, casts=()):
    t, d = h.shape
    dff = w_up.shape[1] // 2
    nj = dff // bn
    hb = bm // CONV_HALO
    kern = functools.partial(_ffn_up_kernel, bm=bm, blocks_per_seq=seq // bm)
    return _call(
        kern,
        grid=(t // bm, nj),
        in_specs=[
            pl.BlockSpec((bm, d), lambda i, j: (i, 0)),
            pl.BlockSpec((CONV_HALO, d), lambda i, j: (jnp.maximum(i * hb - 1, 0), 0)),
            pl.BlockSpec((1, d), lambda i, j: (0, 0)),
            pl.BlockSpec((d, bn), lambda i, j: (0, j)),
            pl.BlockSpec((d, bn), lambda i, j: (0, j + nj)),
            pl.BlockSpec((FFN_CONV, bn), lambda i, j: (0, j)),
            pl.BlockSpec((FFN_CONV, bn), lambda i, j: (0, j + nj)),
            pl.BlockSpec((1, bn), lambda i, j: (0, j)),
            pl.BlockSpec((1, bn), lambda i, j: (0, j + nj)),
        ],
        out_specs=[pl.BlockSpec((bm, bn), lambda i, j: (i, j))],
        out_shape=[jax.ShapeDtypeStruct((t, dff), BF16)],
        scratch_shapes=[pltpu.VMEM((bm + CONV_HALO, d), BF16),
                        pltpu.VMEM((bm + CONV_HALO, bn), F32),
                        pltpu.VMEM((bm + CONV_HALO, bn), F32)],
        args=(h, h, g.reshape(1, d), w_up, w_up, conv_w, conv_w,
              conv_b.reshape(1, -1), conv_b.reshape(1, -1)),
        name="ffn_up", casts=casts)


def _ple_kernel(h_ref, p_ref, gn_ref, wp_ref, wg_ref, *rest, bn, final):
    if final:
        fn_ref, o_ref = rest
    else:
        (o_ref,) = rest
    d = h_ref.shape[1]
    hn = _rms(h_ref[...], gn_ref[...]).astype(BF16)
    pb = p_ref[...].astype(BF16)
    ssq = None
    for c in range(d // bn):
        cols = slice(c * bn, (c + 1) * bn)
        gate = _dot(hn, wg_ref[:, cols])
        proj = _dot(pb, wp_ref[:, cols])
        out = h_ref[:, cols] + proj * jax.nn.sigmoid(gate)
        o_ref[:, cols] = out
        if final:
            part = jnp.sum(out * out, axis=-1, keepdims=True)
            ssq = part if ssq is None else ssq + part
    if final:
        scale = lax.rsqrt(ssq / d + NORM_EPS)
        o_ref[...] = o_ref[...] * scale * fn_ref[...]


def _ple(h, p, layer, gate_norm, w_proj, w_gate, final_norm=None, *, bm, bn, casts=()):
    t, d = h.shape
    dp = p.shape[-1]
    final = final_norm is not None
    in_specs = [pl.BlockSpec((bm, d), lambda i: (i, 0)),
                pl.BlockSpec((None, bm, dp), lambda i: (layer, i, 0)),
                pl.BlockSpec((1, d), lambda i: (0, 0)),
                pl.BlockSpec((dp, d), lambda i: (0, 0)),
                pl.BlockSpec((d, d), lambda i: (0, 0))]
    args = [h, p, gate_norm.reshape(1, d), w_proj, w_gate]
    if final:
        in_specs.append(pl.BlockSpec((1, d), lambda i: (0, 0)))
        args.append(final_norm.reshape(1, d))
    return _call(
        functools.partial(_ple_kernel, bn=bn, final=final),
        grid=(t // bm,),
        in_specs=in_specs,
        out_specs=[pl.BlockSpec((bm, d), lambda i: (i, 0))],
        out_shape=[jax.ShapeDtypeStruct((t, d), F32)],
        args=args, name="ple", casts=casts)


def _split3(x):
    hi = x.astype(BF16).astype(F32)
    r1 = x - hi
    mid = r1.astype(BF16).astype(F32)
    lo = (r1 - mid).astype(BF16).astype(F32)
    return hi, mid, lo


def _lane_pick(x, lane, idx):
    return jnp.sum(jnp.where(lane == idx, x, 0.0), axis=-1, keepdims=True)


def _gdn_kernel(q_ref, k_ref, v_ref, z_ref, ab_ref, hp_ref, nw_ref, tril_ref, o_ref, s_ref, *, lb, nh):
    c = GDN_CHUNK
    n_chunks = lb // c
    dk = GDN_D

    @pl.when(pl.program_id(2) == 0)
    def _():
        s_ref[...] = jnp.zeros_like(s_ref)

    ab = ab_ref[...]
    lane = lax.broadcasted_iota(jnp.int32, ab.shape, 1)
    hp = hp_ref[...]
    xa = ab + hp[1:2]
    softplus = jnp.maximum(xa, 0.0) + jnp.log1p(jnp.exp(-jnp.abs(xa)))
    g_lanes = -jnp.exp(hp[0:1]) * softplus
    beta_lanes = jax.nn.sigmoid(ab)

    def pieces_to_lanes(x, first, sign):
        hi, mid, lo = _split3(x)
        out = jnp.where(lane == first, sign * hi, 0.0)
        out = jnp.where(lane == first + 1, sign * mid, out)
        return jnp.where(lane == first + 2, sign * lo, out)

    row2 = lax.broadcasted_iota(jnp.int32, (c, 2 * c), 0)
    lane2 = lax.broadcasted_iota(jnp.int32, (c, 2 * c), 1)
    col2 = lane2 & (c - 1)
    incl2 = row2 >= col2
    strict2 = row2 > col2
    left = lane2 < c
    eye_left = jnp.where(left & (row2 == lane2), 1.0, 0.0)
    zeros_k = jnp.zeros((c, 2 * c), BF16)
    zeros_x = jnp.zeros((c, 2 * dk), BF16)

    items = []
    for hi in range(nh):
        head = pl.program_id(1) * nh + hi
        cols = slice(hi * dk, (hi + 1) * dk)
        qb = q_ref[:, cols]
        kb16 = k_ref[:, cols]
        q = qb.astype(F32)
        k = kb16.astype(F32)
        v = v_ref[:, cols].astype(F32)
        g_col = _lane_pick(g_lanes, lane, head)
        beta = _lane_pick(beta_lanes, lane, head + GDN_HEADS)
        gm = _dot(tril_ref[...], pieces_to_lanes(g_col, 0, 1.0).astype(BF16))
        big_g = jnp.sum(gm, axis=-1, keepdims=True)
        e_g = jnp.exp(big_g)
        am = jnp.where(lane < 3, pieces_to_lanes(big_g, 0, 1.0), jnp.where(lane < 6, 1.0, 0.0)).astype(BF16)
        bmat = jnp.where(lane < 3, 1.0, pieces_to_lanes(big_g, 3, -1.0)).astype(BF16)
        kbeta = k * beta
        x0 = jnp.concatenate([v * beta, kbeta * e_g], axis=-1).astype(BF16)
        qd = q * e_g
        kbeta16 = kbeta.astype(BF16)
        for n in range(n_chunks):
            r = slice(n * c, (n + 1) * c)
            g_last = big_g[(n + 1) * c - 1:(n + 1) * c, :]
            items.append(dict(
                am=am[r], bm2=jnp.concatenate([bmat[r], bmat[r]], axis=0),
                kbq=jnp.concatenate([kbeta16[r], qb[r]], axis=0), k2=jnp.concatenate([kb16[r], kb16[r]], axis=0),
                x0=jnp.concatenate([x0[r], zeros_x], axis=0), qd=qd[r],
                kd=(k[r] * jnp.exp(g_last - big_g[r])).astype(BF16), gl=jnp.exp(g_last)))

    for it in items:
        it["gamma2"] = jnp.exp(jnp.where(incl2, _dot_nt(it["am"], it["bm2"]), -jnp.inf))
    for it in items:
        it["kq"] = _dot_nt(it["kbq"], it["k2"])
    for it in items:
        it["tp"] = jnp.where(left, eye_left, jnp.where(strict2, -it["kq"][:c] * it["gamma2"], 0.0))
        it["attn"] = (it["kq"][c:, :c] * it["gamma2"][:, :c]).astype(BF16)

    for j in range(6):
        for it in items:
            it["tpb"] = it["tp"].astype(BF16)
        for it in items:
            upd = _dot(it["tpb"], jnp.concatenate([zeros_k, it["tpb"]], axis=0))
            it["tp"] = jnp.where(left, it["tp"] + upd, upd)
    for it in items:
        it["tpb"] = it["tp"].astype(BF16)
    for it in items:
        it["xb"] = _dot(it["tpb"], it["x0"]).astype(BF16)

    for it in items:
        it["au"] = _dot(it["attn"], it["xb"])
    for it in items:
        it["ku"] = _dot_tn(it["kd"], it["xb"])
    for it in items:
        it["lhs"] = jnp.concatenate([it["qd"] - it["au"][:, dk:], it["ku"][:, dk:]], axis=0).astype(BF16)

    states = [s_ref[hi] for hi in range(nh)]
    outs = [[None] * n_chunks for _ in range(nh)]
    for n in range(n_chunks):
        for hi in range(nh):
            it = items[hi * n_chunks + n]
            prod = _dot(it["lhs"], states[hi].astype(BF16))
            outs[hi][n] = prod[:c] + it["au"][:, :dk]
            states[hi] = states[hi] * it["gl"] - prod[c:] + it["ku"][:, :dk]
    for hi in range(nh):
        s_ref[hi] = states[hi]

    nw = nw_ref[...]
    for hi in range(nh):
        cols = slice(hi * dk, (hi + 1) * dk)
        o = jnp.concatenate(outs[hi], axis=0)
        o_ref[:, cols] = (_rms(o, nw) * _silu(z_ref[:, cols].astype(F32))).astype(o_ref.dtype)


def _gdn(main, side, a_log, dt_bias, norm_w, *, batch, seq, lb, nh):
    t = main.shape[0]
    ns = seq // lb
    hh = GDN_HEADS
    ng = hh // nh
    w = nh * GDN_D
    hp = jnp.zeros((8, LANES), F32)
    hp = hp.at[0, :hh].set(a_log).at[1, :hh].set(dt_bias)
    r = jnp.arange(lb)
    tril = ((r[:, None] >= r[None, :]) & (r[:, None] // GDN_CHUNK == r[None, :] // GDN_CHUNK)).astype(BF16)
    row = lambda b, h, s: b * ns + s
    return pl.pallas_call(
        functools.partial(_gdn_kernel, lb=lb, nh=nh),
        grid=(batch, ng, ns),
        in_specs=[
            pl.BlockSpec((lb, w), lambda b, h, s: (row(b, h, s), h)),
            pl.BlockSpec((lb, w), lambda b, h, s: (row(b, h, s), ng + h)),
            pl.BlockSpec((lb, w), lambda b, h, s: (row(b, h, s), 2 * ng + h)),
            pl.BlockSpec((lb, w), lambda b, h, s: (row(b, h, s), 3 * ng + h)),
            pl.BlockSpec((lb, LANES), lambda b, h, s: (row(b, h, s), 1)),
            pl.BlockSpec((8, LANES), lambda b, h, s: (0, 0)),
            pl.BlockSpec((1, GDN_D), lambda b, h, s: (0, 0)),
            pl.BlockSpec((lb, lb), lambda b, h, s: (0, 0)),
        ],
        out_specs=pl.BlockSpec((lb, w), lambda b, h, s: (row(b, h, s), h)),
        out_shape=jax.ShapeDtypeStruct((t, hh * GDN_D), BF16),
        scratch_shapes=[pltpu.VMEM((nh, GDN_D, GDN_D), F32)],
        compiler_params=_params(3), name="gdn",
    )(main, main, main, main, side, hp, norm_w.reshape(1, GDN_D), tril)


def _mla_up_kernel(cq_ref, ckv_ref, side_ref, tab_ref, qn_ref, kvn_ref, wq_ref, wk_ref, wvt_ref,
                   q_out, k_out, v_out):
    hh = MLA_HEADS
    tab = tab_ref[...]
    scale = (MLA_NOPE + MLA_ROPE) ** -0.5 * LOG2_E

    cq = _rms(cq_ref[...].astype(F32), qn_ref[...]).astype(BF16)
    ckv = _rms(ckv_ref[...].astype(F32), kvn_ref[...]).astype(BF16)
    q_all = _dot(cq, wq_ref[...])
    k_all = _dot(ckv, wk_ref[...])
    v_out[...] = _dot_nt(wvt_ref[...], ckv).astype(BF16)

    kr = side_ref[...] * tab
    kpe = (kr + pltpu.roll(kr, shift=MLA_ROPE, axis=1)).astype(BF16)
    rot = tab * scale
    for h in range(hh):
        q_out[:, h * 256:h * 256 + LANES] = (q_all[:, h * 256:h * 256 + LANES] * scale).astype(BF16)
        q_out[:, h * 256 + LANES:(h + 1) * 256] = (q_all[:, h * 256 + LANES:(h + 1) * 256] * rot).astype(BF16)
        k_out[:, h * 256:h * 256 + LANES] = k_all[:, h * LANES:(h + 1) * LANES].astype(BF16)
        k_out[:, h * 256 + LANES:(h + 1) * 256] = kpe


def _mla_up(main, side, tab, q_norm, kv_norm, wq, wk, wvt, *, bm):
    t = main.shape[0]
    r = MLA_RANK
    cq_blk = (GDN_HEADS * GDN_D * 4) // r
    return pl.pallas_call(
        _mla_up_kernel,
        grid=(t // bm,),
        in_specs=[
            pl.BlockSpec((bm, r), lambda i: (i, cq_blk)),
            pl.BlockSpec((bm, r), lambda i: (i, cq_blk + 1)),
            pl.BlockSpec((bm, LANES), lambda i: (i, 0)),
            pl.BlockSpec((bm, LANES), lambda i: (i, 0)),
            pl.BlockSpec((1, r), lambda i: (0, 0)),
            pl.BlockSpec((1, r), lambda i: (0, 0)),
            pl.BlockSpec(wq.shape, lambda i: (0, 0)),
            pl.BlockSpec(wk.shape, lambda i: (0, 0)),
            pl.BlockSpec(wvt.shape, lambda i: (0, 0)),
        ],
        out_specs=[pl.BlockSpec((bm, MLA_HEADS * 256), lambda i: (i, 0)),
                   pl.BlockSpec((bm, MLA_HEADS * 256), lambda i: (i, 0)),
                   pl.BlockSpec((MLA_HEADS * MLA_V, bm), lambda i: (0, i))],
        out_shape=[jax.ShapeDtypeStruct((t, MLA_HEADS * 256), BF16),
                   jax.ShapeDtypeStruct((t, MLA_HEADS * 256), BF16),
                   jax.ShapeDtypeStruct((MLA_HEADS * MLA_V, t), BF16)],
        compiler_params=_params(1), name="mla_up",
    )(main, main, side, tab, q_norm.reshape(1, r), kv_norm.reshape(1, r), wq, wk, wvt)


def _flash_kernel(q_ref, k_ref, vt_ref, o_ref, *, blk, nh):
    qi = pl.program_id(2)
    dq = 2 * LANES
    qs = [q_ref[:, h * dq:(h + 1) * dq] for h in range(nh)]

    def step(j, carry, masked):
        start = pl.multiple_of(j * blk, blk)
        ss = [_dot_nt(k_ref[pl.ds(start, blk), h * dq:(h + 1) * dq], qs[h]) for h in range(nh)]
        if masked:
            key = lax.broadcasted_iota(jnp.int32, ss[0].shape, 0)
            qry = lax.broadcasted_iota(jnp.int32, ss[0].shape, 1)
            ss = [jnp.where(key <= qry, s, -jnp.inf) for s in ss]
        m_new = [jnp.maximum(carry[h][0], jnp.max(ss[h], axis=0, keepdims=True)) for h in range(nh)]
        ps = [jnp.exp2(ss[h] - m_new[h]) for h in range(nh)]
        pv = [_dot(vt_ref[h * MLA_V:(h + 1) * MLA_V, pl.ds(start, blk)], ps[h].astype(BF16)) for h in range(nh)]
        out = []
        for h in range(nh):
            m, l, acc = carry[h]
            alpha = jnp.exp2(m - m_new[h])
            out.append((m_new[h], l * alpha + jnp.sum(ps[h], axis=0, keepdims=True), acc * alpha + pv[h]))
        return tuple(out)

    init = tuple((jnp.full((1, blk), -jnp.inf, F32), jnp.zeros((1, blk), F32),
                  jnp.zeros((MLA_V, blk), F32)) for _ in range(nh))
    carry = lax.fori_loop(0, qi, lambda j, cr: step(j, cr, False), init)
    carry = step(qi, carry, True)
    for h in range(nh):
        _, l, acc = carry[h]
        o_ref[:, h * MLA_V:(h + 1) * MLA_V] = (acc / l).T.astype(o_ref.dtype)


def _flash(qf, kf, vt, *, batch, seq, blk, nh, casts=()):
    t = qf.shape[0]
    nq = seq // blk
    hh = MLA_HEADS
    return _call(
        functools.partial(_flash_kernel, blk=blk, nh=nh),
        grid=(batch, hh // nh, nq),
        in_specs=[
            pl.BlockSpec((blk, nh * 256), lambda b, h, i: (b * nq + i, h)),
            pl.BlockSpec((seq, nh * 256), lambda b, h, i: (b, h)),
            pl.BlockSpec((nh * MLA_V, seq), lambda b, h, i: (h, b)),
        ],
        out_specs=[pl.BlockSpec((blk, nh * MLA_V), lambda b, h, i: (b * nq + i, h))],
        out_shape=[jax.ShapeDtypeStruct((t, hh * MLA_V), BF16)],
        args=(qf, kf, vt), name="mla_flash", casts=casts)


def _ret_kernel(lg_ref, q_ref, k_ref, v_ref, g_ref, nw_ref, o_ref,
                s_ref, dec_ref, xz_ref, *, c, nh):
    @pl.when(pl.program_id(2) == 0)
    def _():
        row = lax.broadcasted_iota(jnp.int32, (c, c), 0)
        col = lax.broadcasted_iota(jnp.int32, (c, c), 1)
        pos = lax.broadcasted_iota(jnp.int32, (c, LANES), 0).astype(F32)
        lane = lax.broadcasted_iota(jnp.int32, (c, LANES), 1)
        for h in range(nh):
            lg = lg_ref[h, :, 0:1]
            dec_ref[h] = jnp.exp(jnp.where(row >= col, (row - col).astype(F32) * lg, -jnp.inf))
            xi = jnp.exp((pos + 1.0) * lg)
            zeta = jnp.exp((c - 1.0 - pos) * lg)
            xz_ref[h] = jnp.where(lane == 0, xi, zeta)
        s_ref[...] = jnp.zeros_like(s_ref)

    hs = range(nh)
    qb = [q_ref[:, h * RET_DK:(h + 1) * RET_DK] for h in hs]
    kb = [k_ref[:, h * RET_DK:(h + 1) * RET_DK] for h in hs]
    v = [v_ref[:, h * RET_DV:(h + 1) * RET_DV] for h in hs]
    qk = [(_dot_nt(qb[h], kb[h]) * dec_ref[h]).astype(BF16) for h in hs]
    states = [s_ref[h] for h in hs]
    cross = [_dot((qb[h].astype(F32) * xz_ref[h, :, 0:1]).astype(BF16), states[h].astype(BF16)) for h in hs]
    inner = [_dot(qk[h], v[h]) for h in hs]
    upd = [_dot_tn((kb[h].astype(F32) * xz_ref[h, :, 1:2]).astype(BF16), v[h]) for h in hs]
    for h in hs:
        s_ref[h] = states[h] * xz_ref[h, c - 1:c, 0:1] + upd[h]
    for h in hs:
        o = inner[h] + cross[h]
        mu = jnp.mean(o, axis=-1, keepdims=True)
        xc = o - mu
        var = jnp.mean(xc * xc, axis=-1, keepdims=True)
        cols = slice(h * RET_DV, (h + 1) * RET_DV)
        y = xc * lax.rsqrt(var + NORM_EPS) * nw_ref[:, cols]
        o_ref[:, cols] = g_ref[:, cols] * y.astype(o_ref.dtype)


def _retention(main, norm_w, *, batch, seq, c, nh, casts=()):
    t = main.shape[0]
    nc = seq // c
    hh = RET_HEADS
    ng = hh // nh
    log_gamma = jnp.log1p(-jnp.power(2.0, -5.0 - jnp.arange(hh, dtype=F32)))
    lg = jnp.broadcast_to(log_gamma[:, None, None], (hh, 1, LANES))
    row = lambda b, h, s: b * nc + s
    kq = RET_DK
    v_blk0 = (2 * hh * kq) // (nh * RET_DV)
    return _call(
        functools.partial(_ret_kernel, c=c, nh=nh),
        grid=(batch, ng, nc),
        in_specs=[
            pl.BlockSpec((nh, 1, LANES), lambda b, h, s: (h, 0, 0)),
            pl.BlockSpec((c, nh * kq), lambda b, h, s: (row(b, h, s), h)),
            pl.BlockSpec((c, nh * kq), lambda b, h, s: (row(b, h, s), ng + h)),
            pl.BlockSpec((c, nh * RET_DV), lambda b, h, s: (row(b, h, s), v_blk0 + h)),
            pl.BlockSpec((c, nh * RET_DV), lambda b, h, s: (row(b, h, s), v_blk0 + ng + h)),
            pl.BlockSpec((1, nh * RET_DV), lambda b, h, s: (0, h)),
        ],
        out_specs=[pl.BlockSpec((c, nh * RET_DV), lambda b, h, s: (row(b, h, s), h))],
        out_shape=[jax.ShapeDtypeStruct((t, hh * RET_DV), BF16)],
        scratch_shapes=[pltpu.VMEM((nh, kq, RET_DV), F32), pltpu.VMEM((nh, c, c), F32),
                        pltpu.VMEM((nh, c, LANES), F32)],
        args=(lg, main, main, main, main, norm_w.reshape(1, -1)), name="retention", casts=casts)


def _rotate_half_cols(w, half):
    return jnp.concatenate([-w[..., half:], w[..., :half]], axis=-1)


def _rope_tab(positions, dim):
    inv_freq = ROPE_THETA ** (-jnp.arange(0, dim, 2, dtype=F32) / dim)
    ang = positions.astype(F32).reshape(-1, 1) * inv_freq
    return jnp.cos(ang), jnp.sin(ang)


def kernel(x, p, positions, l0_attn_norm, l0_w_in, l0_gdn_conv, l0_gdn_A_log, l0_gdn_dt_bias, l0_gdn_norm, l0_mla_q_norm, l0_mla_w_uq, l0_mla_kv_norm, l0_mla_w_ukv, l0_w_out, l0_ffn_norm, l0_ffn_w_up, l0_ffn_conv_w, l0_ffn_conv_b, l0_ffn_w_down, l0_ple_proj, l0_ple_gate_norm, l0_ple_gate, l1_attn_norm, l1_w_in, l1_ret_norm, l1_w_out, l1_ffn_norm, l1_ffn_w_up, l1_ffn_conv_w, l1_ffn_conv_b, l1_ffn_w_down, l1_ple_proj, l1_ple_gate_norm, l1_ple_gate, final_norm):
    batch, seq, d = x.shape
    t = batch * seq
    x2 = x.reshape(t, d)
    p2 = p.reshape(p.shape[0], t, p.shape[-1])

    gq = GDN_HEADS * GDN_D * 4
    n_ab = 2 * GDN_HEADS
    c0 = gq + n_ab
    w_in0 = l0_w_in.astype(BF16)
    w_tail = w_in0[:, c0:c0 + 2 * MLA_RANK]
    w_kr = l0_w_in[:, c0 + 2 * MLA_RANK:]
    w_side = jnp.concatenate(
        [w_kr, _rotate_half_cols(w_kr, MLA_ROPE // 2), l0_w_in[:, gq:c0],
         jnp.zeros((d, LANES - n_ab), F32)], axis=1).astype(BF16)

    wq = l0_mla_w_uq.reshape(MLA_RANK, MLA_HEADS, MLA_NOPE + MLA_ROPE)
    wq_pe = wq[:, :, MLA_NOPE:]
    wq = jnp.concatenate([wq[:, :, :MLA_NOPE], wq_pe, _rotate_half_cols(wq_pe, MLA_ROPE // 2)], axis=-1)
    wq = wq.reshape(MLA_RANK, MLA_HEADS * 256).astype(BF16)
    wkv = l0_mla_w_ukv.reshape(MLA_RANK, MLA_HEADS, MLA_NOPE + MLA_V)
    wk = wkv[:, :, :MLA_NOPE].reshape(MLA_RANK, -1).astype(BF16)
    wvt = wkv[:, :, MLA_NOPE:].reshape(MLA_RANK, -1).T.astype(BF16)

    cos_m, sin_m = _rope_tab(positions, MLA_ROPE)
    tab_m = jnp.concatenate([cos_m, cos_m, sin_m, sin_m], axis=1)
    tab_r = jnp.concatenate(_rope_tab(positions, RET_DK), axis=1)

    main0, side0 = _in_proj0(x2, l0_attn_norm, w_in0, w_tail, w_side, l0_gdn_conv, seq=seq,
                             bm=ROW_BLOCK, bn=COL_BLOCK)
    y_a = _gdn(main0, side0, l0_gdn_A_log, l0_gdn_dt_bias, l0_gdn_norm,
               batch=batch, seq=seq, lb=GDN_TOKENS, nh=GDN_HEADS_PER_STEP)
    qf, kf, vt = _mla_up(main0, side0, tab_m, l0_mla_q_norm, l0_mla_kv_norm, wq, wk, wvt, bm=MLA_UP_ROWS)
    y_b, w_up = _flash(qf, kf, vt, batch=batch, seq=seq, blk=FLASH_BLOCK, nh=FLASH_HEADS_PER_STEP,
                       casts=[l0_ffn_w_up])
    w_out0 = l0_w_out.astype(BF16)
    ha = GDN_HEADS * GDN_D
    (h,) = _proj_res([(y_a, w_out0[:ha]), (y_b, w_out0[ha:])], x2, bm=ROW_BLOCK, bn=COL_BLOCK)
    act, w_down = _ffn_up(h, l0_ffn_norm, w_up, l0_ffn_conv_w, l0_ffn_conv_b, seq=seq,
                          bm=ROW_BLOCK, bn=COL_BLOCK_FFN, casts=[l0_ffn_w_down])
    h, w_gate = _proj_res([(act, w_down)], h, bm=ROW_BLOCK, bn=COL_BLOCK_FFN, casts=[l0_ple_gate])
    h, w_in1, w_out1 = _ple(h, p2, 0, l0_ple_gate_norm, l0_ple_proj.astype(BF16), w_gate,
                            bm=PLE_ROWS, bn=PLE_COLS, casts=[l1_w_in, l1_w_out])

    main1 = _in_proj1(h, l1_attn_norm, w_in1, tab_r, bm=ROW_BLOCK, bn=COL_BLOCK_IN1)
    y, w_up = _retention(main1, l1_ret_norm, batch=batch, seq=seq, c=RET_CHUNK, nh=RET_HEADS_PER_STEP,
                         casts=[l1_ffn_w_up])
    (h,) = _proj_res([(y, w_out1)], h, bm=ROW_BLOCK, bn=COL_BLOCK)
    act, w_down = _ffn_up(h, l1_ffn_norm, w_up, l1_ffn_conv_w, l1_ffn_conv_b, seq=seq,
                          bm=ROW_BLOCK, bn=COL_BLOCK_FFN, casts=[l1_ffn_w_down])
    h, w_gate = _proj_res([(act, w_down)], h, bm=ROW_BLOCK, bn=COL_BLOCK_FFN, casts=[l1_ple_gate])
    (h,) = _ple(h, p2, 1, l1_ple_gate_norm, l1_ple_proj.astype(BF16), w_gate, final_norm,
                bm=PLE_ROWS, bn=PLE_COLS)
    return h.reshape(batch, seq, d)
```

```python
import functools

import jax
import jax.numpy as jnp
from jax import lax
from jax.experimental import pallas as pl
from jax.experimental.pallas import tpu as pltpu

F32 = jnp.float32
BF16 = jnp.bfloat16

NORM_EPS = 1e-6
ROPE_THETA = 10000.0

GDN_HEADS = 8
GDN_D = 128
GDN_CONV = 4
GDN_CHUNK = 64
MLA_HEADS = 8
MLA_RANK = 512
MLA_NOPE = 128
MLA_ROPE = 64
MLA_V = 128
RET_HEADS = 8
RET_DK = 256
RET_DV = 512
FFN_CONV = 3

LANES = 128
VMEM_LIMIT = 56 * 1024 * 1024
ROW_BLOCK = 1024
COL_BLOCK = 1024
IN_PROJ0_STEPS = 4
COL_BLOCK_IN1 = 2048
COL_BLOCK_FFN = 512
PLE_ROWS, PLE_COLS = 512, 512
MLA_UP_ROWS = 512
GDN_TOKENS, GDN_HEADS_PER_STEP = 256, 8
FLASH_BLOCK, FLASH_HEADS_PER_STEP = 512, 4
RET_CHUNK, RET_HEADS_PER_STEP = 256, 8

LOG2_E = 1.4426950408889634
CONV_HALO = 16


def _params(n_grid):
    return pltpu.CompilerParams(dimension_semantics=("arbitrary",) * n_grid,
                                vmem_limit_bytes=VMEM_LIMIT)


def _rms(x, g):
    return x * lax.rsqrt(jnp.mean(x * x, axis=-1, keepdims=True) + NORM_EPS) * g


def _dot(a, b):
    return jnp.dot(a, b, preferred_element_type=F32)


def _dot_nt(a, b):
    return lax.dot_general(a, b, (((1,), (1,)), ((), ())), preferred_element_type=F32)


def _dot_tn(a, b):
    return lax.dot_general(a, b, (((0,), (0,)), ((), ())), preferred_element_type=F32)


def _silu(x):
    return x * jax.nn.sigmoid(x)


def _call(kernel_fn, *, grid, in_specs, out_specs, out_shape, args, name, scratch_shapes=(), casts=()):
    n_in, n_out = len(in_specs), len(out_specs)
    n_steps = 1
    for g in grid:
        n_steps *= g

    def flat_step(*idx):
        step = idx[0]
        for g, i in zip(grid[1:], idx[1:]):
            step = step * g + i
        return step

    in_specs, out_specs, out_shape, args = list(in_specs), list(out_specs), list(out_shape), list(args)
    for w in casts:
        rows, cols = w.shape[0] // n_steps, w.shape[1]
        assert rows * n_steps == w.shape[0] and rows % 16 == 0, (w.shape, n_steps)
        spec = pl.BlockSpec((None, rows, cols), lambda *idx: (flat_step(*idx), 0, 0))
        in_specs.append(spec)
        out_specs.append(spec)
        out_shape.append(jax.ShapeDtypeStruct((n_steps, rows, cols), BF16))
        args.append(w.reshape(n_steps, rows, cols))

    def body(*refs):
        ins, extra_in = refs[:n_in], refs[n_in:n_in + len(casts)]
        outs = refs[n_in + len(casts):n_in + len(casts) + n_out]
        extra_out = refs[n_in + len(casts) + n_out:n_in + 2 * len(casts) + n_out]
        scratch = refs[n_in + 2 * len(casts) + n_out:]
        for src, dst in zip(extra_in, extra_out):
            dst[...] = src[...].astype(BF16)
        kernel_fn(*ins, *outs, *scratch)

    out = pl.pallas_call(
        body, grid=grid, in_specs=in_specs, out_specs=out_specs, out_shape=out_shape,
        scratch_shapes=list(scratch_shapes), compiler_params=_params(len(grid)), name=name,
    )(*args)
    return list(out[:n_out]) + [o.reshape(w.shape) for o, w in zip(out[n_out:], casts)]


def _in_proj0_kernel(x_ref, xh_ref, g_ref, wc_ref, wz_ref, wt_ref, ws_ref, cw_ref, oc_ref, op_ref, os_ref,
                     xn_ref, u_ref, *, bm, blocks_per_seq, n_steps, n_z_steps):
    i = pl.program_id(0)
    j = pl.program_id(1)
    groups = oc_ref.shape[1] // GDN_D

    @pl.when(j == 0)
    def _():
        g = g_ref[...]
        halo = _rms(xh_ref[...], g)
        halo = jnp.where(i % blocks_per_seq == 0, 0.0, halo)
        xn_ref[0:CONV_HALO, :] = halo.astype(BF16)
        xn = _rms(x_ref[...], g).astype(BF16)
        xn_ref[CONV_HALO:, :] = xn
        os_ref[...] = _dot(xn, ws_ref[...])

    def step(c):
        wp_ref = wz_ref if c < n_z_steps else wt_ref
        u_ref[...] = _dot(xn_ref[...], wc_ref[...])
        op_ref[...] = _dot(xn_ref[CONV_HALO:, :], wp_ref[...]).astype(op_ref.dtype)
        for h in range(groups):
            cols = slice(h * GDN_D, (h + 1) * GDN_D)
            cw = cw_ref[:, cols]
            y = cw[GDN_CONV - 1:GDN_CONV] * u_ref[pl.ds(CONV_HALO, bm), cols]
            for s in range(1, GDN_CONV):
                y = y + cw[GDN_CONV - 1 - s:GDN_CONV - s] * u_ref[pl.ds(CONV_HALO - s, bm), cols]
            y = _silu(y)
            head = c * groups + h
            if head < 2 * GDN_HEADS:
                y = y * lax.rsqrt(jnp.sum(y * y, axis=-1, keepdims=True) + NORM_EPS)
            if head < GDN_HEADS:
                y = y * (GDN_D ** -0.5)
            oc_ref[:, cols] = y.astype(oc_ref.dtype)

    for c in range(n_steps):
        pl.when(j == c)(functools.partial(step, c))


def _in_proj0(x, g, w, w_tail, w_side, conv_w, *, seq, bm, n_steps):
    t, d = x.shape
    ns = w_side.shape[1]
    n_conv = conv_w.shape[1]
    n_z = GDN_HEADS * GDN_D
    bc = n_conv // n_steps
    bp = (n_z + w_tail.shape[1]) // n_steps
    n_z_steps = n_z // bp
    hb = bm // CONV_HALO
    kern = functools.partial(_in_proj0_kernel, bm=bm, blocks_per_seq=seq // bm, n_steps=n_steps,
                             n_z_steps=n_z_steps)
    return pl.pallas_call(
        kern,
        grid=(t // bm, n_steps),
        in_specs=[
            pl.BlockSpec((bm, d), lambda i, j: (i, 0)),
            pl.BlockSpec((CONV_HALO, d), lambda i, j: (jnp.maximum(i * hb - 1, 0), 0)),
            pl.BlockSpec((1, d), lambda i, j: (0, 0)),
            pl.BlockSpec((d, bc), lambda i, j: (0, j)),
            pl.BlockSpec((d, bp), lambda i, j: (0, n_conv // bp + jnp.minimum(j, n_z_steps - 1))),
            pl.BlockSpec((d, bp), lambda i, j: (0, jnp.maximum(j - n_z_steps, 0))),
            pl.BlockSpec((d, ns), lambda i, j: (0, 0)),
            pl.BlockSpec((GDN_CONV, bc), lambda i, j: (0, j)),
        ],
        out_specs=[pl.BlockSpec((bm, bc), lambda i, j: (i, j)),
                   pl.BlockSpec((bm, bp), lambda i, j: (i, j)),
                   pl.BlockSpec((bm, ns), lambda i, j: (i, 0))],
        out_shape=[jax.ShapeDtypeStruct((t, n_conv), BF16), jax.ShapeDtypeStruct((t, n_steps * bp), BF16),
                   jax.ShapeDtypeStruct((t, ns), F32)],
        scratch_shapes=[pltpu.VMEM((bm + CONV_HALO, d), BF16), pltpu.VMEM((bm + CONV_HALO, bc), F32)],
        compiler_params=_params(2), name="in_proj0",
    )(x, x, g.reshape(1, d), w, w, w_tail, w_side, conv_w)


def _in_proj1_kernel(x_ref, g_ref, w_ref, cs_ref, o_ref, xn_ref, *, n_q, n_k, n_v):
    j = pl.program_id(1)
    half = RET_DK // 2

    @pl.when(j == 0)
    def _():
        xn_ref[...] = _rms(x_ref[...], g_ref[...]).astype(BF16)

    def proj():
        return _dot(xn_ref[...], w_ref[...])

    def rope_store(scale):
        y = proj()
        cos = cs_ref[:, :half]
        sin = cs_ref[:, half:]
        for h in range(y.shape[1] // RET_DK):
            x1 = y[:, h * RET_DK:h * RET_DK + half]
            x2 = y[:, h * RET_DK + half:(h + 1) * RET_DK]
            o_ref[:, h * RET_DK:h * RET_DK + half] = ((x1 * cos - x2 * sin) * scale).astype(o_ref.dtype)
            o_ref[:, h * RET_DK + half:(h + 1) * RET_DK] = ((x2 * cos + x1 * sin) * scale).astype(o_ref.dtype)

    @pl.when(j < n_q)
    def _():
        rope_store(1.0)

    @pl.when((j >= n_q) & (j < n_q + n_k))
    def _():
        rope_store(RET_DK ** -0.5)

    @pl.when((j >= n_q + n_k) & (j < n_q + n_k + n_v))
    def _():
        o_ref[...] = proj().astype(o_ref.dtype)

    @pl.when(j >= n_q + n_k + n_v)
    def _():
        o_ref[...] = _silu(proj()).astype(o_ref.dtype)


def _in_proj1(x, g, w, tab, *, bm, bn):
    t, d = x.shape
    n = w.shape[1]
    n_q = RET_HEADS * RET_DK // bn
    n_v = RET_HEADS * RET_DV // bn
    return pl.pallas_call(
        functools.partial(_in_proj1_kernel, n_q=n_q, n_k=n_q, n_v=n_v),
        grid=(t // bm, n // bn),
        in_specs=[pl.BlockSpec((bm, d), lambda i, j: (i, 0)),
                  pl.BlockSpec((1, d), lambda i, j: (0, 0)),
                  pl.BlockSpec((d, bn), lambda i, j: (0, j)),
                  pl.BlockSpec((bm, RET_DK), lambda i, j: (i, 0))],
        out_specs=pl.BlockSpec((bm, bn), lambda i, j: (i, j)),
        out_shape=jax.ShapeDtypeStruct((t, n), BF16),
        scratch_shapes=[pltpu.VMEM((bm, d), BF16)],
        compiler_params=_params(2), name="in_proj1",
    )(x, g.reshape(1, d), w, tab)


def _proj_res_kernel(*refs, n_pairs):
    res_ref, o_ref = refs[2 * n_pairs], refs[2 * n_pairs + 1]
    acc = res_ref[...]
    for p in range(n_pairs):
        acc = acc + _dot(refs[2 * p][...], refs[2 * p + 1][...])
    o_ref[...] = acc


def _proj_res(pairs, res, *, bm, bn, casts=()):
    t, n = res.shape
    in_specs, args = [], []
    for y, w in pairs:
        kk = y.shape[1]
        in_specs += [pl.BlockSpec((bm, kk), lambda i, j: (i, 0)),
                     pl.BlockSpec((kk, bn), lambda i, j: (0, j))]
        args += [y, w]
    in_specs.append(pl.BlockSpec((bm, bn), lambda i, j: (i, j)))
    args.append(res)
    return _call(
        functools.partial(_proj_res_kernel, n_pairs=len(pairs)),
        grid=(t // bm, n // bn),
        in_specs=in_specs,
        out_specs=[pl.BlockSpec((bm, bn), lambda i, j: (i, j))],
        out_shape=[jax.ShapeDtypeStruct((t, n), F32)],
        args=args, name="proj_res", casts=casts)


def _ffn_up_kernel(h_ref, hh_ref, g_ref, wg_ref, wu_ref, cg_ref, cu_ref, bg_ref, bu_ref,
                   o_ref, xn_ref, ug_ref, uu_ref, *, bm, blocks_per_seq):
    i = pl.program_id(0)

    @pl.when(pl.program_id(1) == 0)
    def _():
        g = g_ref[...]
        halo = _rms(hh_ref[...], g)
        halo = jnp.where(i % blocks_per_seq == 0, 0.0, halo)
        xn_ref[0:CONV_HALO, :] = halo.astype(BF16)
        xn_ref[CONV_HALO:, :] = _rms(h_ref[...], g).astype(BF16)

    xn = xn_ref[...]
    ug_ref[...] = _dot(xn, wg_ref[...])
    uu_ref[...] = _dot(xn, wu_ref[...])

    def conv(u_ref, c_ref, b_ref):
        cw = c_ref[...]
        y = b_ref[...] + cw[FFN_CONV - 1:FFN_CONV] * u_ref[pl.ds(CONV_HALO, bm), :]
        for s in range(1, FFN_CONV):
            y = y + cw[FFN_CONV - 1 - s:FFN_CONV - s] * u_ref[pl.ds(CONV_HALO - s, bm), :]
        return y

    gate = conv(ug_ref, cg_ref, bg_ref)
    up = conv(uu_ref, cu_ref, bu_ref)
    o_ref[...] = (_silu(gate) * up).astype(o_ref.dtype)


def _ffn_up(h, g, w_up, conv_w, conv_b, *, seq, bm, bn, casts=()):
    t, d = h.shape
    dff = w_up.shape[1] // 2
    nj = dff // bn
    hb = bm // CONV_HALO
    kern = functools.partial(_ffn_up_kernel, bm=bm, blocks_per_seq=seq // bm)
    return _call(
        kern,
        grid=(t // bm, nj),
        in_specs=[
            pl.BlockSpec((bm, d), lambda i, j: (i, 0)),
            pl.BlockSpec((CONV_HALO, d), lambda i, j: (jnp.maximum(i * hb - 1, 0), 0)),
            pl.BlockSpec((1, d), lambda i, j: (0, 0)),
            pl.BlockSpec((d, bn), lambda i, j: (0, j)),
            pl.BlockSpec((d, bn), lambda i, j: (0, j + nj)),
            pl.BlockSpec((FFN_CONV, bn), lambda i, j: (0, j)),
            pl.BlockSpec((FFN_CONV, bn), lambda i, j: (0, j + nj)),
            pl.BlockSpec((1, bn), lambda i, j: (0, j)),
            pl.BlockSpec((1, bn), lambda i, j: (0, j + nj)),
        ],
        out_specs=[pl.BlockSpec((bm, bn), lambda i, j: (i, j))],
        out_shape=[jax.ShapeDtypeStruct((t, dff), BF16)],
        scratch_shapes=[pltpu.VMEM((bm + CONV_HALO, d), BF16),
                        pltpu.VMEM((bm + CONV_HALO, bn), F32),
                        pltpu.VMEM((bm + CONV_HALO, bn), F32)],
        args=(h, h, g.reshape(1, d), w_up, w_up, conv_w, conv_w,
              conv_b.reshape(1, -1), conv_b.reshape(1, -1)),
        name="ffn_up", casts=casts)


def _ple_kernel(h_ref, p_ref, gn_ref, wp_ref, wg_ref, *rest, bn, final):
    if final:
        fn_ref, o_ref = rest
    else:
        (o_ref,) = rest
    d = h_ref.shape[1]
    hn = _rms(h_ref[...], gn_ref[...]).astype(BF16)
    pb = p_ref[...].astype(BF16)
    ssq = None
    for c in range(d // bn):
        cols = slice(c * bn, (c + 1) * bn)
        gate = _dot(hn, wg_ref[:, cols])
        proj = _dot(pb, wp_ref[:, cols])
        out = h_ref[:, cols] + proj * jax.nn.sigmoid(gate)
        o_ref[:, cols] = out
        if final:
            part = jnp.sum(out * out, axis=-1, keepdims=True)
            ssq = part if ssq is None else ssq + part
    if final:
        scale = lax.rsqrt(ssq / d + NORM_EPS)
        o_ref[...] = o_ref[...] * scale * fn_ref[...]


def _ple(h, p, layer, gate_norm, w_proj, w_gate, final_norm=None, *, bm, bn, casts=()):
    t, d = h.shape
    dp = p.shape[-1]
    final = final_norm is not None
    in_specs = [pl.BlockSpec((bm, d), lambda i: (i, 0)),
                pl.BlockSpec((None, bm, dp), lambda i: (layer, i, 0)),
                pl.BlockSpec((1, d), lambda i: (0, 0)),
                pl.BlockSpec((dp, d), lambda i: (0, 0)),
                pl.BlockSpec((d, d), lambda i: (0, 0))]
    args = [h, p, gate_norm.reshape(1, d), w_proj, w_gate]
    if final:
        in_specs.append(pl.BlockSpec((1, d), lambda i: (0, 0)))
        args.append(final_norm.reshape(1, d))
    return _call(
        functools.partial(_ple_kernel, bn=bn, final=final),
        grid=(t // bm,),
        in_specs=in_specs,
        out_specs=[pl.BlockSpec((bm, d), lambda i: (i, 0))],
        out_shape=[jax.ShapeDtypeStruct((t, d), F32)],
        args=args, name="ple", casts=casts)


def _split3(x):
    hi = x.astype(BF16).astype(F32)
    r1 = x - hi
    mid = r1.astype(BF16).astype(F32)
    lo = (r1 - mid).astype(BF16).astype(F32)
    return hi, mid, lo


def _lane_pick(x, lane, idx):
    return jnp.sum(jnp.where(lane == idx, x, 0.0), axis=-1, keepdims=True)


def _gdn_kernel(q_ref, k_ref, v_ref, z_ref, ab_ref, hp_ref, nw_ref, tril_ref, o_ref, s_ref, *, lb, nh):
    c = GDN_CHUNK
    n_chunks = lb // c
    dk = GDN_D

    @pl.when(pl.program_id(2) == 0)
    def _():
        s_ref[...] = jnp.zeros_like(s_ref)

    ab = ab_ref[...]
    lane = lax.broadcasted_iota(jnp.int32, ab.shape, 1)
    hp = hp_ref[...]
    xa = ab + hp[1:2]
    softplus = jnp.maximum(xa, 0.0) + jnp.log1p(jnp.exp(-jnp.abs(xa)))
    g_lanes = -jnp.exp(hp[0:1]) * softplus
    beta_lanes = jax.nn.sigmoid(ab)

    def pieces_to_lanes(x, first, sign):
        hi, mid, lo = _split3(x)
        out = jnp.where(lane == first, sign * hi, 0.0)
        out = jnp.where(lane == first + 1, sign * mid, out)
        return jnp.where(lane == first + 2, sign * lo, out)

    row2 = lax.broadcasted_iota(jnp.int32, (c, 2 * c), 0)
    lane2 = lax.broadcasted_iota(jnp.int32, (c, 2 * c), 1)
    col2 = lane2 & (c - 1)
    incl2 = row2 >= col2
    strict2 = row2 > col2
    left = lane2 < c
    eye_left = jnp.where(left & (row2 == lane2), 1.0, 0.0)
    zeros_k = jnp.zeros((c, 2 * c), BF16)
    zeros_x = jnp.zeros((c, 2 * dk), BF16)

    items = []
    for hi in range(nh):
        head = pl.program_id(1) * nh + hi
        cols = slice(hi * dk, (hi + 1) * dk)
        qb = q_ref[:, cols]
        kb16 = k_ref[:, cols]
        q = qb.astype(F32)
        k = kb16.astype(F32)
        v = v_ref[:, cols].astype(F32)
        g_col = _lane_pick(g_lanes, lane, head)
        beta = _lane_pick(beta_lanes, lane, head + GDN_HEADS)
        gm = _dot(tril_ref[...], pieces_to_lanes(g_col, 0, 1.0).astype(BF16))
        big_g = jnp.sum(gm, axis=-1, keepdims=True)
        e_g = jnp.exp(big_g)
        am = jnp.where(lane < 3, pieces_to_lanes(big_g, 0, 1.0), jnp.where(lane < 6, 1.0, 0.0)).astype(BF16)
        bmat = jnp.where(lane < 3, 1.0, pieces_to_lanes(big_g, 3, -1.0)).astype(BF16)
        kbeta = k * beta
        x0 = jnp.concatenate([v * beta, kbeta * e_g], axis=-1).astype(BF16)
        qd = q * e_g
        kbeta16 = kbeta.astype(BF16)
        for n in range(n_chunks):
            r = slice(n * c, (n + 1) * c)
            g_last = big_g[(n + 1) * c - 1:(n + 1) * c, :]
            items.append(dict(
                am=am[r], bm2=jnp.concatenate([bmat[r], bmat[r]], axis=0),
                kbq=jnp.concatenate([kbeta16[r], qb[r]], axis=0), k2=jnp.concatenate([kb16[r], kb16[r]], axis=0),
                x0=jnp.concatenate([x0[r], zeros_x], axis=0), qd=qd[r],
                kd=(k[r] * jnp.exp(g_last - big_g[r])).astype(BF16), gl=jnp.exp(g_last)))

    for it in items:
        it["gamma2"] = jnp.exp(jnp.where(incl2, _dot_nt(it["am"], it["bm2"]), -jnp.inf))
    for it in items:
        it["kq"] = _dot_nt(it["kbq"], it["k2"])
    for it in items:
        it["tp"] = jnp.where(left, eye_left, jnp.where(strict2, -it["kq"][:c] * it["gamma2"], 0.0))
        it["attn"] = (it["kq"][c:, :c] * it["gamma2"][:, :c]).astype(BF16)

    for j in range(6):
        for it in items:
            it["tpb"] = it["tp"].astype(BF16)
        for it in items:
            upd = _dot(it["tpb"], jnp.concatenate([zeros_k, it["tpb"]], axis=0))
            it["tp"] = jnp.where(left, it["tp"] + upd, upd)
    for it in items:
        it["tpb"] = it["tp"].astype(BF16)
    for it in items:
        it["xb"] = _dot(it["tpb"], it["x0"]).astype(BF16)

    for it in items:
        it["au"] = _dot(it["attn"], it["xb"])
    for it in items:
        it["ku"] = _dot_tn(it["kd"], it["xb"])
    for it in items:
        it["lhs"] = jnp.concatenate([it["qd"] - it["au"][:, dk:], it["ku"][:, dk:]], axis=0).astype(BF16)

    states = [s_ref[hi] for hi in range(nh)]
    outs = [[None] * n_chunks for _ in range(nh)]
    for n in range(n_chunks):
        for hi in range(nh):
            it = items[hi * n_chunks + n]
            prod = _dot(it["lhs"], states[hi].astype(BF16))
            outs[hi][n] = prod[:c] + it["au"][:, :dk]
            states[hi] = states[hi] * it["gl"] - prod[c:] + it["ku"][:, :dk]
    for hi in range(nh):
        s_ref[hi] = states[hi]

    nw = nw_ref[...]
    for hi in range(nh):
        cols = slice(hi * dk, (hi + 1) * dk)
        o = jnp.concatenate(outs[hi], axis=0)
        o_ref[:, cols] = (_rms(o, nw) * _silu(z_ref[:, cols].astype(F32))).astype(o_ref.dtype)


def _gdn(qkv, zc, side, a_log, dt_bias, norm_w, *, batch, seq, lb, nh):
    t = qkv.shape[0]
    ns = seq // lb
    hh = GDN_HEADS
    ng = hh // nh
    w = nh * GDN_D
    hp = jnp.zeros((8, LANES), F32)
    hp = hp.at[0, :hh].set(a_log).at[1, :hh].set(dt_bias)
    r = jnp.arange(lb)
    tril = ((r[:, None] >= r[None, :]) & (r[:, None] // GDN_CHUNK == r[None, :] // GDN_CHUNK)).astype(BF16)
    row = lambda b, h, s: b * ns + s
    return pl.pallas_call(
        functools.partial(_gdn_kernel, lb=lb, nh=nh),
        grid=(batch, ng, ns),
        in_specs=[
            pl.BlockSpec((lb, w), lambda b, h, s: (row(b, h, s), h)),
            pl.BlockSpec((lb, w), lambda b, h, s: (row(b, h, s), ng + h)),
            pl.BlockSpec((lb, w), lambda b, h, s: (row(b, h, s), 2 * ng + h)),
            pl.BlockSpec((lb, w), lambda b, h, s: (row(b, h, s), h)),
            pl.BlockSpec((lb, LANES), lambda b, h, s: (row(b, h, s), 1)),
            pl.BlockSpec((8, LANES), lambda b, h, s: (0, 0)),
            pl.BlockSpec((1, GDN_D), lambda b, h, s: (0, 0)),
            pl.BlockSpec((lb, lb), lambda b, h, s: (0, 0)),
        ],
        out_specs=pl.BlockSpec((lb, w), lambda b, h, s: (row(b, h, s), h)),
        out_shape=jax.ShapeDtypeStruct((t, hh * GDN_D), BF16),
        scratch_shapes=[pltpu.VMEM((nh, GDN_D, GDN_D), F32)],
        compiler_params=_params(3), name="gdn",
    )(qkv, qkv, qkv, zc, side, hp, norm_w.reshape(1, GDN_D), tril)


def _mla_up_kernel(cq_ref, ckv_ref, side_ref, tab_ref, qn_ref, kvn_ref, wq_ref, wk_ref, wvt_ref,
                   q_out, k_out, v_out):
    hh = MLA_HEADS
    tab = tab_ref[...]
    scale = (MLA_NOPE + MLA_ROPE) ** -0.5 * LOG2_E

    cq = _rms(cq_ref[...].astype(F32), qn_ref[...]).astype(BF16)
    ckv = _rms(ckv_ref[...].astype(F32), kvn_ref[...]).astype(BF16)
    q_all = _dot(cq, wq_ref[...])
    k_all = _dot(ckv, wk_ref[...])
    v_out[...] = _dot_nt(wvt_ref[...], ckv).astype(BF16)

    kr = side_ref[...] * tab
    kpe = (kr + pltpu.roll(kr, shift=MLA_ROPE, axis=1)).astype(BF16)
    rot = tab * scale
    for h in range(hh):
        q_out[:, h * 256:h * 256 + LANES] = (q_all[:, h * 256:h * 256 + LANES] * scale).astype(BF16)
        q_out[:, h * 256 + LANES:(h + 1) * 256] = (q_all[:, h * 256 + LANES:(h + 1) * 256] * rot).astype(BF16)
        k_out[:, h * 256:h * 256 + LANES] = k_all[:, h * LANES:(h + 1) * LANES].astype(BF16)
        k_out[:, h * 256 + LANES:(h + 1) * 256] = kpe


def _mla_up(zc, side, tab, q_norm, kv_norm, wq, wk, wvt, *, bm):
    t = zc.shape[0]
    r = MLA_RANK
    cq_blk = (GDN_HEADS * GDN_D) // r
    return pl.pallas_call(
        _mla_up_kernel,
        grid=(t // bm,),
        in_specs=[
            pl.BlockSpec((bm, r), lambda i: (i, cq_blk)),
            pl.BlockSpec((bm, r), lambda i: (i, cq_blk + 1)),
            pl.BlockSpec((bm, LANES), lambda i: (i, 0)),
            pl.BlockSpec((bm, LANES), lambda i: (i, 0)),
            pl.BlockSpec((1, r), lambda i: (0, 0)),
            pl.BlockSpec((1, r), lambda i: (0, 0)),
            pl.BlockSpec(wq.shape, lambda i: (0, 0)),
            pl.BlockSpec(wk.shape, lambda i: (0, 0)),
            pl.BlockSpec(wvt.shape, lambda i: (0, 0)),
        ],
        out_specs=[pl.BlockSpec((bm, MLA_HEADS * 256), lambda i: (i, 0)),
                   pl.BlockSpec((bm, MLA_HEADS * 256), lambda i: (i, 0)),
                   pl.BlockSpec((MLA_HEADS * MLA_V, bm), lambda i: (0, i))],
        out_shape=[jax.ShapeDtypeStruct((t, MLA_HEADS * 256), BF16),
                   jax.ShapeDtypeStruct((t, MLA_HEADS * 256), BF16),
                   jax.ShapeDtypeStruct((MLA_HEADS * MLA_V, t), BF16)],
        compiler_params=_params(1), name="mla_up",
    )(zc, zc, side, tab, q_norm.reshape(1, r), kv_norm.reshape(1, r), wq, wk, wvt)


def _flash_kernel(q_ref, k_ref, vt_ref, o_ref, *, blk, nh):
    qi = pl.program_id(2)
    dq = 2 * LANES
    qs = [q_ref[:, h * dq:(h + 1) * dq] for h in range(nh)]

    def step(j, carry, masked):
        start = pl.multiple_of(j * blk, blk)
        ss = [_dot_nt(k_ref[pl.ds(start, blk), h * dq:(h + 1) * dq], qs[h]) for h in range(nh)]
        if masked:
            key = lax.broadcasted_iota(jnp.int32, ss[0].shape, 0)
            qry = lax.broadcasted_iota(jnp.int32, ss[0].shape, 1)
            ss = [jnp.where(key <= qry, s, -jnp.inf) for s in ss]
        m_new = [jnp.maximum(carry[h][0], jnp.max(ss[h], axis=0, keepdims=True)) for h in range(nh)]
        ps = [jnp.exp2(ss[h] - m_new[h]) for h in range(nh)]
        pv = [_dot(vt_ref[h * MLA_V:(h + 1) * MLA_V, pl.ds(start, blk)], ps[h].astype(BF16)) for h in range(nh)]
        out = []
        for h in range(nh):
            m, l, acc = carry[h]
            alpha = jnp.exp2(m - m_new[h])
            out.append((m_new[h], l * alpha + jnp.sum(ps[h], axis=0, keepdims=True), acc * alpha + pv[h]))
        return tuple(out)

    init = tuple((jnp.full((1, blk), -jnp.inf, F32), jnp.zeros((1, blk), F32),
                  jnp.zeros((MLA_V, blk), F32)) for _ in range(nh))
    carry = lax.fori_loop(0, qi, lambda j, cr: step(j, cr, False), init)
    carry = step(qi, carry, True)
    for h in range(nh):
        _, l, acc = carry[h]
        o_ref[:, h * MLA_V:(h + 1) * MLA_V] = (acc / l).T.astype(o_ref.dtype)


def _flash(qf, kf, vt, *, batch, seq, blk, nh, casts=()):
    t = qf.shape[0]
    nq = seq // blk
    hh = MLA_HEADS
    return _call(
        functools.partial(_flash_kernel, blk=blk, nh=nh),
        grid=(batch, hh // nh, nq),
        in_specs=[
            pl.BlockSpec((blk, nh * 256), lambda b, h, i: (b * nq + i, h)),
            pl.BlockSpec((seq, nh * 256), lambda b, h, i: (b, h)),
            pl.BlockSpec((nh * MLA_V, seq), lambda b, h, i: (h, b)),
        ],
        out_specs=[pl.BlockSpec((blk, nh * MLA_V), lambda b, h, i: (b * nq + i, h))],
        out_shape=[jax.ShapeDtypeStruct((t, hh * MLA_V), BF16)],
        args=(qf, kf, vt), name="mla_flash", casts=casts)


def _ret_kernel(lg_ref, q_ref, k_ref, v_ref, g_ref, nw_ref, o_ref,
                s_ref, dec_ref, xz_ref, *, c, nh):
    @pl.when(pl.program_id(2) == 0)
    def _():
        row = lax.broadcasted_iota(jnp.int32, (c, c), 0)
        col = lax.broadcasted_iota(jnp.int32, (c, c), 1)
        pos = lax.broadcasted_iota(jnp.int32, (c, LANES), 0).astype(F32)
        lane = lax.broadcasted_iota(jnp.int32, (c, LANES), 1)
        for h in range(nh):
            lg = lg_ref[h, :, 0:1]
            dec_ref[h] = jnp.exp(jnp.where(row >= col, (row - col).astype(F32) * lg, -jnp.inf))
            xi = jnp.exp((pos + 1.0) * lg)
            zeta = jnp.exp((c - 1.0 - pos) * lg)
            xz_ref[h] = jnp.where(lane == 0, xi, zeta)
        s_ref[...] = jnp.zeros_like(s_ref)

    hs = range(nh)
    qb = [q_ref[:, h * RET_DK:(h + 1) * RET_DK] for h in hs]
    kb = [k_ref[:, h * RET_DK:(h + 1) * RET_DK] for h in hs]
    v = [v_ref[:, h * RET_DV:(h + 1) * RET_DV] for h in hs]
    qk = [(_dot_nt(qb[h], kb[h]) * dec_ref[h]).astype(BF16) for h in hs]
    states = [s_ref[h] for h in hs]
    cross = [_dot((qb[h].astype(F32) * xz_ref[h, :, 0:1]).astype(BF16), states[h].astype(BF16)) for h in hs]
    inner = [_dot(qk[h], v[h]) for h in hs]
    upd = [_dot_tn((kb[h].astype(F32) * xz_ref[h, :, 1:2]).astype(BF16), v[h]) for h in hs]
    for h in hs:
        s_ref[h] = states[h] * xz_ref[h, c - 1:c, 0:1] + upd[h]
    for h in hs:
        o = inner[h] + cross[h]
        mu = jnp.mean(o, axis=-1, keepdims=True)
        xc = o - mu
        var = jnp.mean(xc * xc, axis=-1, keepdims=True)
        cols = slice(h * RET_DV, (h + 1) * RET_DV)
        y = xc * lax.rsqrt(var + NORM_EPS) * nw_ref[:, cols]
        o_ref[:, cols] = g_ref[:, cols] * y.astype(o_ref.dtype)


def _retention(main, norm_w, *, batch, seq, c, nh, casts=()):
    t = main.shape[0]
    nc = seq // c
    hh = RET_HEADS
    ng = hh // nh
    log_gamma = jnp.log1p(-jnp.power(2.0, -5.0 - jnp.arange(hh, dtype=F32)))
    lg = jnp.broadcast_to(log_gamma[:, None, None], (hh, 1, LANES))
    row = lambda b, h, s: b * nc + s
    kq = RET_DK
    v_blk0 = (2 * hh * kq) // (nh * RET_DV)
    return _call(
        functools.partial(_ret_kernel, c=c, nh=nh),
        grid=(batch, ng, nc),
        in_specs=[
            pl.BlockSpec((nh, 1, LANES), lambda b, h, s: (h, 0, 0)),
            pl.BlockSpec((c, nh * kq), lambda b, h, s: (row(b, h, s), h)),
            pl.BlockSpec((c, nh * kq), lambda b, h, s: (row(b, h, s), ng + h)),
            pl.BlockSpec((c, nh * RET_DV), lambda b, h, s: (row(b, h, s), v_blk0 + h)),
            pl.BlockSpec((c, nh * RET_DV), lambda b, h, s: (row(b, h, s), v_blk0 + ng + h)),
            pl.BlockSpec((1, nh * RET_DV), lambda b, h, s: (0, h)),
        ],
        out_specs=[pl.BlockSpec((c, nh * RET_DV), lambda b, h, s: (row(b, h, s), h))],
        out_shape=[jax.ShapeDtypeStruct((t, hh * RET_DV), BF16)],
        scratch_shapes=[pltpu.VMEM((nh, kq, RET_DV), F32), pltpu.VMEM((nh, c, c), F32),
                        pltpu.VMEM((nh, c, LANES), F32)],
        args=(lg, main, main, main, main, norm_w.reshape(1, -1)), name="retention", casts=casts)


def _rotate_half_cols(w, half):
    return jnp.concatenate([-w[..., half:], w[..., :half]], axis=-1)


def _rope_tab(positions, dim):
    inv_freq = ROPE_THETA ** (-jnp.arange(0, dim, 2, dtype=F32) / dim)
    ang = positions.astype(F32).reshape(-1, 1) * inv_freq
    return jnp.cos(ang), jnp.sin(ang)


def kernel(x, p, positions, l0_attn_norm, l0_w_in, l0_gdn_conv, l0_gdn_A_log, l0_gdn_dt_bias, l0_gdn_norm, l0_mla_q_norm, l0_mla_w_uq, l0_mla_kv_norm, l0_mla_w_ukv, l0_w_out, l0_ffn_norm, l0_ffn_w_up, l0_ffn_conv_w, l0_ffn_conv_b, l0_ffn_w_down, l0_ple_proj, l0_ple_gate_norm, l0_ple_gate, l1_attn_norm, l1_w_in, l1_ret_norm, l1_w_out, l1_ffn_norm, l1_ffn_w_up, l1_ffn_conv_w, l1_ffn_conv_b, l1_ffn_w_down, l1_ple_proj, l1_ple_gate_norm, l1_ple_gate, final_norm):
    batch, seq, d = x.shape
    t = batch * seq
    x2 = x.reshape(t, d)
    p2 = p.reshape(p.shape[0], t, p.shape[-1])

    gq = GDN_HEADS * GDN_D * 4
    n_ab = 2 * GDN_HEADS
    c0 = gq + n_ab
    w_in0 = l0_w_in.astype(BF16)
    w_tail = w_in0[:, c0:c0 + 2 * MLA_RANK]
    w_kr = l0_w_in[:, c0 + 2 * MLA_RANK:]
    w_side = jnp.concatenate(
        [w_kr, _rotate_half_cols(w_kr, MLA_ROPE // 2), l0_w_in[:, gq:c0],
         jnp.zeros((d, LANES - n_ab), F32)], axis=1).astype(BF16)

    wq = l0_mla_w_uq.reshape(MLA_RANK, MLA_HEADS, MLA_NOPE + MLA_ROPE)
    wq_pe = wq[:, :, MLA_NOPE:]
    wq = jnp.concatenate([wq[:, :, :MLA_NOPE], wq_pe, _rotate_half_cols(wq_pe, MLA_ROPE // 2)], axis=-1)
    wq = wq.reshape(MLA_RANK, MLA_HEADS * 256).astype(BF16)
    wkv = l0_mla_w_ukv.reshape(MLA_RANK, MLA_HEADS, MLA_NOPE + MLA_V)
    wk = wkv[:, :, :MLA_NOPE].reshape(MLA_RANK, -1).astype(BF16)
    wvt = wkv[:, :, MLA_NOPE:].reshape(MLA_RANK, -1).T.astype(BF16)

    cos_m, sin_m = _rope_tab(positions, MLA_ROPE)
    tab_m = jnp.concatenate([cos_m, cos_m, sin_m, sin_m], axis=1)
    tab_r = jnp.concatenate(_rope_tab(positions, RET_DK), axis=1)

    qkv0, zc0, side0 = _in_proj0(x2, l0_attn_norm, w_in0, w_tail, w_side, l0_gdn_conv, seq=seq,
                                 bm=ROW_BLOCK, n_steps=IN_PROJ0_STEPS)
    y_a = _gdn(qkv0, zc0, side0, l0_gdn_A_log, l0_gdn_dt_bias, l0_gdn_norm,
               batch=batch, seq=seq, lb=GDN_TOKENS, nh=GDN_HEADS_PER_STEP)
    qf, kf, vt = _mla_up(zc0, side0, tab_m, l0_mla_q_norm, l0_mla_kv_norm, wq, wk, wvt, bm=MLA_UP_ROWS)
    y_b, w_up = _flash(qf, kf, vt, batch=batch, seq=seq, blk=FLASH_BLOCK, nh=FLASH_HEADS_PER_STEP,
                       casts=[l0_ffn_w_up])
    w_out0 = l0_w_out.astype(BF16)
    ha = GDN_HEADS * GDN_D
    (h,) = _proj_res([(y_a, w_out0[:ha]), (y_b, w_out0[ha:])], x2, bm=ROW_BLOCK, bn=COL_BLOCK)
    act, w_down = _ffn_up(h, l0_ffn_norm, w_up, l0_ffn_conv_w, l0_ffn_conv_b, seq=seq,
                          bm=ROW_BLOCK, bn=COL_BLOCK_FFN, casts=[l0_ffn_w_down])
    h, w_gate = _proj_res([(act, w_down)], h, bm=ROW_BLOCK, bn=COL_BLOCK_FFN, casts=[l0_ple_gate])
    h, w_in1, w_out1 = _ple(h, p2, 0, l0_ple_gate_norm, l0_ple_proj.astype(BF16), w_gate,
                            bm=PLE_ROWS, bn=PLE_COLS, casts=[l1_w_in, l1_w_out])

    main1 = _in_proj1(h, l1_attn_norm, w_in1, tab_r, bm=ROW_BLOCK, bn=COL_BLOCK_IN1)
    y, w_up = _retention(main1, l1_ret_norm, batch=batch, seq=seq, c=RET_CHUNK, nh=RET_HEADS_PER_STEP,
                         casts=[l1_ffn_w_up])
    (h,) = _proj_res([(y, w_out1)], h, bm=ROW_BLOCK, bn=COL_BLOCK)
    act, w_down = _ffn_up(h, l1_ffn_norm, w_up, l1_ffn_conv_w, l1_ffn_conv_b, seq=seq,
                          bm=ROW_BLOCK, bn=COL_BLOCK_FFN, casts=[l1_ffn_w_down])
    h, w_gate = _proj_res([(act, w_down)], h, bm=ROW_BLOCK, bn=COL_BLOCK_FFN, casts=[l1_ple_gate])
    (h,) = _ple(h, p2, 1, l1_ple_gate_norm, l1_ple_proj.astype(BF16), w_gate, final_norm,
                bm=PLE_ROWS, bn=PLE_COLS)
    return h.reshape(batch, seq, d)
```
